```python
import math
import jax
import jax.numpy as jnp
from jax import lax
import numpy as np

D_MODEL = 1024
BATCH = 8
SEQ = 2048
DEPTH = 4
DEC_BATCH = 128
DEC_SEQ = 8
PAST_LEN = 16384
PAGE_SIZE = 128

N_EVEN = (DEPTH + 1) // 2
N_ODD = DEPTH // 2
MIX_W = D_MODEL
GROUP_W = MIX_W // 2
H_A = 4
DK_A = GROUP_W // H_A
DV_A = GROUP_W // H_A
CONV_W = 4
CHUNK_A = 64
HD_B = 64
H_B = GROUP_W // HD_B
LR_W = 64
LR_A = 64
LR_G = 128
GN_EPS = 64e-5
H_C = 4
DV_C = GROUP_W // H_C
DK_C = DV_C // 2
LR_GK = 16
GLA_NORM = 16.0
CHUNK_C = 16
H_D = 4
DV_D = GROUP_W // H_D
DK_D = DV_D // 2
CHUNK_D = 64
D_FF = ((8 * D_MODEL // 3 + 255) // 256) * 256
N_MOD = 9
EPS = 1e-6

QK_A = H_A * DK_A
V_A = H_A * DV_A
A_WIDTHS = (QK_A, QK_A, V_A, H_A, H_A, V_A)
V_B = H_B * HD_B
B_WIDTHS = (V_B, V_B, V_B, LR_W, LR_A, LR_G)
QK_C = H_C * DK_C
V_C = H_C * DV_C
C_WIDTHS = (QK_C, QK_C, V_C, LR_GK, V_C)
QK_D = H_D * DK_D
V_D = H_D * DV_D
D_WIDTHS = (QK_D, QK_D, V_D, H_D, H_D, V_D)
P_A = sum(A_WIDTHS)
P_B = sum(B_WIDTHS)
P_C = sum(C_WIDTHS)
P_D = sum(D_WIDTHS)
P_EVEN = P_A + P_B
P_ODD = P_C + P_D

kernel_name = 'hybrid_delta_rwkv7_gla_mlstm_macaron_adaln_step'


def _split(x, widths):
    idx = [int(s) for s in np.cumsum(widths)[:-1]]
    return jnp.split(x, idx, axis=-1)


def _chunk_len(t, cap):
    n = min(cap, t)
    while t % n:
        n -= 1
    return n


def _rms(x, eps=EPS):
    return x * lax.rsqrt(jnp.mean(jnp.square(x), -1, keepdims=True) + eps)


def _l2n(x):
    return x * lax.rsqrt(jnp.sum(jnp.square(x), -1, keepdims=True) + 1e-6)


def _heads(x, h):
    return x.reshape(*x.shape[:-1], h, x.shape[-1] // h)


def _to_blocks(x, n):
    b, t = x.shape[:2]
    x = x.reshape(b, t // n, n, *x.shape[2:])
    return jnp.moveaxis(x, 3, 2)


def _from_blocks(x):
    x = jnp.moveaxis(x, 2, 3)
    return x.reshape(x.shape[0], -1, *x.shape[3:])


def _adaln(x, gain, mod, j):
    n = _rms(x.astype(jnp.float32)) * gain
    return (n * (1.0 + mod[:, 3 * j + 1]) + mod[:, 3 * j]).astype(x.dtype)


def _swiglu(h, w_gate, w_up, w_down):
    return (jax.nn.silu(h @ w_gate) * (h @ w_up)) @ w_down


def _gated_delta(q, k, v, g, beta, s0):
    n = _chunk_len(q.shape[1], CHUNK_A)
    q, k, v = (_to_blocks(z, n) for z in (q, k, v))
    g, beta = _to_blocks(g, n), _to_blocks(beta, n)
    gc = jnp.cumsum(g, -1)
    causal = jnp.tril(jnp.ones((n, n), bool))
    strict = jnp.tril(jnp.ones((n, n), bool), -1)
    decay = jnp.exp(jnp.where(causal, gc[..., :, None] - gc[..., None, :], -jnp.inf))
    kb = k * beta[..., None]
    a_low = jnp.where(strict, jnp.einsum('bnhik,bnhjk->bnhij', kb, k) * decay, 0.0)
    eye = jnp.eye(n, dtype=a_low.dtype)
    rhs = jnp.concatenate([v * beta[..., None], kb * jnp.exp(gc)[..., None]], -1)
    sol = lax.linalg.triangular_solve(a_low + eye, rhs, left_side=True, lower=True, unit_diagonal=True)
    dv = v.shape[-1]
    value, kcum = sol[..., :dv], sol[..., dv:]
    qk = jnp.einsum('bnhik,bnhjk->bnhij', q, k) * decay
    qg = q * jnp.exp(gc)[..., None]
    kd = k * jnp.exp(gc[..., -1:] - gc)[..., None]
    gl = jnp.exp(gc[..., -1])

    def step(s, xs):
        qk_c, qg_c, kcum_c, val_c, kd_c, gl_c = xs
        u = val_c - jnp.einsum('bhlk,bhkv->bhlv', kcum_c, s)
        o = jnp.einsum('bhlk,bhkv->bhlv', qg_c, s) + jnp.einsum('bhij,bhjv->bhiv', qk_c, u)
        s = s * gl_c[..., None, None] + jnp.einsum('bhlk,bhlv->bhkv', kd_c, u)
        return s, o

    xs = tuple(jnp.moveaxis(z, 1, 0) for z in (qk, qg, kcum, value, kd, gl))
    s, o = lax.scan(step, s0, xs)
    return _from_blocks(jnp.moveaxis(o, 0, 1)), s


def _rwkv7(r, w, k, v, a, b, s0):
    def step(s, xs):
        r_t, w_t, k_t, v_t, a_t, b_t = xs
        sa = jnp.einsum('bhvk,bhk->bhv', s, a_t)
        s = s * w_t[:, :, None, :] + sa[..., None] * b_t[:, :, None, :] + v_t[..., None] * k_t[:, :, None, :]
        return s, jnp.einsum('bhvk,bhk->bhv', s, r_t)

    xs = tuple(jnp.moveaxis(z, 1, 0) for z in (r, w, k, v, a, b))
    s, y = lax.scan(step, s0, xs)
    return jnp.moveaxis(y, 0, 1), s


def _gla(q, k, v, gk, s0):
    n = _chunk_len(q.shape[1], CHUNK_C)
    q, k, v, gk = (_to_blocks(z, n) for z in (q, k, v, gk))
    bc = jnp.cumsum(gk, -2)
    causal = jnp.tril(jnp.ones((n, n), bool))
    qd = q * jnp.exp(bc)
    att = jnp.where(causal, jnp.einsum('bnhik,bnhjk->bnhij', qd, k * jnp.exp(-bc)), 0.0)
    intra = jnp.einsum('bnhij,bnhjv->bnhiv', att, v)
    kd = k * jnp.exp(bc[..., -1:, :] - bc)
    gl = jnp.exp(bc[..., -1, :])

    def step(s, xs):
        qd_c, kd_c, v_c, gl_c, in_c = xs
        o = in_c + jnp.einsum('bhlk,bhkv->bhlv', qd_c, s)
        s = s * gl_c[..., None] + jnp.einsum('bhlk,bhlv->bhkv', kd_c, v_c)
        return s, o

    xs = tuple(jnp.moveaxis(z, 1, 0) for z in (qd, kd, v, gl, intra))
    s, o = lax.scan(step, s0, xs)
    return _from_blocks(jnp.moveaxis(o, 0, 1)), s


def _mlstm(q, k, v, ig, fg, c0, n0, m0):
    n = _chunk_len(q.shape[1], CHUNK_D)
    q, k, v = (_to_blocks(z, n) for z in (q, k, v))
    ig = _to_blocks(ig, n)
    fcum = jnp.cumsum(_to_blocks(jax.nn.log_sigmoid(fg), n), -1)
    causal = jnp.tril(jnp.ones((n, n), bool))
    log_d = jnp.where(causal, fcum[..., :, None] - fcum[..., None, :] + ig[..., None, :], -jnp.inf)
    m_in = jnp.max(log_d, -1)
    log_e = fcum[..., -1:] - fcum + ig
    m_e = jnp.max(log_e, -1)
    qk = jnp.einsum('bnhik,bnhjk->bnhij', q, k)

    def step(carry, xs):
        c, nn, m = carry
        f_c, ld_c, mi_c, me_c, le_c, qk_c, q_c, k_c, v_c = xs
        m_t = jnp.maximum(f_c + m[..., None], mi_c)
        w_in = jnp.exp(f_c + m[..., None] - m_t)
        dm = jnp.exp(ld_c - m_t[..., None]) * qk_c
        num = w_in[..., None] * jnp.einsum('bhlk,bhkv->bhlv', q_c, c) + jnp.einsum('bhij,bhjv->bhiv', dm, v_c)
        den = w_in * jnp.einsum('bhlk,bhk->bhl', q_c, nn) + jnp.sum(dm, -1)
        h_out = num / jnp.maximum(jnp.abs(den), jnp.exp(-m_t))[..., None]
        m_new = jnp.maximum(f_c[..., -1] + m, me_c)
        carry_decay = jnp.exp(f_c[..., -1] + m - m_new)
        ke = k_c * jnp.exp(le_c - m_new[..., None])[..., None]
        c = c * carry_decay[..., None, None] + jnp.einsum('bhlk,bhlv->bhkv', ke, v_c)
        nn = nn * carry_decay[..., None] + jnp.sum(ke, -2)
        return (c, nn, m_new), h_out

    xs = tuple(jnp.moveaxis(z, 1, 0) for z in (fcum, log_d, m_in, m_e, log_e, qk, q, k, v))
    (c, nn, m), hs = lax.scan(step, (c0, n0, m0), xs)
    return _from_blocks(jnp.moveaxis(hs, 0, 1)), c, nn, m


def _even_mixer(h, xbuf, s_delta, s_rwkv, w, i):
    f32 = jnp.float32
    b, t, _ = h.shape
    hx = jnp.concatenate([xbuf.astype(h.dtype), h], axis=1)
    proj = (hx @ w['w_in_even'][i]).astype(f32)
    pa, pb = proj[..., :P_A], proj[..., P_A:]
    n_qkv = 2 * QK_A + V_A
    cw = w['conv_w'][i].astype(f32)
    conv = sum(cw[j] * pa[:, j:j + t, :n_qkv] for j in range(CONV_W))
    q, k, v = _split(jax.nn.silu(conv), (QK_A, QK_A, V_A))
    _, _, _, a_in, b_in, z = _split(pa[:, CONV_W - 1:], A_WIDTHS)
    q = _l2n(_heads(q, H_A)) * (DK_A ** -0.5)
    k = _l2n(_heads(k, H_A))
    g = -jnp.exp(w['a_log'][i].astype(f32)) * jax.nn.softplus(a_in + w['dt_bias'][i])
    o_a, s_delta = _gated_delta(q, k, _heads(v, H_A), g, jax.nn.sigmoid(b_in), s_delta.astype(f32))
    o_a = (_rms(o_a) * w['gain_a'][i]).reshape(b, t, V_A) * jax.nn.silu(z)
    cur, prev = pb[:, CONV_W - 1:], pb[:, CONV_W - 2:-1]
    xm = cur + (prev - cur) * w['mu_b'][i]
    r, kb, vb, w_lr, a_lr, g_lr = _split(xm, B_WIDTHS)
    w_raw = w['w0_b'][i] + jnp.tanh(w_lr) @ w['w_w2'][i]
    decay = jnp.exp(-jnp.exp(-jax.nn.softplus(-w_raw) - 0.5))
    a = jax.nn.sigmoid(w['a0_b'][i] + a_lr @ w['w_a2'][i])
    gb = jax.nn.sigmoid(g_lr) @ w['w_g2'][i]
    kk = _l2n(_heads(kb * w['k_k'][i], H_B))
    kb = kb * (1.0 + (a - 1.0) * w['k_a'][i])
    r_h, k_h, v_h, a_h = (_heads(z_, H_B) for z_ in (r, kb, vb, a))
    y, s_rwkv = _rwkv7(r_h, _heads(decay, H_B), k_h, v_h, -kk, kk * a_h, s_rwkv.astype(f32))
    yc = y - jnp.mean(y, -1, keepdims=True)
    y = yc * lax.rsqrt(jnp.mean(jnp.square(yc), -1, keepdims=True) + GN_EPS)
    y = y.reshape(b, t, V_B) * w['lnx_gain'][i] + w['lnx_bias'][i]
    bonus = jnp.sum(r_h * k_h * w['r_k'][i], -1, keepdims=True) * v_h
    y = (y + bonus.reshape(b, t, V_B)) * gb
    out = jnp.concatenate([o_a, y], -1).astype(h.dtype) @ w['w_out_even'][i]
    return out, hx[:, -(CONV_W - 1):], s_delta.astype(h.dtype), s_rwkv.astype(h.dtype)


def _odd_mixer(h, s_gla, s_c, s_n, s_m, w, i):
    f32 = jnp.float32
    b, t, _ = h.shape
    proj = (h @ w['w_in_odd'][i]).astype(f32)
    pc, pd = proj[..., :P_C], proj[..., P_C:]
    q, k, v, gk_lr, z = _split(pc, C_WIDTHS)
    gk = jax.nn.log_sigmoid(gk_lr @ w['w_gk2'][i] + w['b_gk'][i]) / GLA_NORM
    o_c, s_gla = _gla(_heads(q, H_C) * (DK_C ** -0.5), _heads(k, H_C), _heads(v, H_C),
                      _heads(gk, H_C), s_gla.astype(f32))
    o_c = (_rms(o_c) * w['gain_c'][i]).reshape(b, t, V_C) * jax.nn.silu(z)
    q, k, v, i_in, f_in, o_in = _split(pd, D_WIDTHS)
    hd, s_c, s_n, s_m = _mlstm(_heads(q, H_D), _heads(k, H_D) * (DK_D ** -0.5), _heads(v, H_D),
                               i_in + w['b_i'][i], f_in + w['b_f'][i],
                               s_c.astype(f32), s_n.astype(f32), s_m.astype(f32))
    o_d = jax.nn.sigmoid(o_in) * (_rms(hd) * w['gain_d'][i]).reshape(b, t, V_D)
    out = jnp.concatenate([o_c, o_d], -1).astype(h.dtype) @ w['w_out_odd'][i]
    return out, s_gla.astype(h.dtype), s_c.astype(h.dtype), s_n.astype(h.dtype), s_m.astype(h.dtype)


def _trunk(x, c, states, w):
    xbuf, s_delta, s_rwkv, s_gla, s_mc, s_mn, s_mm = states
    new_even = ([], [], [])
    new_odd = ([], [], [], [])
    cs = jax.nn.silu(c.astype(jnp.float32))
    for l in range(DEPTH):
        mod = (cs @ w['w_mod'][l] + w['b_mod'][l]).reshape(c.shape[0], N_MOD, 1, D_MODEL)
        h = _adaln(x, w['norm_gain'][l, 0], mod, 0)
        f1 = _swiglu(h, w['w_ffn_gate'][l, 0], w['w_ffn_up'][l, 0], w['w_ffn_down'][l, 0])
        x = x + (0.5 * (1.0 + mod[:, 2]) * f1).astype(x.dtype)
        h = _adaln(x, w['norm_gain'][l, 1], mod, 1)
        if l % 2 == 0:
            i = l // 2
            o, xb, sd, sr = _even_mixer(h, xbuf[i], s_delta[i], s_rwkv[i], w, i)
            for lst, val in zip(new_even, (xb, sd, sr)):
                lst.append(val)
        else:
            i = l // 2
            o, sg, sc, sn, sm = _odd_mixer(h, s_gla[i], s_mc[i], s_mn[i], s_mm[i], w, i)
            for lst, val in zip(new_odd, (sg, sc, sn, sm)):
                lst.append(val)
        x = x + ((1.0 + mod[:, 5]) * o).astype(x.dtype)
        h = _adaln(x, w['norm_gain'][l, 2], mod, 2)
        f2 = _swiglu(h, w['w_ffn_gate'][l, 1], w['w_ffn_up'][l, 1], w['w_ffn_down'][l, 1])
        x = x + (0.5 * (1.0 + mod[:, 8]) * f2).astype(x.dtype)
    y = (_rms(x.astype(jnp.float32)) * w['final_gain']).astype(x.dtype)
    stacked = [jnp.stack(lst) for lst in new_even + new_odd]
    return (y, *stacked)


def setup_inputs(seed: int = 0) -> dict:
    key = jax.random.key(seed)
    ks = iter(jax.random.split(key, 64))
    f32 = jnp.float32

    def nrm(shape, scale):
        return jax.random.normal(next(ks), shape, f32) * scale

    def unif(shape, lo, hi):
        return jax.random.uniform(next(ks), shape, f32, lo, hi)

    d = D_MODEL
    dt = jnp.exp(unif((N_EVEN, H_A), math.log(1e-3), math.log(1e-1)))
    return {
        'x_prompt': nrm((BATCH, SEQ, d), 1.0),
        'x_sample': nrm((DEC_BATCH, DEC_SEQ, d), 1.0),
        'c_prompt': nrm((BATCH, d), 1.0),
        'c_sample': nrm((DEC_BATCH, d), 1.0),
        'state_xbuf_even': nrm((N_EVEN, DEC_BATCH, CONV_W - 1, d), 1.0),
        'state_delta': nrm((N_EVEN, DEC_BATCH, H_A, DK_A, DV_A), 0.3),
        'state_rwkv': nrm((N_EVEN, DEC_BATCH, H_B, HD_B, HD_B), 0.3),
        'state_gla': nrm((N_ODD, DEC_BATCH, H_C, DK_C, DV_C), 0.3),
        'state_mlstm_c': nrm((N_ODD, DEC_BATCH, H_D, DK_D, DV_D), 0.3),
        'state_mlstm_n': nrm((N_ODD, DEC_BATCH, H_D, DK_D), 0.3),
        'state_mlstm_m': nrm((N_ODD, DEC_BATCH, H_D), 1.0),
        'w_mod': nrm((DEPTH, d, N_MOD * d), 0.1 * d ** -0.5),
        'b_mod': nrm((DEPTH, N_MOD * d), 0.02),
        'norm_gain': 1.0 + nrm((DEPTH, 3, d), 0.02),
        'final_gain': 1.0 + nrm((d,), 0.02),
        'w_ffn_gate': nrm((DEPTH, 2, d, D_FF), d ** -0.5),
        'w_ffn_up': nrm((DEPTH, 2, d, D_FF), d ** -0.5),
        'w_ffn_down': nrm((DEPTH, 2, D_FF, d), D_FF ** -0.5),
        'w_in_even': nrm((N_EVEN, d, P_EVEN), d ** -0.5),
        'w_out_even': nrm((N_EVEN, MIX_W, d), MIX_W ** -0.5),
        'conv_w': nrm((N_EVEN, CONV_W, 2 * QK_A + V_A), CONV_W ** -0.5),
        'a_log': jnp.log(unif((N_EVEN, H_A), 1.0, 16.0)),
        'dt_bias': dt + jnp.log(-jnp.expm1(-dt)),
        'gain_a': 1.0 + nrm((N_EVEN, DV_A), 0.02),
        'mu_b': unif((N_EVEN, P_B), 0.0, 1.0),
        'w0_b': unif((N_EVEN, V_B), -6.0, -1.0),
        'w_w2': nrm((N_EVEN, LR_W, V_B), 0.5 * LR_W ** -0.5),
        'a0_b': nrm((N_EVEN, V_B), 0.1),
        'w_a2': nrm((N_EVEN, LR_A, V_B), LR_A ** -0.5),
        'w_g2': nrm((N_EVEN, LR_G, V_B), LR_G ** -0.5),
        'k_k': 0.85 + nrm((N_EVEN, V_B), 0.05),
        'k_a': 1.0 + nrm((N_EVEN, V_B), 0.05),
        'r_k': nrm((N_EVEN, H_B, HD_B), 0.1),
        'lnx_gain': 1.0 + nrm((N_EVEN, V_B), 0.02),
        'lnx_bias': nrm((N_EVEN, V_B), 0.02),
        'w_in_odd': nrm((N_ODD, d, P_ODD), d ** -0.5),
        'w_out_odd': nrm((N_ODD, MIX_W, d), MIX_W ** -0.5),
        'w_gk2': nrm((N_ODD, LR_GK, QK_C), LR_GK ** -0.5),
        'b_gk': nrm((N_ODD, QK_C), 0.1),
        'gain_c': 1.0 + nrm((N_ODD, DV_C), 0.02),
        'b_i': nrm((N_ODD, H_D), 0.1),
        'b_f': unif((N_ODD, H_D), 3.0, 6.0),
        'gain_d': 1.0 + nrm((N_ODD, DV_D), 0.02),
    }


def reference(x_prompt, x_sample, c_prompt, c_sample,
              state_xbuf_even, state_delta, state_rwkv, state_gla,
              state_mlstm_c, state_mlstm_n, state_mlstm_m,
              w_mod, b_mod, norm_gain, final_gain, w_ffn_gate, w_ffn_up, w_ffn_down,
              w_in_even, w_out_even, conv_w, a_log, dt_bias, gain_a,
              mu_b, w0_b, w_w2, a0_b, w_a2, w_g2, k_k, k_a, r_k, lnx_gain, lnx_bias,
              w_in_odd, w_out_odd, w_gk2, b_gk, gain_c, b_i, b_f, gain_d):
    w = dict(w_mod=w_mod, b_mod=b_mod, norm_gain=norm_gain, final_gain=final_gain,
             w_ffn_gate=w_ffn_gate, w_ffn_up=w_ffn_up, w_ffn_down=w_ffn_down,
             w_in_even=w_in_even, w_out_even=w_out_even, conv_w=conv_w, a_log=a_log,
             dt_bias=dt_bias, gain_a=gain_a, mu_b=mu_b, w0_b=w0_b, w_w2=w_w2, a0_b=a0_b,
             w_a2=w_a2, w_g2=w_g2, k_k=k_k, k_a=k_a, r_k=r_k, lnx_gain=lnx_gain,
             lnx_bias=lnx_bias, w_in_odd=w_in_odd, w_out_odd=w_out_odd, w_gk2=w_gk2,
             b_gk=b_gk, gain_c=gain_c, b_i=b_i, b_f=b_f, gain_d=gain_d)
    bp = x_prompt.shape[0]
    dtp = x_prompt.dtype
    zero_states = (
        jnp.zeros((N_EVEN, bp, CONV_W - 1, D_MODEL), dtp),
        jnp.zeros((N_EVEN, bp, H_A, DK_A, DV_A), dtp),
        jnp.zeros((N_EVEN, bp, H_B, HD_B, HD_B), dtp),
        jnp.zeros((N_ODD, bp, H_C, DK_C, DV_C), dtp),
        jnp.zeros((N_ODD, bp, H_D, DK_D, DV_D), dtp),
        jnp.zeros((N_ODD, bp, H_D, DK_D), dtp),
        jnp.zeros((N_ODD, bp, H_D), dtp),
    )
    y_p, xb_p, dl_p, rw_p, gl_p, mc_p, mn_p, mm_p = _trunk(x_prompt, c_prompt, zero_states, w)
    sample_states = (state_xbuf_even, state_delta, state_rwkv, state_gla,
                     state_mlstm_c, state_mlstm_n, state_mlstm_m)
    y_s, xb_s, dl_s, rw_s, gl_s, mc_s, mn_s, mm_s = _trunk(x_sample, c_sample, sample_states, w)
    return (y_p, y_s, xb_p, xb_s, dl_p, dl_s, rw_p, rw_s, gl_p, gl_s, mc_p, mc_s, mn_p, mn_s, mm_p, mm_s)
```

```python
import functools

import jax
import jax.numpy as jnp
from jax import lax
from jax.experimental import pallas as pl
from jax.experimental.pallas import tpu as pltpu

F32 = jnp.float32
BF16 = jnp.bfloat16
HIGHEST = lax.Precision.HIGHEST

D_MODEL = 1024
DEPTH = 4
N_EVEN = 2
N_ODD = 2
D_FF = 2816
N_MOD = 9
EPS = 1e-6
GN_EPS = 64e-5
CONV_W = 4
H_A, DK_A = 4, 128
H_B, HD_B = 8, 64
H_C, DK_C, DV_C = 4, 64, 128
H_D, DK_D, DV_D = 4, 64, 128
GLA_NORM = 16.0
GROUP_W = 512

LANES = 128
SUBLANES = 8
BF16_ROWS = 16
VMEM_LIMIT_BYTES = 48 * 1024 * 1024

EV_RKV, EV_QKV, EV_Z, EV_WAG, EV_AB, EV_COLS = 0, 1536, 3072, 3584, 3840, 4096
OD_CQK, OD_DQK, OD_CV, OD_CZ, OD_DV, OD_DO, OD_CGK, OD_DIF, OD_COLS = (
    0, 512, 1024, 1536, 2048, 2560, 3072, 3200, 3328)

CHUNK = 64
GLA_SUB = 16


def _bf(x):
    if x.dtype == BF16:
        return x
    if x.shape[-2] % BF16_ROWS == 0 and x.shape[-1] % BF16_ROWS == 0:
        return x.astype(BF16)
    return x


def _pair(a, b):
    a, b = _bf(a), _bf(b)
    if a.dtype != b.dtype:
        a, b = a.astype(F32), b.astype(F32)
    return a, b


def _mm(a, b):
    a, b = _pair(a, b)
    return jnp.dot(a, b, preferred_element_type=F32)


def _mm_nt(a, b):
    a, b = _pair(a, b)
    return lax.dot_general(a, b, (((1,), (1,)), ((), ())), preferred_element_type=F32)


def _mm_tn(a, b):
    a, b = _pair(a, b)
    return lax.dot_general(a, b, (((0,), (0,)), ((), ())), preferred_element_type=F32)


def _mm_f32(a, b):
    return jnp.dot(a, b, precision=HIGHEST, preferred_element_type=F32)


def _mm_nt_f32(a, b):
    return lax.dot_general(a, b, (((1,), (1,)), ((), ())), precision=HIGHEST,
                           preferred_element_type=F32)


def _sigmoid(x):
    return jax.nn.sigmoid(x)


def _silu(x):
    return x * jax.nn.sigmoid(x)


def _softplus(x):
    return jnp.maximum(x, 0.0) + jnp.log1p(jnp.exp(-jnp.abs(x)))


def _log_sigmoid(x):
    return -_softplus(-x)


def _rms(x, eps=EPS):
    return x * lax.rsqrt(jnp.mean(x * x, axis=-1, keepdims=True) + eps)


def _l2n(x):
    return x * lax.rsqrt(jnp.sum(x * x, axis=-1, keepdims=True) + 1e-6)


def _iota2(shape, dim):
    return lax.broadcasted_iota(jnp.int32, shape, dim)


def _tril_masks(n):
    r, c = _iota2((n, n), 0), _iota2((n, n), 1)
    return r >= c, r > c


def _eye(n):
    return (_iota2((n, n), 0) == _iota2((n, n), 1)).astype(F32)


def _transpose_rows(x):
    return _mm_nt_f32(_eye(x.shape[1]), x)


def _unit_lower_inv(a):
    n = a.shape[0]
    m = -a
    p = _eye(n) + m
    covered = 2
    while covered < n:
        m = _mm(m, m)
        p = p + _mm(p, m)
        covered *= 2
    return p


def _head_lane_mask(width, head_w, j):
    lane = _iota2((1, width), 1)
    return jnp.logical_and(lane >= j * head_w, lane < (j + 1) * head_w)


def _cparams(*sem):
    return pltpu.CompilerParams(dimension_semantics=sem, vmem_limit_bytes=VMEM_LIMIT_BYTES)


def _mod_kernel(c_ref, w_ref, b_ref, o_ref):
    cs = _silu(c_ref[...])
    o_ref[...] = _mm(cs, w_ref[...]) + b_ref[...]


def _mod_call(c_all, w_mod, b_mod):
    rows = c_all.shape[0]
    tn = 1024
    width = N_MOD * D_MODEL
    return pl.pallas_call(
        _mod_kernel,
        grid=(DEPTH, width // tn),
        in_specs=[
            pl.BlockSpec((rows, D_MODEL), lambda l, j: (0, 0)),
            pl.BlockSpec((None, D_MODEL, tn), lambda l, j: (l, 0, j)),
            pl.BlockSpec((None, 1, tn), lambda l, j: (l, 0, j)),
        ],
        out_specs=pl.BlockSpec((None, rows, tn), lambda l, j: (l, 0, j)),
        out_shape=jax.ShapeDtypeStruct((DEPTH, rows, width), F32),
        compiler_params=_cparams("parallel", "parallel"),
        name="mod",
    )(c_all, w_mod, b_mod.reshape(DEPTH, 1, width))


def _adaln(x, gain, scale, shift):
    return _rms(x) * gain * (1.0 + scale) + shift


def _ffn_kernel(x_ref, sh_ref, sc_ref, gt_ref, gain_ref, wg_ref, wu_ref, wd_ref, o_ref,
                h_scr, acc_scr):
    f = pl.program_id(2)
    bb, tb, d = x_ref.shape

    @pl.when(f == 0)
    def _():
        h = _adaln(x_ref[...], gain_ref[...], sc_ref[...], sh_ref[...])
        h_scr[...] = h.reshape(bb * tb, d).astype(BF16)
        acc_scr[...] = jnp.zeros_like(acc_scr)

    h = h_scr[...]
    g = jnp.dot(h, wg_ref[...], preferred_element_type=F32)
    u = jnp.dot(h, wu_ref[...], preferred_element_type=F32)
    a = (_silu(g) * u).astype(BF16)
    acc_scr[...] += jnp.dot(a, wd_ref[...], preferred_element_type=F32)

    @pl.when(f == pl.num_programs(2) - 1)
    def _():
        y = acc_scr[...].reshape(bb, tb, d)
        o_ref[...] = x_ref[...] + 0.5 * (1.0 + gt_ref[...]) * y


def _mod_spec(bb, j, ngrid):
    if ngrid == 3:
        return pl.BlockSpec((bb, None, 1, D_MODEL), lambda b, t, f: (b, j, 0, 0))
    return pl.BlockSpec((bb, None, 1, D_MODEL), lambda b, t: (b, j, 0, 0))


def _ffn_call(x, mod, gain, wg, wu, wd, j0, bb, tb, tf):
    b, t, d = x.shape
    xspec = pl.BlockSpec((bb, tb, d), lambda i, k, f: (i, k, 0))
    return pl.pallas_call(
        _ffn_kernel,
        grid=(b // bb, t // tb, D_FF // tf),
        in_specs=[
            xspec,
            _mod_spec(bb, j0, 3), _mod_spec(bb, j0 + 1, 3), _mod_spec(bb, j0 + 2, 3),
            pl.BlockSpec((1, d), lambda i, k, f: (0, 0)),
            pl.BlockSpec((d, tf), lambda i, k, f: (0, f)),
            pl.BlockSpec((d, tf), lambda i, k, f: (0, f)),
            pl.BlockSpec((tf, d), lambda i, k, f: (f, 0)),
        ],
        out_specs=xspec,
        out_shape=jax.ShapeDtypeStruct(x.shape, x.dtype),
        scratch_shapes=[pltpu.VMEM((bb * tb, d), BF16), pltpu.VMEM((bb * tb, d), F32)],
        compiler_params=_cparams("parallel", "parallel", "arbitrary"),
        name="ffn",
    )(x, mod, mod, mod, gain, wg, wu, wd)


def _adaln_proj_kernel(x_ref, sh_ref, sc_ref, gain_ref, w_ref, h_ref, p_ref, hb_scr):
    j = pl.program_id(2)
    bb, tb, d = x_ref.shape

    @pl.when(j == 0)
    def _():
        h = _adaln(x_ref[...], gain_ref[...], sc_ref[...], sh_ref[...])
        h_ref[...] = h
        hb_scr[...] = h.reshape(bb * tb, d).astype(BF16)

    p = jnp.dot(hb_scr[...], w_ref[...], preferred_element_type=F32)
    p_ref[...] = p.reshape(bb, tb, p.shape[-1])


def _adaln_proj_call(x, mod, gain, w, bb, tb, tn):
    b, t, d = x.shape
    n = w.shape[1]
    xspec = pl.BlockSpec((bb, tb, d), lambda i, k, j: (i, k, 0))
    return pl.pallas_call(
        _adaln_proj_kernel,
        grid=(b // bb, t // tb, n // tn),
        in_specs=[
            xspec, _mod_spec(bb, 3, 3), _mod_spec(bb, 4, 3),
            pl.BlockSpec((1, d), lambda i, k, j: (0, 0)),
            pl.BlockSpec((d, tn), lambda i, k, j: (0, j)),
        ],
        out_specs=[xspec, pl.BlockSpec((bb, tb, tn), lambda i, k, j: (i, k, j))],
        out_shape=[jax.ShapeDtypeStruct(x.shape, F32), jax.ShapeDtypeStruct((b, t, n), F32)],
        scratch_shapes=[pltpu.VMEM((bb * tb, d), BF16)],
        compiler_params=_cparams("parallel", "parallel", "arbitrary"),
        name="adaln_proj",
    )(x, mod, mod, gain, w)


def _rows_proj_kernel(a_ref, w_ref, o_ref):
    o_ref[...] = _mm(a_ref[...], w_ref[...])


def _rows_proj_call(a, w, tn):
    m, k = a.shape
    n = w.shape[1]
    return pl.pallas_call(
        _rows_proj_kernel,
        grid=(n // tn,),
        in_specs=[pl.BlockSpec((m, k), lambda j: (0, 0)), pl.BlockSpec((k, tn), lambda j: (0, j))],
        out_specs=pl.BlockSpec((m, tn), lambda j: (0, j)),
        out_shape=jax.ShapeDtypeStruct((m, n), F32),
        compiler_params=_cparams("parallel"),
        name="rows_proj",
    )(a, w)


def _outproj_kernel(x_ref, oa_ref, ob_ref, gt_ref, w_ref, o_ref):
    bb, tb, d = x_ref.shape
    o = jnp.concatenate([oa_ref[...], ob_ref[...]], axis=-1).reshape(bb * tb, d)
    y = jnp.dot(o.astype(BF16), w_ref[...], preferred_element_type=F32).reshape(bb, tb, d)
    o_ref[...] = x_ref[...] + (1.0 + gt_ref[...]) * y


def _outproj_call(x, oa, ob, mod, w, bb, tb):
    b, t, d = x.shape
    xspec = pl.BlockSpec((bb, tb, d), lambda i, k: (i, k, 0))
    hspec = pl.BlockSpec((bb, tb, GROUP_W), lambda i, k: (i, k, 0))
    return pl.pallas_call(
        _outproj_kernel,
        grid=(b // bb, t // tb),
        in_specs=[xspec, hspec, hspec, _mod_spec(bb, 5, 2),
                  pl.BlockSpec((d, d), lambda i, k: (0, 0))],
        out_specs=xspec,
        out_shape=jax.ShapeDtypeStruct(x.shape, x.dtype),
        compiler_params=_cparams("parallel", "parallel"),
        name="outproj",
    )(x, oa, ob, mod, w)


def _final_norm_kernel(x_ref, g_ref, o_ref):
    o_ref[...] = _rms(x_ref[...]) * g_ref[...]


def _final_norm_call(x, gain, bb, tb):
    b, t, d = x.shape
    xspec = pl.BlockSpec((bb, tb, d), lambda i, k: (i, k, 0))
    return pl.pallas_call(
        _final_norm_kernel,
        grid=(b // bb, t // tb),
        in_specs=[xspec, pl.BlockSpec((1, d), lambda i, k: (0, 0))],
        out_specs=xspec,
        out_shape=jax.ShapeDtypeStruct(x.shape, x.dtype),
        compiler_params=_cparams("parallel", "parallel"),
        name="final_norm",
    )(x, gain)


def _delta_kernel(qkv_ref, z_ref, ab_ref, carry_ref, s0_ref, cw_ref, hp_ref, gain_ref,
                  o_ref, s_ref, ext_scr, s_scr):
    t = pl.program_id(1)
    n = qkv_ref.shape[1]

    @pl.when(t == 0)
    def _():
        ext_scr[0:SUBLANES, :] = carry_ref[0]
        s_scr[...] = s0_ref[0]

    @pl.when(t > 0)
    def _():
        ext_scr[0:SUBLANES, :] = ext_scr[n:n + SUBLANES, :]

    ext_scr[SUBLANES:SUBLANES + n, :] = qkv_ref[0]
    conv = cw_ref[0:1, :] * ext_scr[5:5 + n, :]
    for j in range(1, CONV_W):
        conv = conv + cw_ref[j:j + 1, :] * ext_scr[5 + j:5 + j + n, :]
    x = _silu(conv)

    ab = ab_ref[0]
    g = -jnp.exp(hp_ref[0:1, :]) * _softplus(ab + hp_ref[1:2, :])
    beta = _sigmoid(ab)
    incl, strict = _tril_masks(n)
    gc = _mm_f32(incl.astype(F32), g)
    gct = _transpose_rows(gc)
    z = z_ref[0]

    for h in range(H_A):
        sl = slice(h * DK_A, (h + 1) * DK_A)
        q = _l2n(x[:, sl]) * (DK_A ** -0.5)
        k = _l2n(x[:, GROUP_W + h * DK_A:GROUP_W + (h + 1) * DK_A])
        v = x[:, 2 * GROUP_W + h * DK_A:2 * GROUP_W + (h + 1) * DK_A]
        bcol = beta[:, H_A + h:H_A + h + 1]
        gcol = gc[:, h:h + 1]
        grow = gct[h:h + 1, :]
        dec = jnp.exp(jnp.where(incl, gcol - grow, -jnp.inf))
        kb = k * bcol
        kq = _mm_nt(jnp.concatenate([kb, q], axis=0), k)
        a_low = jnp.where(strict, kq[:n] * dec, 0.0)
        tinv = _unit_lower_inv(a_low)
        eg = jnp.exp(gcol)
        sol = _mm(tinv, jnp.concatenate([v * bcol, kb * eg], axis=1))
        value, kcum = sol[:, :DK_A], sol[:, DK_A:]
        qk = kq[n:] * dec
        glast = gc[n - 1:n, h:h + 1]
        kd = k * jnp.exp(glast - gcol)
        s = s_scr[h]
        ks = _mm(jnp.concatenate([kcum, q * eg], axis=0), s)
        u = value - ks[:n]
        o = ks[n:] + _mm(qk, u)
        s_scr[h] = s * jnp.exp(glast) + _mm_tn(kd, u)
        o_ref[0, :, sl] = _rms(o) * gain_ref[...] * _silu(z[:, sl])

    @pl.when(t == pl.num_programs(1) - 1)
    def _():
        s_ref[0] = s_scr[...]


def _delta_call(proj, carry, s0, cw, hp, gain, n):
    b, t, _ = proj.shape
    return pl.pallas_call(
        _delta_kernel,
        grid=(b, t // n),
        in_specs=[
            pl.BlockSpec((1, n, 3 * GROUP_W), lambda i, k: (i, k, EV_QKV // (3 * GROUP_W))),
            pl.BlockSpec((1, n, GROUP_W), lambda i, k: (i, k, EV_Z // GROUP_W)),
            pl.BlockSpec((1, n, LANES), lambda i, k: (i, k, EV_AB // LANES)),
            pl.BlockSpec((1, SUBLANES, 3 * GROUP_W), lambda i, k: (i, 0, EV_QKV // (3 * GROUP_W))),
            pl.BlockSpec((1, H_A, DK_A, DK_A), lambda i, k: (i, 0, 0, 0)),
            pl.BlockSpec((SUBLANES, 3 * GROUP_W), lambda i, k: (0, 0)),
            pl.BlockSpec((SUBLANES, LANES), lambda i, k: (0, 0)),
            pl.BlockSpec((1, DK_A), lambda i, k: (0, 0)),
        ],
        out_specs=[
            pl.BlockSpec((1, n, GROUP_W), lambda i, k: (i, k, 0)),
            pl.BlockSpec((1, H_A, DK_A, DK_A), lambda i, k: (i, 0, 0, 0)),
        ],
        out_shape=[jax.ShapeDtypeStruct((b, t, GROUP_W), F32),
                   jax.ShapeDtypeStruct((b, H_A, DK_A, DK_A), F32)],
        scratch_shapes=[pltpu.VMEM((n + SUBLANES, 3 * GROUP_W), F32),
                        pltpu.VMEM((H_A, DK_A, DK_A), F32)],
        compiler_params=_cparams("parallel", "arbitrary"),
        name="delta",
    )(proj, proj, proj, carry, s0, cw, hp, gain)


def _rwkv_kernel(rkv_ref, wag_ref, crkv_ref, cwag_ref, s0_ref, mur_ref, muw_ref, wlr_ref, prm_ref,
                 y_ref, s_ref, ext1_scr, ext2_scr, sp_scr):
    t = pl.program_id(1)
    n = rkv_ref.shape[1]
    npair = H_B // 2
    pw = 2 * HD_B

    def shift_in(j):
        lane = _iota2((HD_B, pw), 1)
        row = _iota2((HD_B, pw), 0)
        return (lane == row + j * HD_B).astype(F32)

    @pl.when(t == 0)
    def _():
        ext1_scr[0:SUBLANES, :] = crkv_ref[0]
        ext2_scr[0:SUBLANES, :] = cwag_ref[0]
        for p in range(npair):
            halves = [_mm_f32(s0_ref[0, 2 * p + j], shift_in(j)) for j in range(2)]
            sp_scr[p] = jnp.concatenate(halves, axis=0)

    @pl.when(t > 0)
    def _():
        ext1_scr[0:SUBLANES, :] = ext1_scr[n:n + SUBLANES, :]
        ext2_scr[0:SUBLANES, :] = ext2_scr[n:n + SUBLANES, :]

    cur1 = rkv_ref[0]
    cur2 = wag_ref[0]
    ext1_scr[SUBLANES:SUBLANES + n, :] = cur1
    ext2_scr[SUBLANES:SUBLANES + n, :] = cur2
    xm1 = cur1 + (ext1_scr[SUBLANES - 1:SUBLANES - 1 + n, :] - cur1) * mur_ref[...]
    xm2 = cur2 + (ext2_scr[SUBLANES - 1:SUBLANES - 1 + n, :] - cur2) * muw_ref[...]
    r = xm1[:, 0:GROUP_W]
    kb = xm1[:, GROUP_W:2 * GROUP_W]
    vb = xm1[:, 2 * GROUP_W:3 * GROUP_W]
    lane2 = _iota2(xm2.shape, 1)
    feat = jnp.where(lane2 < 64, jnp.tanh(xm2), jnp.where(lane2 < 128, xm2, _sigmoid(xm2)))
    lr = _mm(feat, wlr_ref[...])
    w_raw = prm_ref[0:1, :] + lr[:, 0:GROUP_W]
    logw = -jnp.exp(-_softplus(-w_raw) - 0.5)
    a = _sigmoid(prm_ref[1:2, :] + lr[:, GROUP_W:2 * GROUP_W])
    gb = lr[:, 2 * GROUP_W:3 * GROUP_W]
    kkraw = kb * prm_ref[2:3, :]
    k = kb * (1.0 + (a - 1.0) * prm_ref[3:4, :])

    incl, strict = _tril_masks(n)
    cum = _mm_f32(incl.astype(F32), logw)
    g_inv = jnp.exp(-cum)
    lane_p = _iota2((1, pw), 1)
    blk = ((_iota2((pw, pw), 0) & -HD_B) == (_iota2((pw, pw), 1) & -HD_B))
    blkf = blk.astype(F32)

    for p in range(npair):
        sl = slice(p * pw, (p + 1) * pw)
        kkr = kkraw[:, sl]
        kk = kkr * lax.rsqrt(_mm_f32(kkr * kkr, blkf) + 1e-6)
        ap = a[:, sl]
        rp, kp, vp = r[:, sl], k[:, sl], vb[:, sl]
        cump, ginv = cum[:, sl], g_inv[:, sl]
        rt = rp * jnp.exp(cump)
        at = -kk * jnp.exp(cump - logw[:, sl])
        bt = kk * ap * ginv
        kt = kp * ginv
        sp = sp_scr[p]
        ar = jnp.concatenate([at, rt], axis=0)
        uy0 = _mm_nt(ar, sp)
        xs, rbs, rks = [], [], []
        for j in range(2):
            mh = jnp.logical_and(lane_p >= j * HD_B, lane_p < (j + 1) * HD_B)
            arh = jnp.where(mh, ar, 0.0)
            mb = _mm_nt(arh, bt)
            mk = _mm_nt(arh, kt)
            a_ab = jnp.where(strict, mb[:n], 0.0)
            a_ak = jnp.where(strict, mk[:n], 0.0)
            tinv = _unit_lower_inv(-a_ab)
            xs.append(_mm(tinv, uy0[:n] + _mm(a_ak, vp)))
            rbs.append(jnp.where(incl, mb[n:], 0.0))
            rks.append(jnp.where(incl, mk[n:], 0.0))
        first = lane_p < HD_B
        u = jnp.where(first, xs[0], xs[1])
        zs = [_mm(rbs[j], u) + _mm(rks[j], vp) for j in range(2)]
        y = uy0[n:] + jnp.where(first, zs[0], zs[1])
        upd = _mm_tn(jnp.concatenate([u, vp], axis=0), jnp.concatenate([bt, kt], axis=0))
        glast = jnp.exp(cump[n - 1:n, :])
        sp_scr[p] = jnp.where(blk, sp + upd, 0.0) * glast
        mean = _mm_f32(y, blkf) * (1.0 / HD_B)
        yc = y - mean
        var = _mm_f32(yc * yc, blkf) * (1.0 / HD_B)
        yn = yc * lax.rsqrt(var + GN_EPS) * prm_ref[5:6, sl] + prm_ref[6:7, sl]
        bonus = _mm_f32(rp * kp * prm_ref[4:5, sl], blkf) * vp
        y_ref[0, :, sl] = (yn + bonus) * gb[:, sl]

    @pl.when(t == pl.num_programs(1) - 1)
    def _():
        for p in range(npair):
            sp = sp_scr[p]
            for j in range(2):
                s_ref[0, 2 * p + j] = _mm_nt_f32(sp[j * HD_B:(j + 1) * HD_B, :], shift_in(j))


def _rwkv_call(proj, carry, s0, mur, muw, wlr, prm, n):
    b, t, _ = proj.shape
    wag_w = 2 * LANES
    return pl.pallas_call(
        _rwkv_kernel,
        grid=(b, t // n),
        in_specs=[
            pl.BlockSpec((1, n, 3 * GROUP_W), lambda i, k: (i, k, EV_RKV // (3 * GROUP_W))),
            pl.BlockSpec((1, n, wag_w), lambda i, k: (i, k, EV_WAG // wag_w)),
            pl.BlockSpec((1, SUBLANES, 3 * GROUP_W), lambda i, k: (i, 0, EV_RKV // (3 * GROUP_W))),
            pl.BlockSpec((1, SUBLANES, wag_w), lambda i, k: (i, 0, EV_WAG // wag_w)),
            pl.BlockSpec((1, H_B, HD_B, HD_B), lambda i, k: (i, 0, 0, 0)),
            pl.BlockSpec((1, 3 * GROUP_W), lambda i, k: (0, 0)),
            pl.BlockSpec((1, wag_w), lambda i, k: (0, 0)),
            pl.BlockSpec((wag_w, 3 * GROUP_W), lambda i, k: (0, 0)),
            pl.BlockSpec((SUBLANES, GROUP_W), lambda i, k: (0, 0)),
        ],
        out_specs=[
            pl.BlockSpec((1, n, GROUP_W), lambda i, k: (i, k, 0)),
            pl.BlockSpec((1, H_B, HD_B, HD_B), lambda i, k: (i, 0, 0, 0)),
        ],
        out_shape=[jax.ShapeDtypeStruct((b, t, GROUP_W), F32),
                   jax.ShapeDtypeStruct((b, H_B, HD_B, HD_B), F32)],
        scratch_shapes=[pltpu.VMEM((n + SUBLANES, 3 * GROUP_W), F32),
                        pltpu.VMEM((n + SUBLANES, wag_w), F32),
                        pltpu.VMEM((H_B // 2, 2 * HD_B, 2 * HD_B), F32)],
        compiler_params=_cparams("parallel", "arbitrary"),
        name="rwkv7",
    )(proj, proj, carry, carry, s0, mur, muw, wlr, prm)


def _gla_kernel(qk_ref, v_ref, z_ref, gkin_ref, s0_ref, wgk_ref, bgk_ref, gain_ref,
                o_ref, s_ref, s_scr):
    t = pl.program_id(1)
    n = qk_ref.shape[1]
    sub = min(GLA_SUB, n)
    npair = H_C // 2
    pw = 2 * DK_C
    qkw = H_C * DK_C

    @pl.when(t == 0)
    def _():
        for p in range(npair):
            s_scr[p] = s0_ref[0, 2 * p:2 * p + 2].reshape(pw, DV_C)

    qk = qk_ref[0]
    q = qk[:, :qkw] * (DK_C ** -0.5)
    k = qk[:, qkw:]
    v = v_ref[0]
    z = z_ref[0]
    gk = _log_sigmoid(_mm(gkin_ref[0], wgk_ref[...]) + bgk_ref[...]) * (1.0 / GLA_NORM)
    incl, _ = _tril_masks(sub)
    inclf = incl.astype(F32)
    lane_p = _iota2((1, pw), 1)
    states = [s_scr[p] for p in range(npair)]
    eye_p = _eye(pw)

    for c in range(n // sub):
        rows = slice(c * sub, (c + 1) * sub)
        bc = _mm_f32(inclf, gk[rows])
        blast = bc[sub - 1:sub, :]
        qd = q[rows] * jnp.exp(bc)
        kn = k[rows] * jnp.exp(-bc)
        kd = k[rows] * jnp.exp(blast - bc)
        gl = jnp.exp(blast)
        for p in range(npair):
            sl = slice(p * pw, (p + 1) * pw)
            sp = states[p]
            glcol = _mm_nt_f32(eye_p, jnp.broadcast_to(gl[:, sl], (SUBLANES, pw)))[:, 0:1]
            upd = None
            for j in range(2):
                h = 2 * p + j
                mh = jnp.logical_and(lane_p >= j * DK_C, lane_p < (j + 1) * DK_C)
                qdh = jnp.where(mh, qd[:, sl], 0.0)
                kdh = jnp.where(mh, kd[:, sl], 0.0)
                vh = v[rows, h * DV_C:(h + 1) * DV_C]
                att = jnp.where(incl, _mm_nt(qdh, kn[:, sl]), 0.0)
                o = _mm(att, vh) + _mm(qdh, sp)
                du = _mm_tn(kdh, vh)
                upd = du if upd is None else upd + du
                zh = z[rows, h * DV_C:(h + 1) * DV_C]
                o_ref[0, rows, h * DV_C:(h + 1) * DV_C] = _rms(o) * gain_ref[...] * _silu(zh)
            states[p] = sp * glcol + upd

    for p in range(npair):
        s_scr[p] = states[p]

    @pl.when(t == pl.num_programs(1) - 1)
    def _():
        for p in range(npair):
            s_ref[0, 2 * p:2 * p + 2] = states[p].reshape(2, DK_C, DV_C)


def _gla_call(proj, s0, wgk, bgk, gain, n):
    b, t, _ = proj.shape
    return pl.pallas_call(
        _gla_kernel,
        grid=(b, t // n),
        in_specs=[
            pl.BlockSpec((1, n, GROUP_W), lambda i, k: (i, k, OD_CQK // GROUP_W)),
            pl.BlockSpec((1, n, GROUP_W), lambda i, k: (i, k, OD_CV // GROUP_W)),
            pl.BlockSpec((1, n, GROUP_W), lambda i, k: (i, k, OD_CZ // GROUP_W)),
            pl.BlockSpec((1, n, LANES), lambda i, k: (i, k, OD_CGK // LANES)),
            pl.BlockSpec((1, H_C, DK_C, DV_C), lambda i, k: (i, 0, 0, 0)),
            pl.BlockSpec((LANES, H_C * DK_C), lambda i, k: (0, 0)),
            pl.BlockSpec((1, H_C * DK_C), lambda i, k: (0, 0)),
            pl.BlockSpec((1, DV_C), lambda i, k: (0, 0)),
        ],
        out_specs=[
            pl.BlockSpec((1, n, GROUP_W), lambda i, k: (i, k, 0)),
            pl.BlockSpec((1, H_C, DK_C, DV_C), lambda i, k: (i, 0, 0, 0)),
        ],
        out_shape=[jax.ShapeDtypeStruct((b, t, GROUP_W), F32),
                   jax.ShapeDtypeStruct((b, H_C, DK_C, DV_C), F32)],
        scratch_shapes=[pltpu.VMEM((H_C // 2, 2 * DK_C, DV_C), F32)],
        compiler_params=_cparams("parallel", "arbitrary"),
        name="gla",
    )(proj, proj, proj, proj, s0, wgk, bgk, gain)


def _mlstm_kernel(qk_ref, v_ref, og_ref, if_ref, c0_ref, n0_ref, m0_ref, bif_ref, gain_ref,
                  h_ref, c_ref, nn_ref, m_ref, c_scr, n_scr, m_scr):
    t = pl.program_id(1)
    n = qk_ref.shape[1]
    npair = H_D // 2
    pw = 2 * DK_D
    qkw = H_D * DK_D

    @pl.when(t == 0)
    def _():
        for p in range(npair):
            c_scr[p] = c0_ref[0, 2 * p:2 * p + 2].reshape(pw, DV_D)
        n_scr[...] = n0_ref[0]
        m_scr[...] = m0_ref[0]

    x = if_ref[0] + bif_ref[...]
    incl, _ = _tril_masks(n)
    fcum = _mm_f32(incl.astype(F32), _log_sigmoid(x))
    fct = _transpose_rows(fcum)
    xt = _transpose_rows(x)
    qk = qk_ref[0]
    q = qk[:, :qkw]
    k = qk[:, qkw:] * (DK_D ** -0.5)
    v = v_ref[0]
    og = og_ref[0]
    lane_p = _iota2((1, pw), 1)
    lane_m = _iota2((1, LANES), 1)
    row_p = _iota2((pw, 1), 0)
    m_row = m_scr[...]
    m_new_row = m_row

    for p in range(npair):
        sl = slice(p * pw, (p + 1) * pw)
        cp = c_scr[p]
        nrow = n_scr[:, sl]
        updc, updn, cds = None, None, []
        for j in range(2):
            h = 2 * p + j
            mh = jnp.logical_and(lane_p >= j * DK_D, lane_p < (j + 1) * DK_D)
            fcol = fcum[:, H_D + h:H_D + h + 1]
            frow = fct[H_D + h:H_D + h + 1, :]
            icol = x[:, h:h + 1]
            irow = xt[h:h + 1, :]
            flast = fcum[n - 1:n, H_D + h:H_D + h + 1]
            mprev = m_row[:, h:h + 1]
            log_d = jnp.where(incl, fcol - frow + irow, -jnp.inf)
            m_in = jnp.max(log_d, axis=-1, keepdims=True)
            log_e = flast - fcol + icol
            m_e = jnp.max(log_e, axis=0, keepdims=True)
            m_t = jnp.maximum(fcol + mprev, m_in)
            w_in = jnp.exp(fcol + mprev - m_t)
            qh = jnp.where(mh, q[:, sl], 0.0)
            kh = jnp.where(mh, k[:, sl], 0.0)
            vh = v[:, h * DV_D:(h + 1) * DV_D]
            dm = jnp.exp(log_d - m_t) * _mm_nt(qh, kh)
            num = w_in * _mm(qh, cp) + _mm(dm, vh)
            den = (w_in * jnp.sum(qh * nrow, axis=-1, keepdims=True)
                   + jnp.sum(dm, axis=-1, keepdims=True))
            hout = num / jnp.maximum(jnp.abs(den), jnp.exp(-m_t))
            m_new = jnp.maximum(flast + mprev, m_e)
            cds.append(jnp.exp(flast + mprev - m_new))
            ke = kh * jnp.exp(log_e - m_new)
            dc = _mm_tn(ke, vh)
            dn = jnp.sum(ke, axis=0, keepdims=True)
            updc = dc if updc is None else updc + dc
            updn = dn if updn is None else updn + dn
            m_new_row = jnp.where(lane_m == h, m_new, m_new_row)
            ogh = og[:, h * DV_D:(h + 1) * DV_D]
            h_ref[0, :, h * DV_D:(h + 1) * DV_D] = _sigmoid(ogh) * (_rms(hout) * gain_ref[...])
        c_scr[p] = cp * jnp.where(row_p < DK_D, cds[0], cds[1]) + updc
        n_scr[:, sl] = nrow * jnp.where(lane_p < DK_D, cds[0], cds[1]) + updn
    m_scr[...] = m_new_row

    @pl.when(t == pl.num_programs(1) - 1)
    def _():
        for p in range(npair):
            c_ref[0, 2 * p:2 * p + 2] = c_scr[p].reshape(2, DK_D, DV_D)
        nn_ref[0] = n_scr[...]
        m_ref[0] = m_scr[...]


def _mlstm_call(proj, c0, n0, m0, bif, gain, n):
    b, t, _ = proj.shape
    qkw = H_D * DK_D
    return pl.pallas_call(
        _mlstm_kernel,
        grid=(b, t // n),
        in_specs=[
            pl.BlockSpec((1, n, GROUP_W), lambda i, k: (i, k, OD_DQK // GROUP_W)),
            pl.BlockSpec((1, n, GROUP_W), lambda i, k: (i, k, OD_DV // GROUP_W)),
            pl.BlockSpec((1, n, GROUP_W), lambda i, k: (i, k, OD_DO // GROUP_W)),
            pl.BlockSpec((1, n, LANES), lambda i, k: (i, k, OD_DIF // LANES)),
            pl.BlockSpec((1, H_D, DK_D, DV_D), lambda i, k: (i, 0, 0, 0)),
            pl.BlockSpec((1, 1, qkw), lambda i, k: (i, 0, 0)),
            pl.BlockSpec((1, 1, LANES), lambda i, k: (i, 0, 0)),
            pl.BlockSpec((1, LANES), lambda i, k: (0, 0)),
            pl.BlockSpec((1, DV_D), lambda i, k: (0, 0)),
        ],
        out_specs=[
            pl.BlockSpec((1, n, GROUP_W), lambda i, k: (i, k, 0)),
            pl.BlockSpec((1, H_D, DK_D, DV_D), lambda i, k: (i, 0, 0, 0)),
            pl.BlockSpec((1, 1, qkw), lambda i, k: (i, 0, 0)),
            pl.BlockSpec((1, 1, LANES), lambda i, k: (i, 0, 0)),
        ],
        out_shape=[jax.ShapeDtypeStruct((b, t, GROUP_W), F32),
                   jax.ShapeDtypeStruct((b, H_D, DK_D, DV_D), F32),
                   jax.ShapeDtypeStruct((b, 1, qkw), F32),
                   jax.ShapeDtypeStruct((b, 1, LANES), F32)],
        scratch_shapes=[pltpu.VMEM((H_D // 2, 2 * DK_D, DV_D), F32),
                        pltpu.VMEM((1, qkw), F32),
                        pltpu.VMEM((1, LANES), F32)],
        compiler_params=_cparams("parallel", "arbitrary"),
        name="mlstm",
    )(proj, proj, proj, proj, c0, n0, m0, bif, gain)


def _pad_cols(w, width):
    return jnp.pad(w, ((0, 0), (0, width - w.shape[1])))


def _pad_rows(w, rows, at=0):
    return jnp.pad(w, ((at, rows - at - w.shape[0]), (0, 0)))


def _even_in_weight(w):
    pa, pb = w[:, :2056], w[:, 2056:]
    cols = [
        pb[:, 0:1536],
        pa[:, 0:1536],
        pa[:, 1544:2056],
        pb[:, 1536:1792],
        _pad_cols(pa[:, 1536:1544], LANES),
    ]
    return _pad_cols(jnp.concatenate(cols, axis=1), EV_COLS).astype(BF16)


def _odd_in_weight(w):
    pc, pd = w[:, :1552], w[:, 1552:]
    cols = [
        pc[:, 0:512],
        pd[:, 0:512],
        pc[:, 512:1024],
        pc[:, 1040:1552],
        pd[:, 512:1024],
        pd[:, 1032:1544],
        _pad_cols(pc[:, 1024:1040], LANES),
        _pad_cols(pd[:, 1024:1032], LANES),
    ]
    return jnp.concatenate(cols, axis=1).astype(BF16)


def _row(v):
    return v.reshape(1, -1).astype(F32)


def _tiles(x):
    b, t, _ = x.shape
    tb = min(t, 1024)
    bb = min(b, 1024 // tb)
    return bb, tb


def _trunk(x, mod, states, wts):
    xbuf, s_delta, s_rwkv, s_gla, s_mc, s_mn, s_mm = states
    b, t, _ = x.shape
    bb, tb = _tiles(x)
    n = min(CHUNK, t)
    new_even = ([], [], [])
    new_odd = ([], [], [], [])
    for l in range(DEPTH):
        lw = wts["layers"][l]
        m_l = mod[l]
        x = _ffn_call(x, m_l, lw["gain0"], lw["wg0"], lw["wu0"], lw["wd0"], 0, bb, tb, 256)
        i = l // 2
        if l % 2 == 0:
            h, proj = _adaln_proj_call(x, m_l, lw["gain1"], lw["w_in"], bb, tb, 1024)
            if xbuf is None:
                carry = jnp.zeros((b, SUBLANES, EV_COLS), F32)
            else:
                rows = _rows_proj_call(xbuf[i].reshape(b * (CONV_W - 1), D_MODEL), lw["w_in"], 1024)
                carry = jnp.pad(rows.reshape(b, CONV_W - 1, EV_COLS),
                                ((0, 0), (SUBLANES - CONV_W + 1, 0), (0, 0)))
            oa, sd = _delta_call(proj, carry, s_delta[i], lw["conv_w"], lw["delta_hp"], lw["gain_a"], n)
            ob, sr = _rwkv_call(proj, carry, s_rwkv[i], lw["mu_rkv"], lw["mu_wag"], lw["w_lora"],
                                lw["rwkv_prm"], n)
            for lst, val in zip(new_even, (h[:, t - (CONV_W - 1):], sd, sr)):
                lst.append(val)
        else:
            _, proj = _adaln_proj_call(x, m_l, lw["gain1"], lw["w_in"], bb, tb, 1664)
            oa, sg = _gla_call(proj, s_gla[i], lw["w_gk2"], lw["b_gk"], lw["gain_c"], n)
            ob, sc, sn, sm = _mlstm_call(proj, s_mc[i], s_mn[i].reshape(b, 1, H_D * DK_D),
                                         _pad_cols(s_mm[i], LANES).reshape(b, 1, LANES),
                                         lw["b_if"], lw["gain_d"], n)
            for lst, val in zip(new_odd, (sg, sc, sn.reshape(b, H_D, DK_D), sm[:, 0, :H_D])):
                lst.append(val)
        x = _outproj_call(x, oa, ob, m_l, lw["w_out"], bb, tb)
        x = _ffn_call(x, m_l, lw["gain2"], lw["wg1"], lw["wu1"], lw["wd1"], 6, bb, tb, 256)
    y = _final_norm_call(x, wts["final_gain"], bb, tb)
    stacked = [jnp.stack(lst) for lst in new_even + new_odd]
    return (y, *stacked)


def _prepare_weights(norm_gain, final_gain, w_ffn_gate, w_ffn_up, w_ffn_down, w_in_even, w_out_even,
                     conv_w, a_log, dt_bias, gain_a, mu_b, w0_b, w_w2, a0_b, w_a2, w_g2, k_k, k_a, r_k,
                     lnx_gain, lnx_bias, w_in_odd, w_out_odd, w_gk2, b_gk, gain_c, b_i, b_f, gain_d):
    layers = []
    for l in range(DEPTH):
        i = l // 2
        lw = {
            "gain0": _row(norm_gain[l, 0]), "gain1": _row(norm_gain[l, 1]), "gain2": _row(norm_gain[l, 2]),
            "wg0": w_ffn_gate[l, 0].astype(BF16), "wu0": w_ffn_up[l, 0].astype(BF16),
            "wd0": w_ffn_down[l, 0].astype(BF16),
            "wg1": w_ffn_gate[l, 1].astype(BF16), "wu1": w_ffn_up[l, 1].astype(BF16),
            "wd1": w_ffn_down[l, 1].astype(BF16),
        }
        if l % 2 == 0:
            lw["w_in"] = _even_in_weight(w_in_even[i])
            lw["w_out"] = w_out_even[i].astype(BF16)
            lw["conv_w"] = _pad_rows(conv_w[i].astype(F32), SUBLANES)
            lw["delta_hp"] = _pad_rows(jnp.stack([_pad_cols(_row(a_log[i]), LANES)[0],
                                                  _pad_cols(_row(dt_bias[i]), LANES)[0]]), SUBLANES)
            lw["gain_a"] = _row(gain_a[i])
            lw["mu_rkv"] = _row(mu_b[i, :1536])
            lw["mu_wag"] = _row(mu_b[i, 1536:])
            lora = jnp.zeros((2 * LANES, 3 * GROUP_W), F32)
            lora = lora.at[0:64, 0:GROUP_W].set(w_w2[i])
            lora = lora.at[64:128, GROUP_W:2 * GROUP_W].set(w_a2[i])
            lora = lora.at[128:256, 2 * GROUP_W:].set(w_g2[i])
            lw["w_lora"] = lora.astype(BF16)
            lw["rwkv_prm"] = jnp.stack([w0_b[i], a0_b[i], k_k[i], k_a[i], r_k[i].reshape(-1),
                                        lnx_gain[i], lnx_bias[i], jnp.zeros_like(w0_b[i])]).astype(F32)
        else:
            lw["w_in"] = _odd_in_weight(w_in_odd[i])
            lw["w_out"] = w_out_odd[i].astype(BF16)
            lw["w_gk2"] = _pad_rows(w_gk2[i], LANES).astype(BF16)
            lw["b_gk"] = _row(b_gk[i])
            lw["gain_c"] = _row(gain_c[i])
            lw["b_if"] = _pad_cols(_row(jnp.concatenate([b_i[i], b_f[i]])), LANES)
            lw["gain_d"] = _row(gain_d[i])
        layers.append(lw)
    return {"layers": layers, "final_gain": _row(final_gain)}


def kernel(x_prompt, x_sample, c_prompt, c_sample, state_xbuf_even, state_delta, state_rwkv, state_gla,
           state_mlstm_c, state_mlstm_n, state_mlstm_m, w_mod, b_mod, norm_gain, final_gain, w_ffn_gate,
           w_ffn_up, w_ffn_down, w_in_even, w_out_even, conv_w, a_log, dt_bias, gain_a, mu_b, w0_b, w_w2,
           a0_b, w_a2, w_g2, k_k, k_a, r_k, lnx_gain, lnx_bias, w_in_odd, w_out_odd, w_gk2, b_gk, gain_c,
           b_i, b_f, gain_d):
    wts = _prepare_weights(norm_gain, final_gain, w_ffn_gate, w_ffn_up, w_ffn_down, w_in_even, w_out_even,
                           conv_w, a_log, dt_bias, gain_a, mu_b, w0_b, w_w2, a0_b, w_a2, w_g2, k_k, k_a,
                           r_k, lnx_gain, lnx_bias, w_in_odd, w_out_odd, w_gk2, b_gk, gain_c, b_i, b_f,
                           gain_d)
    bp, bs = x_prompt.shape[0], x_sample.shape[0]
    c_all = jnp.concatenate([c_prompt, c_sample], axis=0).astype(F32)
    mod = _mod_call(c_all, w_mod, b_mod).reshape(DEPTH, bp + bs, N_MOD, 1, D_MODEL)
    mod_p, mod_s = mod[:, :bp], mod[:, bp:]

    def zeros(shape):
        return jnp.zeros(shape, F32)

    zero_states = (
        None,
        zeros((N_EVEN, bp, H_A, DK_A, DK_A)),
        zeros((N_EVEN, bp, H_B, HD_B, HD_B)),
        zeros((N_ODD, bp, H_C, DK_C, DV_C)),
        zeros((N_ODD, bp, H_D, DK_D, DV_D)),
        zeros((N_ODD, bp, H_D, DK_D)),
        zeros((N_ODD, bp, H_D)),
    )
    y_p, xb_p, dl_p, rw_p, gl_p, mc_p, mn_p, mm_p = _trunk(x_prompt, mod_p, zero_states, wts)
    sample_states = (state_xbuf_even, state_delta, state_rwkv, state_gla,
                     state_mlstm_c, state_mlstm_n, state_mlstm_m)
    y_s, xb_s, dl_s, rw_s, gl_s, mc_s, mn_s, mm_s = _trunk(x_sample, mod_s, sample_states, wts)
    return (y_p, y_s, xb_p, xb_s, dl_p, dl_s, rw_p, rw_s, gl_p, gl_s, mc_p, mc_s, mn_p, mn_s, mm_p, mm_s)
```

```python
import functools

import jax
import jax.numpy as jnp
from jax import lax
from jax.experimental import pallas as pl
from jax.experimental.pallas import tpu as pltpu

F32 = jnp.float32
BF16 = jnp.bfloat16
HIGHEST = lax.Precision.HIGHEST

D_MODEL = 1024
DEPTH = 4
N_EVEN = 2
N_ODD = 2
D_FF = 2816
N_MOD = 9
EPS = 1e-6
GN_EPS = 64e-5
CONV_W = 4
H_A, DK_A = 4, 128
H_B, HD_B = 8, 64
H_C, DK_C, DV_C = 4, 64, 128
H_D, DK_D, DV_D = 4, 64, 128
GLA_NORM = 16.0
GROUP_W = 512

LANES = 128
SUBLANES = 8
BF16_ROWS = 16
VMEM_LIMIT_BYTES = 48 * 1024 * 1024

EV_RKV, EV_QKV, EV_Z, EV_WAG, EV_AB, EV_COLS = 0, 1536, 3072, 3584, 3840, 4096
OD_CQK, OD_DQK, OD_CV, OD_CZ, OD_DV, OD_DO, OD_CGK, OD_DIF, OD_COLS = (
    0, 512, 1024, 1536, 2048, 2560, 3072, 3200, 3328)

CHUNK = 64
GLA_SUB = 16
PROBLEMS = 2


def _bf(x):
    if x.dtype == BF16:
        return x
    if x.shape[-2] % BF16_ROWS == 0 and x.shape[-1] % BF16_ROWS == 0:
        return x.astype(BF16)
    return x


def _pair(a, b):
    a, b = _bf(a), _bf(b)
    if a.dtype != b.dtype:
        a, b = a.astype(F32), b.astype(F32)
    return a, b


def _mm(a, b):
    a, b = _pair(a, b)
    return jnp.dot(a, b, preferred_element_type=F32)


def _mm_nt(a, b):
    a, b = _pair(a, b)
    return lax.dot_general(a, b, (((1,), (1,)), ((), ())), preferred_element_type=F32)


def _mm_tn(a, b):
    a, b = _pair(a, b)
    return lax.dot_general(a, b, (((0,), (0,)), ((), ())), preferred_element_type=F32)


def _mm_f32(a, b):
    return jnp.dot(a, b, precision=HIGHEST, preferred_element_type=F32)


def _mm_nt_f32(a, b):
    return lax.dot_general(a, b, (((1,), (1,)), ((), ())), precision=HIGHEST,
                           preferred_element_type=F32)


def _split_bf16(x, parts):
    out, r = [], x
    for i in range(parts):
        p = r.astype(BF16)
        out.append(p)
        if i + 1 < parts:
            r = r - p.astype(F32)
    return out


def _mm01(a01, x, parts):
    a = a01.astype(BF16)
    acc = None
    for p in _split_bf16(x, parts):
        d = jnp.dot(a, p, preferred_element_type=F32)
        acc = d if acc is None else acc + d
    return acc


def _mm_x01(x, b01, parts):
    b = b01.astype(BF16)
    acc = None
    for p in _split_bf16(x, parts):
        d = jnp.dot(p, b, preferred_element_type=F32)
        acc = d if acc is None else acc + d
    return acc


def _mm_nt01(a01, x, parts):
    a = a01.astype(BF16)
    acc = None
    for p in _split_bf16(x, parts):
        d = lax.dot_general(a, p, (((1,), (1,)), ((), ())), preferred_element_type=F32)
        acc = d if acc is None else acc + d
    return acc


def _sigmoid(x):
    return jax.nn.sigmoid(x)


def _silu(x):
    return x * jax.nn.sigmoid(x)


def _softplus(x):
    return jnp.maximum(x, 0.0) + jnp.log1p(jnp.exp(-jnp.abs(x)))


def _log_sigmoid(x):
    return -_softplus(-x)


def _rms(x, eps=EPS):
    return x * lax.rsqrt(jnp.mean(x * x, axis=-1, keepdims=True) + eps)


def _l2n(x):
    return x * lax.rsqrt(jnp.sum(x * x, axis=-1, keepdims=True) + 1e-6)


def _iota2(shape, dim):
    return lax.broadcasted_iota(jnp.int32, shape, dim)


def _tril_masks(n):
    r, c = _iota2((n, n), 0), _iota2((n, n), 1)
    return r >= c, r > c


def _group_masks(size, n):
    r, c = _iota2((size, size), 0), _iota2((size, size), 1)
    same = (r & -n) == (c & -n)
    return jnp.logical_and(same, r >= c), jnp.logical_and(same, r > c)


def _eye(n):
    return (_iota2((n, n), 0) == _iota2((n, n), 1)).astype(F32)


def _transpose_rows(x):
    return _mm_nt_f32(_eye(x.shape[1]), x)


def _unit_lower_inv_many(mats, n):
    eye = _eye(mats[0].shape[0])
    ms = [-a for a in mats]
    ps = [eye + m for m in ms]
    covered = 2
    while covered < n:
        ms = [_mm(m, m) for m in ms]
        ps = [p + _mm(p, m) for p, m in zip(ps, ms)]
        covered *= 2
    return ps


def _cparams(*sem):
    return pltpu.CompilerParams(dimension_semantics=sem, vmem_limit_bytes=VMEM_LIMIT_BYTES)


def _mod_kernel(c_ref, w_ref, b_ref, o_ref):
    cs = _silu(c_ref[...])
    o_ref[...] = _mm(cs, w_ref[...]) + b_ref[...]


def _mod_call(c_all, w_mod, b_mod):
    rows = c_all.shape[0]
    tn = 1024
    width = N_MOD * D_MODEL
    return pl.pallas_call(
        _mod_kernel,
        grid=(DEPTH, width // tn),
        in_specs=[
            pl.BlockSpec((rows, D_MODEL), lambda l, j: (0, 0)),
            pl.BlockSpec((None, D_MODEL, tn), lambda l, j: (l, 0, j)),
            pl.BlockSpec((None, 1, tn), lambda l, j: (l, 0, j)),
        ],
        out_specs=pl.BlockSpec((None, rows, tn), lambda l, j: (l, 0, j)),
        out_shape=jax.ShapeDtypeStruct((DEPTH, rows, width), F32),
        compiler_params=_cparams("parallel", "parallel"),
        name="mod",
    )(c_all, w_mod, b_mod.reshape(DEPTH, 1, width))


def _adaln(x, gain, scale, shift):
    return _rms(x) * gain * (1.0 + scale) + shift


def _ffn_kernel(x_ref, sh_ref, sc_ref, gt_ref, gain_ref, wg_ref, wu_ref, wd_ref, o_ref,
                h_scr, acc_scr):
    f = pl.program_id(2)
    bb, tb, d = x_ref.shape

    @pl.when(f == 0)
    def _():
        h = _adaln(x_ref[...], gain_ref[...], sc_ref[...], sh_ref[...])
        h_scr[...] = h.reshape(bb * tb, d).astype(BF16)
        acc_scr[...] = jnp.zeros_like(acc_scr)

    h = h_scr[...]
    g = jnp.dot(h, wg_ref[...], preferred_element_type=F32)
    u = jnp.dot(h, wu_ref[...], preferred_element_type=F32)
    a = (_silu(g) * u).astype(BF16)
    acc_scr[...] += jnp.dot(a, wd_ref[...], preferred_element_type=F32)

    @pl.when(f == pl.num_programs(2) - 1)
    def _():
        y = acc_scr[...].reshape(bb, tb, d)
        o_ref[...] = x_ref[...] + 0.5 * (1.0 + gt_ref[...]) * y


def _mod_spec(bb, j, ngrid):
    if ngrid == 3:
        return pl.BlockSpec((bb, None, 1, D_MODEL), lambda b, t, f: (b, j, 0, 0))
    return pl.BlockSpec((bb, None, 1, D_MODEL), lambda b, t: (b, j, 0, 0))


def _ffn_call(x, mod, gain, wg, wu, wd, j0, bb, tb, tf):
    b, t, d = x.shape
    xspec = pl.BlockSpec((bb, tb, d), lambda i, k, f: (i, k, 0))
    return pl.pallas_call(
        _ffn_kernel,
        grid=(b // bb, t // tb, D_FF // tf),
        in_specs=[
            xspec,
            _mod_spec(bb, j0, 3), _mod_spec(bb, j0 + 1, 3), _mod_spec(bb, j0 + 2, 3),
            pl.BlockSpec((1, d), lambda i, k, f: (0, 0)),
            pl.BlockSpec((d, tf), lambda i, k, f: (0, f)),
            pl.BlockSpec((d, tf), lambda i, k, f: (0, f)),
            pl.BlockSpec((tf, d), lambda i, k, f: (f, 0)),
        ],
        out_specs=xspec,
        out_shape=jax.ShapeDtypeStruct(x.shape, x.dtype),
        scratch_shapes=[pltpu.VMEM((bb * tb, d), BF16), pltpu.VMEM((bb * tb, d), F32)],
        compiler_params=_cparams("parallel", "parallel", "arbitrary"),
        name="ffn",
    )(x, mod, mod, mod, gain, wg, wu, wd)


def _adaln_proj_kernel(x_ref, sh_ref, sc_ref, gain_ref, w_ref, h_ref, p_ref, hb_scr):
    j = pl.program_id(2)
    bb, tb, d = x_ref.shape

    @pl.when(j == 0)
    def _():
        h = _adaln(x_ref[...], gain_ref[...], sc_ref[...], sh_ref[...])
        h_ref[...] = h
        hb_scr[...] = h.reshape(bb * tb, d).astype(BF16)

    p = jnp.dot(hb_scr[...], w_ref[...], preferred_element_type=F32)
    p_ref[...] = p.reshape(bb, tb, p.shape[-1])


def _adaln_proj_call(x, mod, gain, w, bb, tb, tn):
    b, t, d = x.shape
    n = w.shape[1]
    xspec = pl.BlockSpec((bb, tb, d), lambda i, k, j: (i, k, 0))
    return pl.pallas_call(
        _adaln_proj_kernel,
        grid=(b // bb, t // tb, n // tn),
        in_specs=[
            xspec, _mod_spec(bb, 3, 3), _mod_spec(bb, 4, 3),
            pl.BlockSpec((1, d), lambda i, k, j: (0, 0)),
            pl.BlockSpec((d, tn), lambda i, k, j: (0, j)),
        ],
        out_specs=[xspec, pl.BlockSpec((bb, tb, tn), lambda i, k, j: (i, k, j))],
        out_shape=[jax.ShapeDtypeStruct(x.shape, F32), jax.ShapeDtypeStruct((b, t, n), F32)],
        scratch_shapes=[pltpu.VMEM((bb * tb, d), BF16)],
        compiler_params=_cparams("parallel", "parallel", "arbitrary"),
        name="adaln_proj",
    )(x, mod, mod, gain, w)


def _rows_proj_kernel(a_ref, w_ref, o_ref):
    o_ref[...] = _mm(a_ref[...], w_ref[...])


def _rows_proj_call(a, w, tn):
    m, k = a.shape
    n = w.shape[1]
    return pl.pallas_call(
        _rows_proj_kernel,
        grid=(n // tn,),
        in_specs=[pl.BlockSpec((m, k), lambda j: (0, 0)), pl.BlockSpec((k, tn), lambda j: (0, j))],
        out_specs=pl.BlockSpec((m, tn), lambda j: (0, j)),
        out_shape=jax.ShapeDtypeStruct((m, n), F32),
        compiler_params=_cparams("parallel"),
        name="rows_proj",
    )(a, w)


def _outproj_kernel(x_ref, oa_ref, ob_ref, gt_ref, w_ref, o_ref):
    bb, tb, d = x_ref.shape
    o = jnp.concatenate([oa_ref[...], ob_ref[...]], axis=-1).reshape(bb * tb, d)
    y = jnp.dot(o.astype(BF16), w_ref[...], preferred_element_type=F32).reshape(bb, tb, d)
    o_ref[...] = x_ref[...] + (1.0 + gt_ref[...]) * y


def _outproj_call(x, oa, ob, mod, w, bb, tb):
    b, t, d = x.shape
    xspec = pl.BlockSpec((bb, tb, d), lambda i, k: (i, k, 0))
    hspec = pl.BlockSpec((bb, tb, GROUP_W), lambda i, k: (i, k, 0))
    return pl.pallas_call(
        _outproj_kernel,
        grid=(b // bb, t // tb),
        in_specs=[xspec, hspec, hspec, _mod_spec(bb, 5, 2),
                  pl.BlockSpec((d, d), lambda i, k: (0, 0))],
        out_specs=xspec,
        out_shape=jax.ShapeDtypeStruct(x.shape, x.dtype),
        compiler_params=_cparams("parallel", "parallel"),
        name="outproj",
    )(x, oa, ob, mod, w)


def _final_norm_kernel(x_ref, g_ref, o_ref):
    o_ref[...] = _rms(x_ref[...]) * g_ref[...]


def _final_norm_call(x, gain, bb, tb):
    b, t, d = x.shape
    xspec = pl.BlockSpec((bb, tb, d), lambda i, k: (i, k, 0))
    return pl.pallas_call(
        _final_norm_kernel,
        grid=(b // bb, t // tb),
        in_specs=[xspec, pl.BlockSpec((1, d), lambda i, k: (0, 0))],
        out_specs=xspec,
        out_shape=jax.ShapeDtypeStruct(x.shape, x.dtype),
        compiler_params=_cparams("parallel", "parallel"),
        name="final_norm",
    )(x, gain)


def _stack_heads(x, nheads, head_w):
    lane = _iota2((1, x.shape[1]), 1)
    return jnp.concatenate(
        [jnp.where(jnp.logical_and(lane >= h * head_w, lane < (h + 1) * head_w), x, 0.0)
         for h in range(nheads)], axis=0)


def _delta_kernel(qkv_ref, z_ref, ab_ref, carry_ref, s0_ref, cw_ref, hp_ref, gain_ref,
                  o_ref, s_ref, ext_scr, s_scr, *, groups):
    t = pl.program_id(1)
    bb, n, _ = qkv_ref.shape
    rows = groups * n
    nprob = bb // groups
    hr = H_A * rows

    @pl.when(t == 0)
    def _():
        ext_scr[:, 0:SUBLANES, :] = carry_ref[...]
        s_scr[...] = s0_ref[...]

    @pl.when(t > 0)
    def _():
        ext_scr[:, 0:SUBLANES, :] = ext_scr[:, n:n + SUBLANES, :]

    ext_scr[:, SUBLANES:SUBLANES + n, :] = qkv_ref[...]
    conv = cw_ref[0:1, :] * ext_scr[:, 5:5 + n, :]
    for j in range(1, CONV_W):
        conv = conv + cw_ref[j:j + 1, :] * ext_scr[:, 5 + j:5 + j + n, :]
    x = _silu(conv).reshape(bb * n, 3 * GROUP_W)
    ab = ab_ref[...].reshape(bb * n, LANES)
    g = -jnp.exp(hp_ref[0:1, :]) * _softplus(ab + hp_ref[1:2, :])
    beta = _sigmoid(ab)
    z = z_ref[...].reshape(bb * n, GROUP_W)

    incl1, _ = _group_masks(rows, n)
    incl, strict = _group_masks(hr, n)
    lane = _iota2((1, LANES), 1)
    ones = jnp.ones((hr, LANES), F32)
    probs = range(nprob)

    def head_rows(a, h):
        return a[:, h * DK_A:(h + 1) * DK_A]

    gcol, bcol, qs, ks, vs, zs = [], [], [], [], [], []
    kst, lhs = [], []
    grow = []
    for p in probs:
        sl = slice(p * rows, (p + 1) * rows)
        gc = _mm01(incl1, g[sl], 3)
        gsel = jnp.concatenate([jnp.where(lane == h, gc, 0.0) for h in range(H_A)], axis=0)
        bsel = jnp.concatenate([jnp.where(lane == H_A + h, beta[sl], 0.0) for h in range(H_A)], axis=0)
        gcol.append(jnp.sum(gsel, axis=-1, keepdims=True))
        bcol.append(jnp.sum(bsel, axis=-1, keepdims=True))
        grow.append(_mm_nt01(ones, gsel, 3))
        xp = x[sl]
        qn = jnp.concatenate([_l2n(head_rows(xp[:, 0:GROUP_W], h)) * (DK_A ** -0.5)
                              for h in range(H_A)], axis=1)
        kn = jnp.concatenate([_l2n(head_rows(xp[:, GROUP_W:2 * GROUP_W], h)) for h in range(H_A)], axis=1)
        qs.append(jnp.concatenate([head_rows(qn, h) for h in range(H_A)], axis=0))
        ks.append(jnp.concatenate([head_rows(kn, h) for h in range(H_A)], axis=0))
        vs.append(jnp.concatenate([head_rows(xp[:, 2 * GROUP_W:], h) for h in range(H_A)], axis=0))
        zs.append(z[sl])
        k_st = _stack_heads(kn, H_A, DK_A)
        kst.append(k_st)
        lhs.append(jnp.concatenate([k_st * bcol[p], _stack_heads(qn, H_A, DK_A)], axis=0))

    kq = [_mm_nt(lhs[p], kst[p]) for p in probs]
    dec = [jnp.exp(jnp.where(incl, gcol[p] - grow[p], -jnp.inf)) for p in probs]
    a_low = [jnp.where(strict, kq[p][:hr] * dec[p], 0.0) for p in probs]
    tinv = _unit_lower_inv_many(a_low, n)
    eg = [jnp.exp(gcol[p]) for p in probs]
    kb = [ks[p] * bcol[p] for p in probs]
    sol = [_mm(tinv[p], jnp.concatenate([vs[p] * bcol[p], kb[p] * eg[p]], axis=1)) for p in probs]
    qg = [qs[p] * eg[p] for p in probs]

    us, oparts = [], []
    for p in probs:
        u_rows, o_rows = [], []
        for h in range(H_A):
            for gi in range(groups):
                r0 = h * rows + gi * n
                s = s_scr[p * groups + gi, h]
                ksq = _mm(jnp.concatenate([sol[p][r0:r0 + n, DK_A:], qg[p][r0:r0 + n]], axis=0), s)
                u_rows.append(sol[p][r0:r0 + n, :DK_A] - ksq[:n])
                o_rows.append(ksq[n:])
        us.append(jnp.concatenate(u_rows, axis=0))
        oparts.append(jnp.concatenate(o_rows, axis=0))
    outs = [oparts[p] + _mm(kq[p][hr:] * dec[p], us[p]) for p in probs]
    for p in probs:
        for h in range(H_A):
            for gi in range(groups):
                r0 = h * rows + gi * n
                glast = gcol[p][r0 + n - 1:r0 + n]
                kd = ks[p][r0:r0 + n] * jnp.exp(glast - gcol[p][r0:r0 + n])
                s = s_scr[p * groups + gi, h]
                s_scr[p * groups + gi, h] = s * jnp.exp(glast) + _mm_tn(kd, us[p][r0:r0 + n])
            o = outs[p][h * rows:(h + 1) * rows]
            val = _rms(o) * gain_ref[...] * _silu(head_rows(zs[p], h))
            o_ref[p * groups:(p + 1) * groups, :, h * DK_A:(h + 1) * DK_A] = val.reshape(groups, n, DK_A)

    @pl.when(t == pl.num_programs(1) - 1)
    def _():
        s_ref[...] = s_scr[...]


def _delta_call(proj, carry, s0, cw, hp, gain, bb, n, groups):
    b, t, _ = proj.shape
    return pl.pallas_call(
        functools.partial(_delta_kernel, groups=groups),
        grid=(b // bb, t // n),
        in_specs=[
            pl.BlockSpec((bb, n, 3 * GROUP_W), lambda i, k: (i, k, EV_QKV // (3 * GROUP_W))),
            pl.BlockSpec((bb, n, GROUP_W), lambda i, k: (i, k, EV_Z // GROUP_W)),
            pl.BlockSpec((bb, n, LANES), lambda i, k: (i, k, EV_AB // LANES)),
            pl.BlockSpec((bb, SUBLANES, 3 * GROUP_W), lambda i, k: (i, 0, EV_QKV // (3 * GROUP_W))),
            pl.BlockSpec((bb, H_A, DK_A, DK_A), lambda i, k: (i, 0, 0, 0)),
            pl.BlockSpec((SUBLANES, 3 * GROUP_W), lambda i, k: (0, 0)),
            pl.BlockSpec((SUBLANES, LANES), lambda i, k: (0, 0)),
            pl.BlockSpec((1, DK_A), lambda i, k: (0, 0)),
        ],
        out_specs=[
            pl.BlockSpec((bb, n, GROUP_W), lambda i, k: (i, k, 0)),
            pl.BlockSpec((bb, H_A, DK_A, DK_A), lambda i, k: (i, 0, 0, 0)),
        ],
        out_shape=[jax.ShapeDtypeStruct((b, t, GROUP_W), F32),
                   jax.ShapeDtypeStruct((b, H_A, DK_A, DK_A), F32)],
        scratch_shapes=[pltpu.VMEM((bb, n + SUBLANES, 3 * GROUP_W), F32),
                        pltpu.VMEM((bb, H_A, DK_A, DK_A), F32)],
        compiler_params=_cparams("parallel", "arbitrary"),
        name="delta",
    )(proj, proj, proj, carry, s0, cw, hp, gain)


def _rwkv_kernel(rkv_ref, wag_ref, crkv_ref, cwag_ref, s0_ref, mur_ref, muw_ref, wlr_ref, prm_ref,
                 y_ref, s_ref, ext1_scr, ext2_scr, sp_scr, *, groups):
    t = pl.program_id(1)
    bb, n, _ = rkv_ref.shape
    rows = groups * n
    nprob = bb // groups
    npair = H_B // 2
    pw = 2 * HD_B
    r2 = 2 * rows

    @pl.when(t == 0)
    def _():
        ext1_scr[:, 0:SUBLANES, :] = crkv_ref[...]
        ext2_scr[:, 0:SUBLANES, :] = cwag_ref[...]
        sp_scr[...] = s0_ref[...]

    @pl.when(t > 0)
    def _():
        ext1_scr[:, 0:SUBLANES, :] = ext1_scr[:, n:n + SUBLANES, :]
        ext2_scr[:, 0:SUBLANES, :] = ext2_scr[:, n:n + SUBLANES, :]

    cur1 = rkv_ref[...]
    cur2 = wag_ref[...]
    ext1_scr[:, SUBLANES:SUBLANES + n, :] = cur1
    ext2_scr[:, SUBLANES:SUBLANES + n, :] = cur2
    xm1 = cur1 + (ext1_scr[:, SUBLANES - 1:SUBLANES - 1 + n, :] - cur1) * mur_ref[...]
    xm2 = cur2 + (ext2_scr[:, SUBLANES - 1:SUBLANES - 1 + n, :] - cur2) * muw_ref[...]
    xm1 = xm1.reshape(bb * n, 3 * GROUP_W)
    xm2 = xm2.reshape(bb * n, 2 * LANES)
    r = xm1[:, 0:GROUP_W]
    kb = xm1[:, GROUP_W:2 * GROUP_W]
    vb = xm1[:, 2 * GROUP_W:3 * GROUP_W]
    lane2 = _iota2(xm2.shape, 1)
    feat = jnp.where(lane2 < 64, jnp.tanh(xm2), jnp.where(lane2 < 128, xm2, _sigmoid(xm2)))
    lr = _mm(feat, wlr_ref[...])
    w_raw = prm_ref[0:1, :] + lr[:, 0:GROUP_W]
    logw = -jnp.exp(-_softplus(-w_raw) - 0.5)
    a = _sigmoid(prm_ref[1:2, :] + lr[:, GROUP_W:2 * GROUP_W])
    gb = lr[:, 2 * GROUP_W:3 * GROUP_W]
    kkraw = kb * prm_ref[2:3, :]
    k = kb * (1.0 + (a - 1.0) * prm_ref[3:4, :])

    incl1, _ = _group_masks(rows, n)
    incl, strict = _group_masks(r2, n)
    blk = (_iota2((pw, pw), 0) & -HD_B) == (_iota2((pw, pw), 1) & -HD_B)
    blkf = blk.astype(F32)
    cums = [_mm01(incl1, logw[p * rows:(p + 1) * rows], 3) for p in range(nprob)]

    units = [(p, q) for p in range(nprob) for q in range(npair)]

    def st(x):
        return _stack_heads(x, 2, HD_B)

    pre = []
    for p, q in units:
        rs = slice(p * rows, (p + 1) * rows)
        sl = slice(q * pw, (q + 1) * pw)
        kkr = kkraw[rs, sl]
        rp, kp, vp, ap = r[rs, sl], k[rs, sl], vb[rs, sl], a[rs, sl]
        sums = _mm_x01(jnp.concatenate([kkr * kkr, rp * kp * prm_ref[4:5, sl]], axis=0), blkf, 2)
        kk = kkr * lax.rsqrt(sums[:rows] + 1e-6)
        cump = cums[p][:, sl]
        ginv = jnp.exp(-cump)
        rt = rp * jnp.exp(cump)
        at = -kk * jnp.exp(cump - logw[rs, sl])
        bt = kk * ap * ginv
        kt = kp * ginv
        pre.append(dict(rt=rt, at=at, bt=bt, kt=kt, vp=vp, cump=cump, bonus=sums[rows:] * vp,
                        vst=st(vp), gb=gb[rs, sl]))
    mats = [_mm_nt(jnp.concatenate([st(u["at"]), st(u["rt"])], axis=0),
                   jnp.concatenate([st(u["bt"]), st(u["kt"])], axis=0)) for u in pre]
    tinv = _unit_lower_inv_many([-jnp.where(strict, m[:r2, :r2], 0.0) for m in mats], n)

    u0s, y0s = [], []
    for (p, q), u in zip(units, pre):
        u0_rows, y0_rows = [], []
        for gi in range(groups):
            gs = slice(gi * n, (gi + 1) * n)
            uy = _mm_nt(jnp.concatenate([u["at"][gs], u["rt"][gs]], axis=0), sp_scr[p * groups + gi, q])
            u0_rows.append(uy[:n])
            y0_rows.append(uy[n:])
        u0s.append(jnp.concatenate(u0_rows, axis=0))
        y0s.append(jnp.concatenate(y0_rows, axis=0))
    x1 = [_mm(jnp.where(strict, m[:r2, r2:], 0.0), u["vst"]) for m, u in zip(mats, pre)]
    ust = [_mm(ti, st(u0) + x) for ti, u0, x in zip(tinv, u0s, x1)]
    yst = [_mm(jnp.concatenate([jnp.where(incl, m[r2:, :r2], 0.0), jnp.where(incl, m[r2:, r2:], 0.0)], axis=1),
               jnp.concatenate([us_, u["vst"]], axis=0)) for m, us_, u in zip(mats, ust, pre)]
    ys = [y0 + ys_[:rows] + ys_[rows:] for y0, ys_ in zip(y0s, yst)]
    means = [_mm_x01(y, blkf, 2) * (1.0 / HD_B) for y in ys]
    ycs = [y - m for y, m in zip(ys, means)]
    variances = [_mm_x01(yc * yc, blkf, 2) * (1.0 / HD_B) for yc in ycs]

    for (p, q), u, us_, yc, var in zip(units, pre, ust, ycs, variances):
        sl = slice(q * pw, (q + 1) * pw)
        uu = us_[:rows] + us_[rows:]
        for gi in range(groups):
            gs = slice(gi * n, (gi + 1) * n)
            upd = _mm_tn(jnp.concatenate([uu[gs], u["vp"][gs]], axis=0),
                         jnp.concatenate([u["bt"][gs], u["kt"][gs]], axis=0))
            glast = jnp.exp(u["cump"][gi * n + n - 1:gi * n + n, :])
            sp = sp_scr[p * groups + gi, q]
            sp_scr[p * groups + gi, q] = jnp.where(blk, sp + upd, 0.0) * glast
        yn = yc * lax.rsqrt(var + GN_EPS) * prm_ref[5:6, sl] + prm_ref[6:7, sl]
        val = (yn + u["bonus"]) * u["gb"]
        y_ref[p * groups:(p + 1) * groups, :, sl] = val.reshape(groups, n, pw)

    @pl.when(t == pl.num_programs(1) - 1)
    def _():
        s_ref[...] = sp_scr[...]


def _rwkv_call(proj, carry, s0, mur, muw, wlr, prm, bb, n, groups):
    b, t, _ = proj.shape
    wag_w = 2 * LANES
    npair, pw = H_B // 2, 2 * HD_B
    return pl.pallas_call(
        functools.partial(_rwkv_kernel, groups=groups),
        grid=(b // bb, t // n),
        in_specs=[
            pl.BlockSpec((bb, n, 3 * GROUP_W), lambda i, k: (i, k, EV_RKV // (3 * GROUP_W))),
            pl.BlockSpec((bb, n, wag_w), lambda i, k: (i, k, EV_WAG // wag_w)),
            pl.BlockSpec((bb, SUBLANES, 3 * GROUP_W), lambda i, k: (i, 0, EV_RKV // (3 * GROUP_W))),
            pl.BlockSpec((bb, SUBLANES, wag_w), lambda i, k: (i, 0, EV_WAG // wag_w)),
            pl.BlockSpec((bb, npair, pw, pw), lambda i, k: (i, 0, 0, 0)),
            pl.BlockSpec((1, 3 * GROUP_W), lambda i, k: (0, 0)),
            pl.BlockSpec((1, wag_w), lambda i, k: (0, 0)),
            pl.BlockSpec((wag_w, 3 * GROUP_W), lambda i, k: (0, 0)),
            pl.BlockSpec((SUBLANES, GROUP_W), lambda i, k: (0, 0)),
        ],
        out_specs=[
            pl.BlockSpec((bb, n, GROUP_W), lambda i, k: (i, k, 0)),
            pl.BlockSpec((bb, npair, pw, pw), lambda i, k: (i, 0, 0, 0)),
        ],
        out_shape=[jax.ShapeDtypeStruct((b, t, GROUP_W), F32),
                   jax.ShapeDtypeStruct((b, npair, pw, pw), F32)],
        scratch_shapes=[pltpu.VMEM((bb, n + SUBLANES, 3 * GROUP_W), F32),
                        pltpu.VMEM((bb, n + SUBLANES, wag_w), F32),
                        pltpu.VMEM((bb, npair, pw, pw), F32)],
        compiler_params=_cparams("parallel", "arbitrary"),
        name="rwkv7",
    )(proj, proj, carry, carry, s0, mur, muw, wlr, prm)


def _rwkv_state_to_pairs(s):
    b = s.shape[0]
    s = s.reshape(b, H_B // 2, 2, HD_B, HD_B)
    zero = jnp.zeros_like(s[:, :, 0])
    top = jnp.concatenate([s[:, :, 0], zero], axis=-1)
    bot = jnp.concatenate([zero, s[:, :, 1]], axis=-1)
    return jnp.concatenate([top, bot], axis=-2)


def _rwkv_state_from_pairs(sp):
    b = sp.shape[0]
    heads = jnp.stack([sp[:, :, :HD_B, :HD_B], sp[:, :, HD_B:, HD_B:]], axis=2)
    return heads.reshape(b, H_B, HD_B, HD_B)


def _gla_kernel(qk_ref, v_ref, z_ref, gkin_ref, s0_ref, wgk_ref, bgk_ref, gain_ref,
                o_ref, s_ref, s_scr):
    t = pl.program_id(1)
    n = qk_ref.shape[1]
    sub = min(GLA_SUB, n)
    npair = H_C // 2
    pw = 2 * DK_C
    qkw = H_C * DK_C

    @pl.when(t == 0)
    def _():
        for p in range(npair):
            s_scr[p] = s0_ref[0, 2 * p:2 * p + 2].reshape(pw, DV_C)

    qk = qk_ref[0]
    q = qk[:, :qkw] * (DK_C ** -0.5)
    k = qk[:, qkw:]
    v = v_ref[0]
    z = z_ref[0]
    gk = _log_sigmoid(_mm(gkin_ref[0], wgk_ref[...]) + bgk_ref[...]) * (1.0 / GLA_NORM)
    incl, _ = _tril_masks(sub)
    inclf = incl.astype(F32)
    lane_p = _iota2((1, pw), 1)
    states = [s_scr[p] for p in range(npair)]
    eye_p = _eye(pw)

    for c in range(n // sub):
        rows = slice(c * sub, (c + 1) * sub)
        bc = _mm_f32(inclf, gk[rows])
        blast = bc[sub - 1:sub, :]
        qd = q[rows] * jnp.exp(bc)
        kn = k[rows] * jnp.exp(-bc)
        kd = k[rows] * jnp.exp(blast - bc)
        gl = jnp.exp(blast)
        for p in range(npair):
            sl = slice(p * pw, (p + 1) * pw)
            sp = states[p]
            glcol = _mm_nt_f32(eye_p, jnp.broadcast_to(gl[:, sl], (SUBLANES, pw)))[:, 0:1]
            upd = None
            for j in range(2):
                h = 2 * p + j
                mh = jnp.logical_and(lane_p >= j * DK_C, lane_p < (j + 1) * DK_C)
                qdh = jnp.where(mh, qd[:, sl], 0.0)
                kdh = jnp.where(mh, kd[:, sl], 0.0)
                vh = v[rows, h * DV_C:(h + 1) * DV_C]
                att = jnp.where(incl, _mm_nt(qdh, kn[:, sl]), 0.0)
                o = _mm(att, vh) + _mm(qdh, sp)
                du = _mm_tn(kdh, vh)
                upd = du if upd is None else upd + du
                zh = z[rows, h * DV_C:(h + 1) * DV_C]
                o_ref[0, rows, h * DV_C:(h + 1) * DV_C] = _rms(o) * gain_ref[...] * _silu(zh)
            states[p] = sp * glcol + upd

    for p in range(npair):
        s_scr[p] = states[p]

    @pl.when(t == pl.num_programs(1) - 1)
    def _():
        for p in range(npair):
            s_ref[0, 2 * p:2 * p + 2] = states[p].reshape(2, DK_C, DV_C)


def _gla_call(proj, s0, wgk, bgk, gain, n):
    b, t, _ = proj.shape
    return pl.pallas_call(
        _gla_kernel,
        grid=(b, t // n),
        in_specs=[
            pl.BlockSpec((1, n, GROUP_W), lambda i, k: (i, k, OD_CQK // GROUP_W)),
            pl.BlockSpec((1, n, GROUP_W), lambda i, k: (i, k, OD_CV // GROUP_W)),
            pl.BlockSpec((1, n, GROUP_W), lambda i, k: (i, k, OD_CZ // GROUP_W)),
            pl.BlockSpec((1, n, LANES), lambda i, k: (i, k, OD_CGK // LANES)),
            pl.BlockSpec((1, H_C, DK_C, DV_C), lambda i, k: (i, 0, 0, 0)),
            pl.BlockSpec((LANES, H_C * DK_C), lambda i, k: (0, 0)),
            pl.BlockSpec((1, H_C * DK_C), lambda i, k: (0, 0)),
            pl.BlockSpec((1, DV_C), lambda i, k: (0, 0)),
        ],
        out_specs=[
            pl.BlockSpec((1, n, GROUP_W), lambda i, k: (i, k, 0)),
            pl.BlockSpec((1, H_C, DK_C, DV_C), lambda i, k: (i, 0, 0, 0)),
        ],
        out_shape=[jax.ShapeDtypeStruct((b, t, GROUP_W), F32),
                   jax.ShapeDtypeStruct((b, H_C, DK_C, DV_C), F32)],
        scratch_shapes=[pltpu.VMEM((H_C // 2, 2 * DK_C, DV_C), F32)],
        compiler_params=_cparams("parallel", "arbitrary"),
        name="gla",
    )(proj, proj, proj, proj, s0, wgk, bgk, gain)


def _mlstm_kernel(qk_ref, v_ref, og_ref, if_ref, c0_ref, n0_ref, m0_ref, bif_ref, gain_ref,
                  h_ref, c_ref, nn_ref, m_ref, c_scr, n_scr, m_scr):
    t = pl.program_id(1)
    n = qk_ref.shape[1]
    npair = H_D // 2
    pw = 2 * DK_D
    qkw = H_D * DK_D

    @pl.when(t == 0)
    def _():
        for p in range(npair):
            c_scr[p] = c0_ref[0, 2 * p:2 * p + 2].reshape(pw, DV_D)
        n_scr[...] = n0_ref[0]
        m_scr[...] = m0_ref[0]

    x = if_ref[0] + bif_ref[...]
    incl, _ = _tril_masks(n)
    fcum = _mm_f32(incl.astype(F32), _log_sigmoid(x))
    fct = _transpose_rows(fcum)
    xt = _transpose_rows(x)
    qk = qk_ref[0]
    q = qk[:, :qkw]
    k = qk[:, qkw:] * (DK_D ** -0.5)
    v = v_ref[0]
    og = og_ref[0]
    lane_p = _iota2((1, pw), 1)
    lane_m = _iota2((1, LANES), 1)
    row_p = _iota2((pw, 1), 0)
    m_row = m_scr[...]
    m_new_row = m_row

    for p in range(npair):
        sl = slice(p * pw, (p + 1) * pw)
        cp = c_scr[p]
        nrow = n_scr[:, sl]
        updc, updn, cds = None, None, []
        for j in range(2):
            h = 2 * p + j
            mh = jnp.logical_and(lane_p >= j * DK_D, lane_p < (j + 1) * DK_D)
            fcol = fcum[:, H_D + h:H_D + h + 1]
            frow = fct[H_D + h:H_D + h + 1, :]
            icol = x[:, h:h + 1]
            irow = xt[h:h + 1, :]
            flast = fcum[n - 1:n, H_D + h:H_D + h + 1]
            mprev = m_row[:, h:h + 1]
            log_d = jnp.where(incl, fcol - frow + irow, -jnp.inf)
            m_in = jnp.max(log_d, axis=-1, keepdims=True)
            log_e = flast - fcol + icol
            m_e = jnp.max(log_e, axis=0, keepdims=True)
            m_t = jnp.maximum(fcol + mprev, m_in)
            w_in = jnp.exp(fcol + mprev - m_t)
            qh = jnp.where(mh, q[:, sl], 0.0)
            kh = jnp.where(mh, k[:, sl], 0.0)
            vh = v[:, h * DV_D:(h + 1) * DV_D]
            dm = jnp.exp(log_d - m_t) * _mm_nt(qh, kh)
            num = w_in * _mm(qh, cp) + _mm(dm, vh)
            den = (w_in * jnp.sum(qh * nrow, axis=-1, keepdims=True)
                   + jnp.sum(dm, axis=-1, keepdims=True))
            hout = num / jnp.maximum(jnp.abs(den), jnp.exp(-m_t))
            m_new = jnp.maximum(flast + mprev, m_e)
            cds.append(jnp.exp(flast + mprev - m_new))
            ke = kh * jnp.exp(log_e - m_new)
            dc = _mm_tn(ke, vh)
            dn = jnp.sum(ke, axis=0, keepdims=True)
            updc = dc if updc is None else updc + dc
            updn = dn if updn is None else updn + dn
            m_new_row = jnp.where(lane_m == h, m_new, m_new_row)
            ogh = og[:, h * DV_D:(h + 1) * DV_D]
            h_ref[0, :, h * DV_D:(h + 1) * DV_D] = _sigmoid(ogh) * (_rms(hout) * gain_ref[...])
        c_scr[p] = cp * jnp.where(row_p < DK_D, cds[0], cds[1]) + updc
        n_scr[:, sl] = nrow * jnp.where(lane_p < DK_D, cds[0], cds[1]) + updn
    m_scr[...] = m_new_row

    @pl.when(t == pl.num_programs(1) - 1)
    def _():
        for p in range(npair):
            c_ref[0, 2 * p:2 * p + 2] = c_scr[p].reshape(2, DK_D, DV_D)
        nn_ref[0] = n_scr[...]
        m_ref[0] = m_scr[...]


def _mlstm_call(proj, c0, n0, m0, bif, gain, n):
    b, t, _ = proj.shape
    qkw = H_D * DK_D
    return pl.pallas_call(
        _mlstm_kernel,
        grid=(b, t // n),
        in_specs=[
            pl.BlockSpec((1, n, GROUP_W), lambda i, k: (i, k, OD_DQK // GROUP_W)),
            pl.BlockSpec((1, n, GROUP_W), lambda i, k: (i, k, OD_DV // GROUP_W)),
            pl.BlockSpec((1, n, GROUP_W), lambda i, k: (i, k, OD_DO // GROUP_W)),
            pl.BlockSpec((1, n, LANES), lambda i, k: (i, k, OD_DIF // LANES)),
            pl.BlockSpec((1, H_D, DK_D, DV_D), lambda i, k: (i, 0, 0, 0)),
            pl.BlockSpec((1, 1, qkw), lambda i, k: (i, 0, 0)),
            pl.BlockSpec((1, 1, LANES), lambda i, k: (i, 0, 0)),
            pl.BlockSpec((1, LANES), lambda i, k: (0, 0)),
            pl.BlockSpec((1, DV_D), lambda i, k: (0, 0)),
        ],
        out_specs=[
            pl.BlockSpec((1, n, GROUP_W), lambda i, k: (i, k, 0)),
            pl.BlockSpec((1, H_D, DK_D, DV_D), lambda i, k: (i, 0, 0, 0)),
            pl.BlockSpec((1, 1, qkw), lambda i, k: (i, 0, 0)),
            pl.BlockSpec((1, 1, LANES), lambda i, k: (i, 0, 0)),
        ],
        out_shape=[jax.ShapeDtypeStruct((b, t, GROUP_W), F32),
                   jax.ShapeDtypeStruct((b, H_D, DK_D, DV_D), F32),
                   jax.ShapeDtypeStruct((b, 1, qkw), F32),
                   jax.ShapeDtypeStruct((b, 1, LANES), F32)],
        scratch_shapes=[pltpu.VMEM((H_D // 2, 2 * DK_D, DV_D), F32),
                        pltpu.VMEM((1, qkw), F32),
                        pltpu.VMEM((1, LANES), F32)],
        compiler_params=_cparams("parallel", "arbitrary"),
        name="mlstm",
    )(proj, proj, proj, proj, c0, n0, m0, bif, gain)


def _pad_cols(w, width):
    return jnp.pad(w, ((0, 0), (0, width - w.shape[1])))


def _pad_rows(w, rows, at=0):
    return jnp.pad(w, ((at, rows - at - w.shape[0]), (0, 0)))


def _even_in_weight(w):
    pa, pb = w[:, :2056], w[:, 2056:]
    cols = [
        pb[:, 0:1536],
        pa[:, 0:1536],
        pa[:, 1544:2056],
        pb[:, 1536:1792],
        _pad_cols(pa[:, 1536:1544], LANES),
    ]
    return _pad_cols(jnp.concatenate(cols, axis=1), EV_COLS).astype(BF16)


def _odd_in_weight(w):
    pc, pd = w[:, :1552], w[:, 1552:]
    cols = [
        pc[:, 0:512],
        pd[:, 0:512],
        pc[:, 512:1024],
        pc[:, 1040:1552],
        pd[:, 512:1024],
        pd[:, 1032:1544],
        _pad_cols(pc[:, 1024:1040], LANES),
        _pad_cols(pd[:, 1024:1032], LANES),
    ]
    return jnp.concatenate(cols, axis=1).astype(BF16)


def _row(v):
    return v.reshape(1, -1).astype(F32)


def _tiles(x):
    b, t, _ = x.shape
    tb = min(t, 1024)
    bb = min(b, 1024 // tb)
    return bb, tb


def _trunk(x, mod, states, wts):
    xbuf, s_delta, s_rwkv, s_gla, s_mc, s_mn, s_mm = states
    b, t, _ = x.shape
    bb, tb = _tiles(x)
    n = min(CHUNK, t)
    groups = CHUNK // n
    rbb = PROBLEMS * groups
    new_even = ([], [], [])
    new_odd = ([], [], [], [])
    for l in range(DEPTH):
        lw = wts["layers"][l]
        m_l = mod[l]
        x = _ffn_call(x, m_l, lw["gain0"], lw["wg0"], lw["wu0"], lw["wd0"], 0, bb, tb, 256)
        i = l // 2
        if l % 2 == 0:
            h, proj = _adaln_proj_call(x, m_l, lw["gain1"], lw["w_in"], bb, tb, 1024)
            if xbuf is None:
                carry = jnp.zeros((b, SUBLANES, EV_COLS), F32)
            else:
                rows = _rows_proj_call(xbuf[i].reshape(b * (CONV_W - 1), D_MODEL), lw["w_in"], 1024)
                carry = jnp.pad(rows.reshape(b, CONV_W - 1, EV_COLS),
                                ((0, 0), (SUBLANES - CONV_W + 1, 0), (0, 0)))
            oa, sd = _delta_call(proj, carry, s_delta[i], lw["conv_w"], lw["delta_hp"], lw["gain_a"],
                                 rbb, n, groups)
            ob, sr = _rwkv_call(proj, carry, _rwkv_state_to_pairs(s_rwkv[i]), lw["mu_rkv"], lw["mu_wag"],
                                lw["w_lora"], lw["rwkv_prm"], rbb, n, groups)
            for lst, val in zip(new_even, (h[:, t - (CONV_W - 1):], sd, _rwkv_state_from_pairs(sr))):
                lst.append(val)
        else:
            _, proj = _adaln_proj_call(x, m_l, lw["gain1"], lw["w_in"], bb, tb, 1664)
            oa, sg = _gla_call(proj, s_gla[i], lw["w_gk2"], lw["b_gk"], lw["gain_c"], n)
            ob, sc, sn, sm = _mlstm_call(proj, s_mc[i], s_mn[i].reshape(b, 1, H_D * DK_D),
                                         _pad_cols(s_mm[i], LANES).reshape(b, 1, LANES),
                                         lw["b_if"], lw["gain_d"], n)
            for lst, val in zip(new_odd, (sg, sc, sn.reshape(b, H_D, DK_D), sm[:, 0, :H_D])):
                lst.append(val)
        x = _outproj_call(x, oa, ob, m_l, lw["w_out"], bb, tb)
        x = _ffn_call(x, m_l, lw["gain2"], lw["wg1"], lw["wu1"], lw["wd1"], 6, bb, tb, 256)
    y = _final_norm_call(x, wts["final_gain"], bb, tb)
    stacked = [jnp.stack(lst) for lst in new_even + new_odd]
    return (y, *stacked)


def _prepare_weights(norm_gain, final_gain, w_ffn_gate, w_ffn_up, w_ffn_down, w_in_even, w_out_even,
                     conv_w, a_log, dt_bias, gain_a, mu_b, w0_b, w_w2, a0_b, w_a2, w_g2, k_k, k_a, r_k,
                     lnx_gain, lnx_bias, w_in_odd, w_out_odd, w_gk2, b_gk, gain_c, b_i, b_f, gain_d):
    layers = []
    for l in range(DEPTH):
        i = l // 2
        lw = {
            "gain0": _row(norm_gain[l, 0]), "gain1": _row(norm_gain[l, 1]), "gain2": _row(norm_gain[l, 2]),
            "wg0": w_ffn_gate[l, 0].astype(BF16), "wu0": w_ffn_up[l, 0].astype(BF16),
            "wd0": w_ffn_down[l, 0].astype(BF16),
            "wg1": w_ffn_gate[l, 1].astype(BF16), "wu1": w_ffn_up[l, 1].astype(BF16),
            "wd1": w_ffn_down[l, 1].astype(BF16),
        }
        if l % 2 == 0:
            lw["w_in"] = _even_in_weight(w_in_even[i])
            lw["w_out"] = w_out_even[i].astype(BF16)
            lw["conv_w"] = _pad_rows(conv_w[i].astype(F32), SUBLANES)
            lw["delta_hp"] = _pad_rows(jnp.stack([_pad_cols(_row(a_log[i]), LANES)[0],
                                                  _pad_cols(_row(dt_bias[i]), LANES)[0]]), SUBLANES)
            lw["gain_a"] = _row(gain_a[i])
            lw["mu_rkv"] = _row(mu_b[i, :1536])
            lw["mu_wag"] = _row(mu_b[i, 1536:])
            lora = jnp.zeros((2 * LANES, 3 * GROUP_W), F32)
            lora = lora.at[0:64, 0:GROUP_W].set(w_w2[i])
            lora = lora.at[64:128, GROUP_W:2 * GROUP_W].set(w_a2[i])
            lora = lora.at[128:256, 2 * GROUP_W:].set(w_g2[i])
            lw["w_lora"] = lora.astype(BF16)
            lw["rwkv_prm"] = jnp.stack([w0_b[i], a0_b[i], k_k[i], k_a[i], r_k[i].reshape(-1),
                                        lnx_gain[i], lnx_bias[i], jnp.zeros_like(w0_b[i])]).astype(F32)
        else:
            lw["w_in"] = _odd_in_weight(w_in_odd[i])
            lw["w_out"] = w_out_odd[i].astype(BF16)
            lw["w_gk2"] = _pad_rows(w_gk2[i], LANES).astype(BF16)
            lw["b_gk"] = _row(b_gk[i])
            lw["gain_c"] = _row(gain_c[i])
            lw["b_if"] = _pad_cols(_row(jnp.concatenate([b_i[i], b_f[i]])), LANES)
            lw["gain_d"] = _row(gain_d[i])
        layers.append(lw)
    return {"layers": layers, "final_gain": _row(final_gain)}


def kernel(x_prompt, x_sample, c_prompt, c_sample, state_xbuf_even, state_delta, state_rwkv, state_gla,
           state_mlstm_c, state_mlstm_n, state_mlstm_m, w_mod, b_mod, norm_gain, final_gain, w_ffn_gate,
           w_ffn_up, w_ffn_down, w_in_even, w_out_even, conv_w, a_log, dt_bias, gain_a, mu_b, w0_b, w_w2,
           a0_b, w_a2, w_g2, k_k, k_a, r_k, lnx_gain, lnx_bias, w_in_odd, w_out_odd, w_gk2, b_gk, gain_c,
           b_i, b_f, gain_d):
    wts = _prepare_weights(norm_gain, final_gain, w_ffn_gate, w_ffn_up, w_ffn_down, w_in_even, w_out_even,
                           conv_w, a_log, dt_bias, gain_a, mu_b, w0_b, w_w2, a0_b, w_a2, w_g2, k_k, k_a,
                           r_k, lnx_gain, lnx_bias, w_in_odd, w_out_odd, w_gk2, b_gk, gain_c, b_i, b_f,
                           gain_d)
    bp, bs = x_prompt.shape[0], x_sample.shape[0]
    c_all = jnp.concatenate([c_prompt, c_sample], axis=0).astype(F32)
    mod = _mod_call(c_all, w_mod, b_mod).reshape(DEPTH, bp + bs, N_MOD, 1, D_MODEL)
    mod_p, mod_s = mod[:, :bp], mod[:, bp:]

    def zeros(shape):
        return jnp.zeros(shape, F32)

    zero_states = (
        None,
        zeros((N_EVEN, bp, H_A, DK_A, DK_A)),
        zeros((N_EVEN, bp, H_B, HD_B, HD_B)),
        zeros((N_ODD, bp, H_C, DK_C, DV_C)),
        zeros((N_ODD, bp, H_D, DK_D, DV_D)),
        zeros((N_ODD, bp, H_D, DK_D)),
        zeros((N_ODD, bp, H_D)),
    )
    y_p, xb_p, dl_p, rw_p, gl_p, mc_p, mn_p, mm_p = _trunk(x_prompt, mod_p, zero_states, wts)
    sample_states = (state_xbuf_even, state_delta, state_rwkv, state_gla,
                     state_mlstm_c, state_mlstm_n, state_mlstm_m)
    y_s, xb_s, dl_s, rw_s, gl_s, mc_s, mn_s, mm_s = _trunk(x_sample, mod_s, sample_states, wts)
    return (y_p, y_s, xb_p, xb_s, dl_p, dl_s, rw_p, rw_s, gl_p, gl_s, mc_p, mc_s, mn_p, mn_s, mm_p, mm_s)
```

```python
import functools

import jax
import jax.numpy as jnp
from jax import lax
from jax.experimental import pallas as pl
from jax.experimental.pallas import tpu as pltpu

F32 = jnp.float32
BF16 = jnp.bfloat16
HIGHEST = lax.Precision.HIGHEST

D_MODEL = 1024
DEPTH = 4
N_EVEN = 2
N_ODD = 2
D_FF = 2816
N_MOD = 9
EPS = 1e-6
GN_EPS = 64e-5
CONV_W = 4
H_A, DK_A = 4, 128
H_B, HD_B = 8, 64
H_C, DK_C, DV_C = 4, 64, 128
H_D, DK_D, DV_D = 4, 64, 128
GLA_NORM = 16.0
GROUP_W = 512

LANES = 128
SUBLANES = 8
BF16_ROWS = 16
VMEM_LIMIT_BYTES = 48 * 1024 * 1024

EV_RKV, EV_QKV, EV_Z, EV_WAG, EV_AB, EV_COLS = 0, 1536, 3072, 3584, 3840, 4096
OD_CQK, OD_DQK, OD_CV, OD_CZ, OD_DV, OD_DO, OD_CGK, OD_DIF, OD_COLS = (
    0, 512, 1024, 1536, 2048, 2560, 3072, 3200, 3328)

CHUNK = 64
GLA_SUB = 16
PROBLEMS = 2
PROBLEMS_LONG = 4


def _bf(x):
    if x.dtype == BF16:
        return x
    if x.shape[-2] % BF16_ROWS == 0 and x.shape[-1] % BF16_ROWS == 0:
        return x.astype(BF16)
    return x


def _pair(a, b):
    a, b = _bf(a), _bf(b)
    if a.dtype != b.dtype:
        a, b = a.astype(F32), b.astype(F32)
    return a, b


def _mm(a, b):
    a, b = _pair(a, b)
    return jnp.dot(a, b, preferred_element_type=F32)


def _mm_nt(a, b):
    a, b = _pair(a, b)
    return lax.dot_general(a, b, (((1,), (1,)), ((), ())), preferred_element_type=F32)


def _mm_tn(a, b):
    a, b = _pair(a, b)
    return lax.dot_general(a, b, (((0,), (0,)), ((), ())), preferred_element_type=F32)


def _mm_f32(a, b):
    return jnp.dot(a, b, precision=HIGHEST, preferred_element_type=F32)


def _mm_nt_f32(a, b):
    return lax.dot_general(a, b, (((1,), (1,)), ((), ())), precision=HIGHEST,
                           preferred_element_type=F32)


def _split_bf16(x, parts):
    out, r = [], x
    for i in range(parts):
        p = r.astype(BF16)
        out.append(p)
        if i + 1 < parts:
            r = r - p.astype(F32)
    return out


def _mm01(a01, x, parts):
    a = a01.astype(BF16)
    acc = None
    for p in _split_bf16(x, parts):
        d = jnp.dot(a, p, preferred_element_type=F32)
        acc = d if acc is None else acc + d
    return acc


def _mm_x01(x, b01, parts):
    b = b01.astype(BF16)
    acc = None
    for p in _split_bf16(x, parts):
        d = jnp.dot(p, b, preferred_element_type=F32)
        acc = d if acc is None else acc + d
    return acc


def _mm_nt01(a01, x, parts):
    a = a01.astype(BF16)
    acc = None
    for p in _split_bf16(x, parts):
        d = lax.dot_general(a, p, (((1,), (1,)), ((), ())), preferred_element_type=F32)
        acc = d if acc is None else acc + d
    return acc


def _sigmoid(x):
    return jax.nn.sigmoid(x)


def _silu(x):
    return x * jax.nn.sigmoid(x)


def _softplus(x):
    return jnp.maximum(x, 0.0) + jnp.log1p(jnp.exp(-jnp.abs(x)))


def _log_sigmoid(x):
    return -_softplus(-x)


def _rms(x, eps=EPS):
    return x * lax.rsqrt(jnp.mean(x * x, axis=-1, keepdims=True) + eps)


def _l2n(x):
    return x * lax.rsqrt(jnp.sum(x * x, axis=-1, keepdims=True) + 1e-6)


def _iota2(shape, dim):
    return lax.broadcasted_iota(jnp.int32, shape, dim)


def _tril_masks(n):
    r, c = _iota2((n, n), 0), _iota2((n, n), 1)
    return r >= c, r > c


def _group_masks(size, n):
    r, c = _iota2((size, size), 0), _iota2((size, size), 1)
    same = (r & -n) == (c & -n)
    return jnp.logical_and(same, r >= c), jnp.logical_and(same, r > c)


def _eye(n):
    return (_iota2((n, n), 0) == _iota2((n, n), 1)).astype(F32)


def _transpose_rows(x):
    return _mm_nt_f32(_eye(x.shape[1]), x)


def _unit_lower_inv_many(mats, n):
    eye = _eye(mats[0].shape[0])
    ms = [-a for a in mats]
    ps = [eye + m for m in ms]
    covered = 2
    while covered < n:
        ms = [_mm(m, m) for m in ms]
        ps = [p + _mm(p, m) for p, m in zip(ps, ms)]
        covered *= 2
    return ps


def _cparams(*sem):
    return pltpu.CompilerParams(dimension_semantics=sem, vmem_limit_bytes=VMEM_LIMIT_BYTES)


def _mod_kernel(c_ref, w_ref, b_ref, o_ref):
    cs = _silu(c_ref[...])
    o_ref[...] = _mm(cs, w_ref[...]) + b_ref[...]


def _mod_call(c_all, w_mod, b_mod):
    rows = c_all.shape[0]
    tn = 1024
    width = N_MOD * D_MODEL
    return pl.pallas_call(
        _mod_kernel,
        grid=(DEPTH, width // tn),
        in_specs=[
            pl.BlockSpec((rows, D_MODEL), lambda l, j: (0, 0)),
            pl.BlockSpec((None, D_MODEL, tn), lambda l, j: (l, 0, j)),
            pl.BlockSpec((None, 1, tn), lambda l, j: (l, 0, j)),
        ],
        out_specs=pl.BlockSpec((None, rows, tn), lambda l, j: (l, 0, j)),
        out_shape=jax.ShapeDtypeStruct((DEPTH, rows, width), F32),
        compiler_params=_cparams("parallel", "parallel"),
        name="mod",
    )(c_all, w_mod, b_mod.reshape(DEPTH, 1, width))


def _adaln(x, gain, scale, shift):
    return _rms(x) * gain * (1.0 + scale) + shift


def _ffn_kernel(x_ref, sh_ref, sc_ref, gt_ref, gain_ref, wg_ref, wu_ref, wd_ref, o_ref,
                h_scr, acc_scr):
    f = pl.program_id(2)
    bb, tb, d = x_ref.shape

    @pl.when(f == 0)
    def _():
        h = _adaln(x_ref[...], gain_ref[...], sc_ref[...], sh_ref[...])
        h_scr[...] = h.reshape(bb * tb, d).astype(BF16)
        acc_scr[...] = jnp.zeros_like(acc_scr)

    h = h_scr[...]
    g = jnp.dot(h, wg_ref[...], preferred_element_type=F32)
    u = jnp.dot(h, wu_ref[...], preferred_element_type=F32)
    a = (_silu(g) * u).astype(BF16)
    acc_scr[...] += jnp.dot(a, wd_ref[...], preferred_element_type=F32)

    @pl.when(f == pl.num_programs(2) - 1)
    def _():
        y = acc_scr[...].reshape(bb, tb, d)
        o_ref[...] = x_ref[...] + 0.5 * (1.0 + gt_ref[...]) * y


def _mod_spec(bb, j, ngrid):
    if ngrid == 3:
        return pl.BlockSpec((bb, None, 1, D_MODEL), lambda b, t, f: (b, j, 0, 0))
    return pl.BlockSpec((bb, None, 1, D_MODEL), lambda b, t: (b, j, 0, 0))


def _ffn_call(x, mod, gain, wg, wu, wd, j0, bb, tb, tf):
    b, t, d = x.shape
    xspec = pl.BlockSpec((bb, tb, d), lambda i, k, f: (i, k, 0))
    return pl.pallas_call(
        _ffn_kernel,
        grid=(b // bb, t // tb, D_FF // tf),
        in_specs=[
            xspec,
            _mod_spec(bb, j0, 3), _mod_spec(bb, j0 + 1, 3), _mod_spec(bb, j0 + 2, 3),
            pl.BlockSpec((1, d), lambda i, k, f: (0, 0)),
            pl.BlockSpec((d, tf), lambda i, k, f: (0, f)),
            pl.BlockSpec((d, tf), lambda i, k, f: (0, f)),
            pl.BlockSpec((tf, d), lambda i, k, f: (f, 0)),
        ],
        out_specs=xspec,
        out_shape=jax.ShapeDtypeStruct(x.shape, x.dtype),
        scratch_shapes=[pltpu.VMEM((bb * tb, d), BF16), pltpu.VMEM((bb * tb, d), F32)],
        compiler_params=_cparams("parallel", "parallel", "arbitrary"),
        name="ffn",
    )(x, mod, mod, mod, gain, wg, wu, wd)


def _adaln_proj_kernel(x_ref, sh_ref, sc_ref, gain_ref, w_ref, h_ref, p_ref, hb_scr):
    j = pl.program_id(2)
    bb, tb, d = x_ref.shape

    @pl.when(j == 0)
    def _():
        h = _adaln(x_ref[...], gain_ref[...], sc_ref[...], sh_ref[...])
        h_ref[...] = h
        hb_scr[...] = h.reshape(bb * tb, d).astype(BF16)

    p = jnp.dot(hb_scr[...], w_ref[...], preferred_element_type=F32)
    p_ref[...] = p.reshape(bb, tb, p.shape[-1])


def _adaln_proj_call(x, mod, gain, w, bb, tb, tn):
    b, t, d = x.shape
    n = w.shape[1]
    xspec = pl.BlockSpec((bb, tb, d), lambda i, k, j: (i, k, 0))
    return pl.pallas_call(
        _adaln_proj_kernel,
        grid=(b // bb, t // tb, n // tn),
        in_specs=[
            xspec, _mod_spec(bb, 3, 3), _mod_spec(bb, 4, 3),
            pl.BlockSpec((1, d), lambda i, k, j: (0, 0)),
            pl.BlockSpec((d, tn), lambda i, k, j: (0, j)),
        ],
        out_specs=[xspec, pl.BlockSpec((bb, tb, tn), lambda i, k, j: (i, k, j))],
        out_shape=[jax.ShapeDtypeStruct(x.shape, F32), jax.ShapeDtypeStruct((b, t, n), F32)],
        scratch_shapes=[pltpu.VMEM((bb * tb, d), BF16)],
        compiler_params=_cparams("parallel", "parallel", "arbitrary"),
        name="adaln_proj",
    )(x, mod, mod, gain, w)


def _rows_proj_kernel(a_ref, w_ref, o_ref):
    o_ref[...] = _mm(a_ref[...], w_ref[...])


def _rows_proj_call(a, w, tn):
    m, k = a.shape
    n = w.shape[1]
    return pl.pallas_call(
        _rows_proj_kernel,
        grid=(n // tn,),
        in_specs=[pl.BlockSpec((m, k), lambda j: (0, 0)), pl.BlockSpec((k, tn), lambda j: (0, j))],
        out_specs=pl.BlockSpec((m, tn), lambda j: (0, j)),
        out_shape=jax.ShapeDtypeStruct((m, n), F32),
        compiler_params=_cparams("parallel"),
        name="rows_proj",
    )(a, w)


def _outproj_kernel(x_ref, oa_ref, ob_ref, gt_ref, w_ref, o_ref):
    bb, tb, d = x_ref.shape
    o = jnp.concatenate([oa_ref[...], ob_ref[...]], axis=-1).reshape(bb * tb, d)
    y = jnp.dot(o.astype(BF16), w_ref[...], preferred_element_type=F32).reshape(bb, tb, d)
    o_ref[...] = x_ref[...] + (1.0 + gt_ref[...]) * y


def _outproj_call(x, oa, ob, mod, w, bb, tb):
    b, t, d = x.shape
    xspec = pl.BlockSpec((bb, tb, d), lambda i, k: (i, k, 0))
    hspec = pl.BlockSpec((bb, tb, GROUP_W), lambda i, k: (i, k, 0))
    return pl.pallas_call(
        _outproj_kernel,
        grid=(b // bb, t // tb),
        in_specs=[xspec, hspec, hspec, _mod_spec(bb, 5, 2),
                  pl.BlockSpec((d, d), lambda i, k: (0, 0))],
        out_specs=xspec,
        out_shape=jax.ShapeDtypeStruct(x.shape, x.dtype),
        compiler_params=_cparams("parallel", "parallel"),
        name="outproj",
    )(x, oa, ob, mod, w)


def _final_norm_kernel(x_ref, g_ref, o_ref):
    o_ref[...] = _rms(x_ref[...]) * g_ref[...]


def _final_norm_call(x, gain, bb, tb):
    b, t, d = x.shape
    xspec = pl.BlockSpec((bb, tb, d), lambda i, k: (i, k, 0))
    return pl.pallas_call(
        _final_norm_kernel,
        grid=(b // bb, t // tb),
        in_specs=[xspec, pl.BlockSpec((1, d), lambda i, k: (0, 0))],
        out_specs=xspec,
        out_shape=jax.ShapeDtypeStruct(x.shape, x.dtype),
        compiler_params=_cparams("parallel", "parallel"),
        name="final_norm",
    )(x, gain)


def _stack_heads(x, nheads, head_w):
    lane = _iota2((1, x.shape[1]), 1)
    return jnp.concatenate(
        [jnp.where(jnp.logical_and(lane >= h * head_w, lane < (h + 1) * head_w), x, 0.0)
         for h in range(nheads)], axis=0)


def _delta_kernel(qkv_ref, z_ref, ab_ref, carry_ref, s0_ref, cw_ref, hp_ref, gain_ref,
                  o_ref, s_ref, ext_scr, s_scr, *, groups):
    t = pl.program_id(1)
    bb, n, _ = qkv_ref.shape
    rows = groups * n
    nprob = bb // groups
    hr = H_A * rows

    @pl.when(t == 0)
    def _():
        ext_scr[:, 0:SUBLANES, :] = carry_ref[...]
        s_scr[...] = s0_ref[...]

    @pl.when(t > 0)
    def _():
        ext_scr[:, 0:SUBLANES, :] = ext_scr[:, n:n + SUBLANES, :]

    ext_scr[:, SUBLANES:SUBLANES + n, :] = qkv_ref[...]
    conv = cw_ref[0:1, :] * ext_scr[:, 5:5 + n, :]
    for j in range(1, CONV_W):
        conv = conv + cw_ref[j:j + 1, :] * ext_scr[:, 5 + j:5 + j + n, :]
    x = _silu(conv).reshape(bb * n, 3 * GROUP_W)
    ab = ab_ref[...].reshape(bb * n, LANES)
    g = -jnp.exp(hp_ref[0:1, :]) * _softplus(ab + hp_ref[1:2, :])
    beta = _sigmoid(ab)
    z = z_ref[...].reshape(bb * n, GROUP_W)

    incl1, _ = _group_masks(rows, n)
    incl, strict = _group_masks(hr, n)
    lane = _iota2((1, LANES), 1)
    ones = jnp.ones((hr, LANES), F32)
    probs = range(nprob)

    def head_rows(a, h):
        return a[:, h * DK_A:(h + 1) * DK_A]

    gcol, bcol, qs, ks, vs, zs = [], [], [], [], [], []
    kst, lhs = [], []
    grow = []
    for p in probs:
        sl = slice(p * rows, (p + 1) * rows)
        gc = _mm01(incl1, g[sl], 3)
        gsel = jnp.concatenate([jnp.where(lane == h, gc, 0.0) for h in range(H_A)], axis=0)
        bsel = jnp.concatenate([jnp.where(lane == H_A + h, beta[sl], 0.0) for h in range(H_A)], axis=0)
        gcol.append(jnp.sum(gsel, axis=-1, keepdims=True))
        bcol.append(jnp.sum(bsel, axis=-1, keepdims=True))
        grow.append(_mm_nt01(ones, gsel, 3))
        xp = x[sl]
        qn = jnp.concatenate([_l2n(head_rows(xp[:, 0:GROUP_W], h)) * (DK_A ** -0.5)
                              for h in range(H_A)], axis=1)
        kn = jnp.concatenate([_l2n(head_rows(xp[:, GROUP_W:2 * GROUP_W], h)) for h in range(H_A)], axis=1)
        qs.append(jnp.concatenate([head_rows(qn, h) for h in range(H_A)], axis=0))
        ks.append(jnp.concatenate([head_rows(kn, h) for h in range(H_A)], axis=0))
        vs.append(jnp.concatenate([head_rows(xp[:, 2 * GROUP_W:], h) for h in range(H_A)], axis=0))
        zs.append(z[sl])
        k_st = _stack_heads(kn, H_A, DK_A)
        kst.append(k_st)
        lhs.append(jnp.concatenate([k_st * bcol[p], _stack_heads(qn, H_A, DK_A)], axis=0))

    kq = [_mm_nt(lhs[p], kst[p]) for p in probs]
    dec = [jnp.exp(jnp.where(incl, gcol[p] - grow[p], -jnp.inf)) for p in probs]
    a_low = [jnp.where(strict, kq[p][:hr] * dec[p], 0.0) for p in probs]
    tinv = _unit_lower_inv_many(a_low, n)
    eg = [jnp.exp(gcol[p]) for p in probs]
    kb = [ks[p] * bcol[p] for p in probs]
    sol = [_mm(tinv[p], jnp.concatenate([vs[p] * bcol[p], kb[p] * eg[p]], axis=1)) for p in probs]
    qg = [qs[p] * eg[p] for p in probs]

    us, oparts = [], []
    for p in probs:
        u_rows, o_rows = [], []
        for h in range(H_A):
            for gi in range(groups):
                r0 = h * rows + gi * n
                s = s_scr[p * groups + gi, h]
                ksq = _mm(jnp.concatenate([sol[p][r0:r0 + n, DK_A:], qg[p][r0:r0 + n]], axis=0), s)
                u_rows.append(sol[p][r0:r0 + n, :DK_A] - ksq[:n])
                o_rows.append(ksq[n:])
        us.append(jnp.concatenate(u_rows, axis=0))
        oparts.append(jnp.concatenate(o_rows, axis=0))
    outs = [oparts[p] + _mm(kq[p][hr:] * dec[p], us[p]) for p in probs]
    for p in probs:
        for h in range(H_A):
            for gi in range(groups):
                r0 = h * rows + gi * n
                glast = gcol[p][r0 + n - 1:r0 + n]
                kd = ks[p][r0:r0 + n] * jnp.exp(glast - gcol[p][r0:r0 + n])
                s = s_scr[p * groups + gi, h]
                s_scr[p * groups + gi, h] = s * jnp.exp(glast) + _mm_tn(kd, us[p][r0:r0 + n])
            o = outs[p][h * rows:(h + 1) * rows]
            val = _rms(o) * gain_ref[...] * _silu(head_rows(zs[p], h))
            o_ref[p * groups:(p + 1) * groups, :, h * DK_A:(h + 1) * DK_A] = val.reshape(groups, n, DK_A)

    @pl.when(t == pl.num_programs(1) - 1)
    def _():
        s_ref[...] = s_scr[...]


def _delta_call(proj, carry, s0, cw, hp, gain, bb, n, groups):
    b, t, _ = proj.shape
    return pl.pallas_call(
        functools.partial(_delta_kernel, groups=groups),
        grid=(b // bb, t // n),
        in_specs=[
            pl.BlockSpec((bb, n, 3 * GROUP_W), lambda i, k: (i, k, EV_QKV // (3 * GROUP_W))),
            pl.BlockSpec((bb, n, GROUP_W), lambda i, k: (i, k, EV_Z // GROUP_W)),
            pl.BlockSpec((bb, n, LANES), lambda i, k: (i, k, EV_AB // LANES)),
            pl.BlockSpec((bb, SUBLANES, 3 * GROUP_W), lambda i, k: (i, 0, EV_QKV // (3 * GROUP_W))),
            pl.BlockSpec((bb, H_A, DK_A, DK_A), lambda i, k: (i, 0, 0, 0)),
            pl.BlockSpec((SUBLANES, 3 * GROUP_W), lambda i, k: (0, 0)),
            pl.BlockSpec((SUBLANES, LANES), lambda i, k: (0, 0)),
            pl.BlockSpec((1, DK_A), lambda i, k: (0, 0)),
        ],
        out_specs=[
            pl.BlockSpec((bb, n, GROUP_W), lambda i, k: (i, k, 0)),
            pl.BlockSpec((bb, H_A, DK_A, DK_A), lambda i, k: (i, 0, 0, 0)),
        ],
        out_shape=[jax.ShapeDtypeStruct((b, t, GROUP_W), F32),
                   jax.ShapeDtypeStruct((b, H_A, DK_A, DK_A), F32)],
        scratch_shapes=[pltpu.VMEM((bb, n + SUBLANES, 3 * GROUP_W), F32),
                        pltpu.VMEM((bb, H_A, DK_A, DK_A), F32)],
        compiler_params=_cparams("parallel", "arbitrary"),
        name="delta",
    )(proj, proj, proj, carry, s0, cw, hp, gain)


def _rwkv_kernel(rkv_ref, wag_ref, crkv_ref, cwag_ref, s0_ref, mur_ref, muw_ref, wlr_ref, prm_ref,
                 y_ref, s_ref, ext1_scr, ext2_scr, sp_scr, *, groups):
    t = pl.program_id(1)
    bb, n, _ = rkv_ref.shape
    rows = groups * n
    nprob = bb // groups
    npair = H_B // 2
    pw = 2 * HD_B
    r2 = 2 * rows

    @pl.when(t == 0)
    def _():
        ext1_scr[:, 0:SUBLANES, :] = crkv_ref[...]
        ext2_scr[:, 0:SUBLANES, :] = cwag_ref[...]
        sp_scr[...] = s0_ref[...]

    @pl.when(t > 0)
    def _():
        ext1_scr[:, 0:SUBLANES, :] = ext1_scr[:, n:n + SUBLANES, :]
        ext2_scr[:, 0:SUBLANES, :] = ext2_scr[:, n:n + SUBLANES, :]

    cur1 = rkv_ref[...]
    cur2 = wag_ref[...]
    ext1_scr[:, SUBLANES:SUBLANES + n, :] = cur1
    ext2_scr[:, SUBLANES:SUBLANES + n, :] = cur2
    xm1 = cur1 + (ext1_scr[:, SUBLANES - 1:SUBLANES - 1 + n, :] - cur1) * mur_ref[...]
    xm2 = cur2 + (ext2_scr[:, SUBLANES - 1:SUBLANES - 1 + n, :] - cur2) * muw_ref[...]
    xm1 = xm1.reshape(bb * n, 3 * GROUP_W)
    xm2 = xm2.reshape(bb * n, 2 * LANES)
    r = xm1[:, 0:GROUP_W]
    kb = xm1[:, GROUP_W:2 * GROUP_W]
    vb = xm1[:, 2 * GROUP_W:3 * GROUP_W]
    lane2 = _iota2(xm2.shape, 1)
    feat = jnp.where(lane2 < 64, jnp.tanh(xm2), jnp.where(lane2 < 128, xm2, _sigmoid(xm2)))
    lr = _mm(feat, wlr_ref[...])
    w_raw = prm_ref[0:1, :] + lr[:, 0:GROUP_W]
    logw = -jnp.exp(-_softplus(-w_raw) - 0.5)
    a = _sigmoid(prm_ref[1:2, :] + lr[:, GROUP_W:2 * GROUP_W])
    gb = lr[:, 2 * GROUP_W:3 * GROUP_W]
    kkraw = kb * prm_ref[2:3, :]
    k = kb * (1.0 + (a - 1.0) * prm_ref[3:4, :])

    incl1, _ = _group_masks(rows, n)
    incl, strict = _group_masks(r2, n)
    blk = (_iota2((pw, pw), 0) & -HD_B) == (_iota2((pw, pw), 1) & -HD_B)
    blkf = blk.astype(F32)
    cums = [_mm01(incl1, logw[p * rows:(p + 1) * rows], 3) for p in range(nprob)]

    units = [(p, q) for p in range(nprob) for q in range(npair)]

    def st(x):
        return _stack_heads(x, 2, HD_B)

    pre = []
    for p, q in units:
        rs = slice(p * rows, (p + 1) * rows)
        sl = slice(q * pw, (q + 1) * pw)
        kkr = kkraw[rs, sl]
        rp, kp, vp, ap = r[rs, sl], k[rs, sl], vb[rs, sl], a[rs, sl]
        sums = _mm_x01(jnp.concatenate([kkr * kkr, rp * kp * prm_ref[4:5, sl]], axis=0), blkf, 2)
        kk = kkr * lax.rsqrt(sums[:rows] + 1e-6)
        cump = cums[p][:, sl]
        ginv = jnp.exp(-cump)
        rt = rp * jnp.exp(cump)
        at = -kk * jnp.exp(cump - logw[rs, sl])
        bt = kk * ap * ginv
        kt = kp * ginv
        pre.append(dict(rt=rt, at=at, bt=bt, kt=kt, vp=vp, cump=cump, bonus=sums[rows:] * vp,
                        vst=st(vp), gb=gb[rs, sl]))
    mats = [_mm_nt(jnp.concatenate([st(u["at"]), st(u["rt"])], axis=0),
                   jnp.concatenate([st(u["bt"]), st(u["kt"])], axis=0)) for u in pre]
    tinv = _unit_lower_inv_many([-jnp.where(strict, m[:r2, :r2], 0.0) for m in mats], n)

    u0s, y0s = [], []
    for (p, q), u in zip(units, pre):
        u0_rows, y0_rows = [], []
        for gi in range(groups):
            gs = slice(gi * n, (gi + 1) * n)
            uy = _mm_nt(jnp.concatenate([u["at"][gs], u["rt"][gs]], axis=0), sp_scr[p * groups + gi, q])
            u0_rows.append(uy[:n])
            y0_rows.append(uy[n:])
        u0s.append(jnp.concatenate(u0_rows, axis=0))
        y0s.append(jnp.concatenate(y0_rows, axis=0))
    x1 = [_mm(jnp.where(strict, m[:r2, r2:], 0.0), u["vst"]) for m, u in zip(mats, pre)]
    ust = [_mm(ti, st(u0) + x) for ti, u0, x in zip(tinv, u0s, x1)]
    yst = [_mm(jnp.concatenate([jnp.where(incl, m[r2:, :r2], 0.0), jnp.where(incl, m[r2:, r2:], 0.0)], axis=1),
               jnp.concatenate([us_, u["vst"]], axis=0)) for m, us_, u in zip(mats, ust, pre)]
    ys = [y0 + ys_[:rows] + ys_[rows:] for y0, ys_ in zip(y0s, yst)]
    means = [_mm_x01(y, blkf, 2) * (1.0 / HD_B) for y in ys]
    ycs = [y - m for y, m in zip(ys, means)]
    variances = [_mm_x01(yc * yc, blkf, 2) * (1.0 / HD_B) for yc in ycs]

    for (p, q), u, us_, yc, var in zip(units, pre, ust, ycs, variances):
        sl = slice(q * pw, (q + 1) * pw)
        uu = us_[:rows] + us_[rows:]
        for gi in range(groups):
            gs = slice(gi * n, (gi + 1) * n)
            upd = _mm_tn(jnp.concatenate([uu[gs], u["vp"][gs]], axis=0),
                         jnp.concatenate([u["bt"][gs], u["kt"][gs]], axis=0))
            glast = jnp.exp(u["cump"][gi * n + n - 1:gi * n + n, :])
            sp = sp_scr[p * groups + gi, q]
            sp_scr[p * groups + gi, q] = jnp.where(blk, sp + upd, 0.0) * glast
        yn = yc * lax.rsqrt(var + GN_EPS) * prm_ref[5:6, sl] + prm_ref[6:7, sl]
        val = (yn + u["bonus"]) * u["gb"]
        y_ref[p * groups:(p + 1) * groups, :, sl] = val.reshape(groups, n, pw)

    @pl.when(t == pl.num_programs(1) - 1)
    def _():
        s_ref[...] = sp_scr[...]


def _rwkv_call(proj, carry, s0, mur, muw, wlr, prm, bb, n, groups):
    b, t, _ = proj.shape
    wag_w = 2 * LANES
    npair, pw = H_B // 2, 2 * HD_B
    return pl.pallas_call(
        functools.partial(_rwkv_kernel, groups=groups),
        grid=(b // bb, t // n),
        in_specs=[
            pl.BlockSpec((bb, n, 3 * GROUP_W), lambda i, k: (i, k, EV_RKV // (3 * GROUP_W))),
            pl.BlockSpec((bb, n, wag_w), lambda i, k: (i, k, EV_WAG // wag_w)),
            pl.BlockSpec((bb, SUBLANES, 3 * GROUP_W), lambda i, k: (i, 0, EV_RKV // (3 * GROUP_W))),
            pl.BlockSpec((bb, SUBLANES, wag_w), lambda i, k: (i, 0, EV_WAG // wag_w)),
            pl.BlockSpec((bb, npair, pw, pw), lambda i, k: (i, 0, 0, 0)),
            pl.BlockSpec((1, 3 * GROUP_W), lambda i, k: (0, 0)),
            pl.BlockSpec((1, wag_w), lambda i, k: (0, 0)),
            pl.BlockSpec((wag_w, 3 * GROUP_W), lambda i, k: (0, 0)),
            pl.BlockSpec((SUBLANES, GROUP_W), lambda i, k: (0, 0)),
        ],
        out_specs=[
            pl.BlockSpec((bb, n, GROUP_W), lambda i, k: (i, k, 0)),
            pl.BlockSpec((bb, npair, pw, pw), lambda i, k: (i, 0, 0, 0)),
        ],
        out_shape=[jax.ShapeDtypeStruct((b, t, GROUP_W), F32),
                   jax.ShapeDtypeStruct((b, npair, pw, pw), F32)],
        scratch_shapes=[pltpu.VMEM((bb, n + SUBLANES, 3 * GROUP_W), F32),
                        pltpu.VMEM((bb, n + SUBLANES, wag_w), F32),
                        pltpu.VMEM((bb, npair, pw, pw), F32)],
        compiler_params=_cparams("parallel", "arbitrary"),
        name="rwkv7",
    )(proj, proj, carry, carry, s0, mur, muw, wlr, prm)


def _rwkv_state_to_pairs(s):
    b = s.shape[0]
    s = s.reshape(b, H_B // 2, 2, HD_B, HD_B)
    zero = jnp.zeros_like(s[:, :, 0])
    top = jnp.concatenate([s[:, :, 0], zero], axis=-1)
    bot = jnp.concatenate([zero, s[:, :, 1]], axis=-1)
    return jnp.concatenate([top, bot], axis=-2)


def _rwkv_state_from_pairs(sp):
    b = sp.shape[0]
    heads = jnp.stack([sp[:, :, :HD_B, :HD_B], sp[:, :, HD_B:, HD_B:]], axis=2)
    return heads.reshape(b, H_B, HD_B, HD_B)


def _state_to_pairs(s):
    b = s.shape[0]
    return s.reshape(b, 2, 2, 64, 128).transpose(0, 1, 4, 2, 3).reshape(b, 2, 128, 128)


def _state_from_pairs(sp):
    b = sp.shape[0]
    return sp.reshape(b, 2, 128, 2, 64).transpose(0, 1, 3, 4, 2).reshape(b, 4, 64, 128)


def _stack_pair(x):
    first = (_iota2((1, x.shape[1]), 1) & 64) == 0
    return jnp.concatenate([jnp.where(first, x, 0.0), jnp.where(first, 0.0, x)], axis=0)


def _group_rows(a, rows, n, gi):
    return jnp.concatenate([a[gi * n:(gi + 1) * n], a[rows + gi * n:rows + (gi + 1) * n]], axis=0)


def _ungroup_rows(pieces, n):
    return jnp.concatenate([p[:n] for p in pieces] + [p[n:] for p in pieces], axis=0)


def _gla_kernel(qk_ref, v_ref, z_ref, gkin_ref, s0_ref, wgk_ref, bgk_ref, gain_ref,
                o_ref, s_ref, s_scr, *, groups):
    t = pl.program_id(1)
    bb, n, _ = qk_ref.shape
    rows = groups * n
    nprob = bb // groups
    npair = H_C // 2
    pw = 2 * DK_C
    qkw = H_C * DK_C
    r2 = 2 * rows
    sub = min(GLA_SUB, n)
    nslab = n // sub
    assert groups == 1 or nslab == 1

    @pl.when(t == 0)
    def _():
        s_scr[...] = s0_ref[...]

    qk = qk_ref[...].reshape(bb * n, 2 * qkw)
    q = qk[:, :qkw] * (DK_C ** -0.5)
    k = qk[:, qkw:]
    v = v_ref[...].reshape(bb * n, GROUP_W)
    z = z_ref[...].reshape(bb * n, GROUP_W)
    gk = _log_sigmoid(_mm(gkin_ref[...].reshape(bb * n, LANES), wgk_ref[...]) + bgk_ref[...]) * (1.0 / GLA_NORM)
    incl1, _ = _group_masks(rows, n)
    incl2, _ = _group_masks(r2, n)
    row_t = _iota2((rows, 1), 0) & (n - 1)
    cums = [_mm01(incl1, gk[p * rows:(p + 1) * rows], 3) for p in range(nprob)]
    units = [(p, u) for p in range(nprob) for u in range(npair)]

    pre = []
    for p, u in units:
        rs = slice(p * rows, (p + 1) * rows)
        sl = slice(u * pw, (u + 1) * pw)
        bcum, qp, kp = cums[p][:, sl], q[rs, sl], k[rs, sl]
        qparts, kparts = [], []
        for s in range(nslab):
            rho = bcum[s * sub - 1:s * sub, :] if s > 0 else jnp.zeros((1, pw), F32)
            in_slab = jnp.logical_and(row_t >= s * sub, row_t < (s + 1) * sub)
            qparts.append(qp * jnp.exp(jnp.where(in_slab, bcum - rho, -jnp.inf)))
            kparts.append(kp * jnp.exp(jnp.where(row_t < (s + 1) * sub, rho - bcum, -jnp.inf)))
        blast = jnp.concatenate(
            [jnp.broadcast_to(bcum[gi * n + n - 1:gi * n + n, :], (n, pw)) for gi in range(groups)], axis=0)
        vst = jnp.concatenate([v[rs, (2 * u + j) * DV_C:(2 * u + j + 1) * DV_C] for j in range(2)], axis=0)
        pre.append(dict(qcat=_stack_pair(jnp.concatenate(qparts, axis=1)),
                        kcat=_stack_pair(jnp.concatenate(kparts, axis=1)),
                        qdb=_stack_pair(qp * jnp.exp(bcum)), kd=_stack_pair(kp * jnp.exp(blast - bcum)),
                        vst=vst, bcum=bcum))
    att = [jnp.where(incl2, _mm_nt(u["qcat"], u["kcat"]), 0.0) for u in pre]
    intra = [_mm(a, u["vst"]) for a, u in zip(att, pre)]
    inter = []
    for (p, uidx), u in zip(units, pre):
        pieces = [_mm_nt(_group_rows(u["qdb"], rows, n, gi), s_scr[p * groups + gi, uidx])
                  for gi in range(groups)]
        inter.append(_ungroup_rows(pieces, n))
    for (p, uidx), u, o_in, o_x in zip(units, pre, intra, inter):
        for gi in range(groups):
            glast = jnp.exp(u["bcum"][gi * n + n - 1:gi * n + n, :])
            upd = _mm_tn(_group_rows(u["vst"], rows, n, gi), _group_rows(u["kd"], rows, n, gi))
            s_scr[p * groups + gi, uidx] = s_scr[p * groups + gi, uidx] * glast + upd
        o = o_in + o_x
        for j in range(2):
            h = 2 * uidx + j
            zh = z[p * rows:(p + 1) * rows, h * DV_C:(h + 1) * DV_C]
            val = _rms(o[j * rows:(j + 1) * rows]) * gain_ref[...] * _silu(zh)
            o_ref[p * groups:(p + 1) * groups, :, h * DV_C:(h + 1) * DV_C] = val.reshape(groups, n, DV_C)

    @pl.when(t == pl.num_programs(1) - 1)
    def _():
        s_ref[...] = s_scr[...]


def _gla_call(proj, s0, wgk, bgk, gain, bb, n, groups):
    b, t, _ = proj.shape
    npair, pw = H_C // 2, 2 * DK_C
    return pl.pallas_call(
        functools.partial(_gla_kernel, groups=groups),
        grid=(b // bb, t // n),
        in_specs=[
            pl.BlockSpec((bb, n, GROUP_W), lambda i, k: (i, k, OD_CQK // GROUP_W)),
            pl.BlockSpec((bb, n, GROUP_W), lambda i, k: (i, k, OD_CV // GROUP_W)),
            pl.BlockSpec((bb, n, GROUP_W), lambda i, k: (i, k, OD_CZ // GROUP_W)),
            pl.BlockSpec((bb, n, LANES), lambda i, k: (i, k, OD_CGK // LANES)),
            pl.BlockSpec((bb, npair, DV_C, pw), lambda i, k: (i, 0, 0, 0)),
            pl.BlockSpec((LANES, H_C * DK_C), lambda i, k: (0, 0)),
            pl.BlockSpec((1, H_C * DK_C), lambda i, k: (0, 0)),
            pl.BlockSpec((1, DV_C), lambda i, k: (0, 0)),
        ],
        out_specs=[
            pl.BlockSpec((bb, n, GROUP_W), lambda i, k: (i, k, 0)),
            pl.BlockSpec((bb, npair, DV_C, pw), lambda i, k: (i, 0, 0, 0)),
        ],
        out_shape=[jax.ShapeDtypeStruct((b, t, GROUP_W), F32),
                   jax.ShapeDtypeStruct((b, npair, DV_C, pw), F32)],
        scratch_shapes=[pltpu.VMEM((bb, npair, DV_C, pw), F32)],
        compiler_params=_cparams("parallel", "arbitrary"),
        name="gla",
    )(proj, proj, proj, proj, s0, wgk, bgk, gain)


def _mlstm_kernel(qk_ref, v_ref, og_ref, if_ref, c0_ref, n0_ref, m0_ref, bif_ref, gain_ref,
                  h_ref, c_ref, nn_ref, m_ref, c_scr, n_scr, m_scr, *, groups):
    t = pl.program_id(1)
    bb, n, _ = qk_ref.shape
    rows = groups * n
    nprob = bb // groups
    npair = H_D // 2
    pw = 2 * DK_D
    qkw = H_D * DK_D
    r2 = 2 * rows

    @pl.when(t == 0)
    def _():
        c_scr[...] = c0_ref[...]
        n_scr[...] = n0_ref[...]
        m_scr[...] = m0_ref[...]

    x = if_ref[...].reshape(bb * n, LANES) + bif_ref[...]
    lf = _log_sigmoid(x)
    qk = qk_ref[...].reshape(bb * n, 2 * qkw)
    q = qk[:, :qkw]
    k = qk[:, qkw:] * (DK_D ** -0.5)
    v = v_ref[...].reshape(bb * n, GROUP_W)
    og = og_ref[...].reshape(bb * n, GROUP_W)
    incl1, _ = _group_masks(rows, n)
    incl2, _ = _group_masks(r2, n)
    r_i, c_i = _iota2((r2, r2), 0), _iota2((r2, r2), 1)
    same2 = (r_i & -n) == (c_i & -n)
    last2 = jnp.logical_and(same2, (c_i & (n - 1)) == n - 1)
    lane = _iota2((1, LANES), 1)
    ones = jnp.ones((r2, LANES), F32)
    fcums = [_mm01(incl1, lf[p * rows:(p + 1) * rows], 3) for p in range(nprob)]
    m_rows = [m_scr[b_] for b_ in range(bb)]
    m_old = list(m_rows)
    units = [(p, u) for p in range(nprob) for u in range(npair)]

    def per_block(fn):
        return jnp.concatenate([jnp.broadcast_to(fn(j, gi), (n, fn(j, gi).shape[1]))
                                for j in range(2) for gi in range(groups)], axis=0)

    pre = []
    for p, u in units:
        rs = slice(p * rows, (p + 1) * rows)
        sl = slice(u * pw, (u + 1) * pw)
        fsel = jnp.concatenate([jnp.where(lane == H_D + 2 * u + j, fcums[p], 0.0) for j in range(2)], axis=0)
        isel = jnp.concatenate([jnp.where(lane == 2 * u + j, x[rs], 0.0) for j in range(2)], axis=0)
        fcol = jnp.sum(fsel, axis=-1, keepdims=True)
        icol = jnp.sum(isel, axis=-1, keepdims=True)
        drow = _mm_nt01(ones, isel - fsel, 3)
        flast = jnp.sum(_mm01(last2, fsel, 3), axis=-1, keepdims=True)
        mprev = per_block(lambda j, gi: m_old[p * groups + gi][:, 2 * u + j:2 * u + j + 1])
        nmat = per_block(lambda j, gi: n_scr[p * groups + gi][:, sl])
        log_d = jnp.where(incl2, fcol + drow, -jnp.inf)
        m_in = jnp.max(log_d, axis=-1, keepdims=True)
        m_e = jnp.max(jnp.where(same2, flast + drow, -jnp.inf), axis=-1, keepdims=True)
        m_t = jnp.maximum(fcol + mprev, m_in)
        w_in = jnp.exp(fcol + mprev - m_t)
        m_new = jnp.maximum(flast + mprev, m_e)
        qst, kst = _stack_pair(q[rs, sl]), _stack_pair(k[rs, sl])
        vst = jnp.concatenate([v[rs, (2 * u + j) * DV_D:(2 * u + j + 1) * DV_D] for j in range(2)], axis=0)
        pre.append(dict(qst=qst, kst=kst, vst=vst, log_d=log_d, m_t=m_t, w_in=w_in, m_new=m_new,
                        cd=jnp.exp(flast + mprev - m_new), nmat=nmat,
                        ke=kst * jnp.exp(flast - fcol + icol - m_new)))
    dms = [jnp.exp(u["log_d"] - u["m_t"]) * _mm_nt(u["qst"], u["kst"]) for u in pre]
    intra = [_mm(dm, u["vst"]) for dm, u in zip(dms, pre)]
    inter = []
    for (p, uidx), u in zip(units, pre):
        qw = u["w_in"] * u["qst"]
        pieces = [_mm_nt(_group_rows(qw, rows, n, gi), c_scr[p * groups + gi, uidx]) for gi in range(groups)]
        inter.append(_ungroup_rows(pieces, n))
    for (p, uidx), u, dm, o_in, o_x in zip(units, pre, dms, intra, inter):
        sl = slice(uidx * pw, (uidx + 1) * pw)
        den = (u["w_in"] * jnp.sum(u["qst"] * u["nmat"], axis=-1, keepdims=True)
               + jnp.sum(dm, axis=-1, keepdims=True))
        hout = (o_in + o_x) / jnp.maximum(jnp.abs(den), jnp.exp(-u["m_t"]))
        for gi in range(groups):
            b_ = p * groups + gi
            r0, r1 = gi * n, rows + gi * n
            cdrow = jnp.where(lane < DK_D, u["cd"][r0:r0 + 1], u["cd"][r1:r1 + 1])
            ke_g = _group_rows(u["ke"], rows, n, gi)
            c_scr[b_, uidx] = c_scr[b_, uidx] * cdrow + _mm_tn(_group_rows(u["vst"], rows, n, gi), ke_g)
            n_scr[b_, :, sl] = n_scr[b_][:, sl] * cdrow + jnp.sum(ke_g, axis=0, keepdims=True)
            for j, r in ((0, r0), (1, r1)):
                m_rows[b_] = jnp.where(lane == 2 * uidx + j, u["m_new"][r:r + 1], m_rows[b_])
        for j in range(2):
            h = 2 * uidx + j
            ogh = og[p * rows:(p + 1) * rows, h * DV_D:(h + 1) * DV_D]
            val = _sigmoid(ogh) * (_rms(hout[j * rows:(j + 1) * rows]) * gain_ref[...])
            h_ref[p * groups:(p + 1) * groups, :, h * DV_D:(h + 1) * DV_D] = val.reshape(groups, n, DV_D)
    for b_ in range(bb):
        m_scr[b_] = m_rows[b_]

    @pl.when(t == pl.num_programs(1) - 1)
    def _():
        c_ref[...] = c_scr[...]
        nn_ref[...] = n_scr[...]
        m_ref[...] = m_scr[...]


def _mlstm_call(proj, c0, n0, m0, bif, gain, bb, n, groups):
    b, t, _ = proj.shape
    qkw = H_D * DK_D
    npair, pw = H_D // 2, 2 * DK_D
    return pl.pallas_call(
        functools.partial(_mlstm_kernel, groups=groups),
        grid=(b // bb, t // n),
        in_specs=[
            pl.BlockSpec((bb, n, GROUP_W), lambda i, k: (i, k, OD_DQK // GROUP_W)),
            pl.BlockSpec((bb, n, GROUP_W), lambda i, k: (i, k, OD_DV // GROUP_W)),
            pl.BlockSpec((bb, n, GROUP_W), lambda i, k: (i, k, OD_DO // GROUP_W)),
            pl.BlockSpec((bb, n, LANES), lambda i, k: (i, k, OD_DIF // LANES)),
            pl.BlockSpec((bb, npair, DV_D, pw), lambda i, k: (i, 0, 0, 0)),
            pl.BlockSpec((bb, 1, qkw), lambda i, k: (i, 0, 0)),
            pl.BlockSpec((bb, 1, LANES), lambda i, k: (i, 0, 0)),
            pl.BlockSpec((1, LANES), lambda i, k: (0, 0)),
            pl.BlockSpec((1, DV_D), lambda i, k: (0, 0)),
        ],
        out_specs=[
            pl.BlockSpec((bb, n, GROUP_W), lambda i, k: (i, k, 0)),
            pl.BlockSpec((bb, npair, DV_D, pw), lambda i, k: (i, 0, 0, 0)),
            pl.BlockSpec((bb, 1, qkw), lambda i, k: (i, 0, 0)),
            pl.BlockSpec((bb, 1, LANES), lambda i, k: (i, 0, 0)),
        ],
        out_shape=[jax.ShapeDtypeStruct((b, t, GROUP_W), F32),
                   jax.ShapeDtypeStruct((b, npair, DV_D, pw), F32),
                   jax.ShapeDtypeStruct((b, 1, qkw), F32),
                   jax.ShapeDtypeStruct((b, 1, LANES), F32)],
        scratch_shapes=[pltpu.VMEM((bb, npair, DV_D, pw), F32),
                        pltpu.VMEM((bb, 1, qkw), F32),
                        pltpu.VMEM((bb, 1, LANES), F32)],
        compiler_params=_cparams("parallel", "arbitrary"),
        name="mlstm",
    )(proj, proj, proj, proj, c0, n0, m0, bif, gain)


def _pad_cols(w, width):
    return jnp.pad(w, ((0, 0), (0, width - w.shape[1])))


def _pad_rows(w, rows, at=0):
    return jnp.pad(w, ((at, rows - at - w.shape[0]), (0, 0)))


def _even_in_weight(w):
    pa, pb = w[:, :2056], w[:, 2056:]
    cols = [
        pb[:, 0:1536],
        pa[:, 0:1536],
        pa[:, 1544:2056],
        pb[:, 1536:1792],
        _pad_cols(pa[:, 1536:1544], LANES),
    ]
    return _pad_cols(jnp.concatenate(cols, axis=1), EV_COLS).astype(BF16)


def _odd_in_weight(w):
    pc, pd = w[:, :1552], w[:, 1552:]
    cols = [
        pc[:, 0:512],
        pd[:, 0:512],
        pc[:, 512:1024],
        pc[:, 1040:1552],
        pd[:, 512:1024],
        pd[:, 1032:1544],
        _pad_cols(pc[:, 1024:1040], LANES),
        _pad_cols(pd[:, 1024:1032], LANES),
    ]
    return jnp.concatenate(cols, axis=1).astype(BF16)


def _row(v):
    return v.reshape(1, -1).astype(F32)


def _tiles(x):
    b, t, _ = x.shape
    tb = min(t, 1024)
    bb = min(b, 1024 // tb)
    return bb, tb


def _trunk(x, mod, states, wts):
    xbuf, s_delta, s_rwkv, s_gla, s_mc, s_mn, s_mm = states
    b, t, _ = x.shape
    bb, tb = _tiles(x)
    n = min(CHUNK, t)
    groups = CHUNK // n
    rbb = (PROBLEMS_LONG if t > CHUNK else PROBLEMS) * groups
    new_even = ([], [], [])
    new_odd = ([], [], [], [])
    for l in range(DEPTH):
        lw = wts["layers"][l]
        m_l = mod[l]
        x = _ffn_call(x, m_l, lw["gain0"], lw["wg0"], lw["wu0"], lw["wd0"], 0, bb, tb, 256)
        i = l // 2
        if l % 2 == 0:
            h, proj = _adaln_proj_call(x, m_l, lw["gain1"], lw["w_in"], bb, tb, 1024)
            if xbuf is None:
                carry = jnp.zeros((b, SUBLANES, EV_COLS), F32)
            else:
                rows = _rows_proj_call(xbuf[i].reshape(b * (CONV_W - 1), D_MODEL), lw["w_in"], 1024)
                carry = jnp.pad(rows.reshape(b, CONV_W - 1, EV_COLS),
                                ((0, 0), (SUBLANES - CONV_W + 1, 0), (0, 0)))
            oa, sd = _delta_call(proj, carry, s_delta[i], lw["conv_w"], lw["delta_hp"], lw["gain_a"],
                                 rbb, n, groups)
            ob, sr = _rwkv_call(proj, carry, _rwkv_state_to_pairs(s_rwkv[i]), lw["mu_rkv"], lw["mu_wag"],
                                lw["w_lora"], lw["rwkv_prm"], rbb, n, groups)
            for lst, val in zip(new_even, (h[:, t - (CONV_W - 1):], sd, _rwkv_state_from_pairs(sr))):
                lst.append(val)
        else:
            _, proj = _adaln_proj_call(x, m_l, lw["gain1"], lw["w_in"], bb, tb, 1664)
            oa, sg = _gla_call(proj, _state_to_pairs(s_gla[i]), lw["w_gk2"], lw["b_gk"], lw["gain_c"],
                               rbb, n, groups)
            ob, sc, sn, sm = _mlstm_call(proj, _state_to_pairs(s_mc[i]), s_mn[i].reshape(b, 1, H_D * DK_D),
                                         _pad_cols(s_mm[i], LANES).reshape(b, 1, LANES),
                                         lw["b_if"], lw["gain_d"], rbb, n, groups)
            for lst, val in zip(new_odd, (_state_from_pairs(sg), _state_from_pairs(sc),
                                          sn.reshape(b, H_D, DK_D), sm[:, 0, :H_D])):
                lst.append(val)
        x = _outproj_call(x, oa, ob, m_l, lw["w_out"], bb, tb)
        x = _ffn_call(x, m_l, lw["gain2"], lw["wg1"], lw["wu1"], lw["wd1"], 6, bb, tb, 256)
    y = _final_norm_call(x, wts["final_gain"], bb, tb)
    stacked = [jnp.stack(lst) for lst in new_even + new_odd]
    return (y, *stacked)


def _prepare_weights(norm_gain, final_gain, w_ffn_gate, w_ffn_up, w_ffn_down, w_in_even, w_out_even,
                     conv_w, a_log, dt_bias, gain_a, mu_b, w0_b, w_w2, a0_b, w_a2, w_g2, k_k, k_a, r_k,
                     lnx_gain, lnx_bias, w_in_odd, w_out_odd, w_gk2, b_gk, gain_c, b_i, b_f, gain_d):
    layers = []
    for l in range(DEPTH):
        i = l // 2
        lw = {
            "gain0": _row(norm_gain[l, 0]), "gain1": _row(norm_gain[l, 1]), "gain2": _row(norm_gain[l, 2]),
            "wg0": w_ffn_gate[l, 0].astype(BF16), "wu0": w_ffn_up[l, 0].astype(BF16),
            "wd0": w_ffn_down[l, 0].astype(BF16),
            "wg1": w_ffn_gate[l, 1].astype(BF16), "wu1": w_ffn_up[l, 1].astype(BF16),
            "wd1": w_ffn_down[l, 1].astype(BF16),
        }
        if l % 2 == 0:
            lw["w_in"] = _even_in_weight(w_in_even[i])
            lw["w_out"] = w_out_even[i].astype(BF16)
            lw["conv_w"] = _pad_rows(conv_w[i].astype(F32), SUBLANES)
            lw["delta_hp"] = _pad_rows(jnp.stack([_pad_cols(_row(a_log[i]), LANES)[0],
                                                  _pad_cols(_row(dt_bias[i]), LANES)[0]]), SUBLANES)
            lw["gain_a"] = _row(gain_a[i])
            lw["mu_rkv"] = _row(mu_b[i, :1536])
            lw["mu_wag"] = _row(mu_b[i, 1536:])
            lora = jnp.zeros((2 * LANES, 3 * GROUP_W), F32)
            lora = lora.at[0:64, 0:GROUP_W].set(w_w2[i])
            lora = lora.at[64:128, GROUP_W:2 * GROUP_W].set(w_a2[i])
            lora = lora.at[128:256, 2 * GROUP_W:].set(w_g2[i])
            lw["w_lora"] = lora.astype(BF16)
            lw["rwkv_prm"] = jnp.stack([w0_b[i], a0_b[i], k_k[i], k_a[i], r_k[i].reshape(-1),
                                        lnx_gain[i], lnx_bias[i], jnp.zeros_like(w0_b[i])]).astype(F32)
        else:
            lw["w_in"] = _odd_in_weight(w_in_odd[i])
            lw["w_out"] = w_out_odd[i].astype(BF16)
            lw["w_gk2"] = _pad_rows(w_gk2[i], LANES).astype(BF16)
            lw["b_gk"] = _row(b_gk[i])
            lw["gain_c"] = _row(gain_c[i])
            lw["b_if"] = _pad_cols(_row(jnp.concatenate([b_i[i], b_f[i]])), LANES)
            lw["gain_d"] = _row(gain_d[i])
        layers.append(lw)
    return {"layers": layers, "final_gain": _row(final_gain)}


def kernel(x_prompt, x_sample, c_prompt, c_sample, state_xbuf_even, state_delta, state_rwkv, state_gla,
           state_mlstm_c, state_mlstm_n, state_mlstm_m, w_mod, b_mod, norm_gain, final_gain, w_ffn_gate,
           w_ffn_up, w_ffn_down, w_in_even, w_out_even, conv_w, a_log, dt_bias, gain_a, mu_b, w0_b, w_w2,
           a0_b, w_a2, w_g2, k_k, k_a, r_k, lnx_gain, lnx_bias, w_in_odd, w_out_odd, w_gk2, b_gk, gain_c,
           b_i, b_f, gain_d):
    wts = _prepare_weights(norm_gain, final_gain, w_ffn_gate, w_ffn_up, w_ffn_down, w_in_even, w_out_even,
                           conv_w, a_log, dt_bias, gain_a, mu_b, w0_b, w_w2, a0_b, w_a2, w_g2, k_k, k_a,
                           r_k, lnx_gain, lnx_bias, w_in_odd, w_out_odd, w_gk2, b_gk, gain_c, b_i, b_f,
                           gain_d)
    bp, bs = x_prompt.shape[0], x_sample.shape[0]
    c_all = jnp.concatenate([c_prompt, c_sample], axis=0).astype(F32)
    mod = _mod_call(c_all, w_mod, b_mod).reshape(DEPTH, bp + bs, N_MOD, 1, D_MODEL)
    mod_p, mod_s = mod[:, :bp], mod[:, bp:]

    def zeros(shape):
        return jnp.zeros(shape, F32)

    zero_states = (
        None,
        zeros((N_EVEN, bp, H_A, DK_A, DK_A)),
        zeros((N_EVEN, bp, H_B, HD_B, HD_B)),
        zeros((N_ODD, bp, H_C, DK_C, DV_C)),
        zeros((N_ODD, bp, H_D, DK_D, DV_D)),
        zeros((N_ODD, bp, H_D, DK_D)),
        zeros((N_ODD, bp, H_D)),
    )
    y_p, xb_p, dl_p, rw_p, gl_p, mc_p, mn_p, mm_p = _trunk(x_prompt, mod_p, zero_states, wts)
    sample_states = (state_xbuf_even, state_delta, state_rwkv, state_gla,
                     state_mlstm_c, state_mlstm_n, state_mlstm_m)
    y_s, xb_s, dl_s, rw_s, gl_s, mc_s, mn_s, mm_s = _trunk(x_sample, mod_s, sample_states, wts)
    return (y_p, y_s, xb_p, xb_s, dl_p, dl_s, rw_p, rw_s, gl_p, gl_s, mc_p, mc_s, mn_p, mn_s, mm_p, mm_s)
```

```python
import functools

import jax
import jax.numpy as jnp
from jax import lax
from jax.experimental import pallas as pl
from jax.experimental.pallas import tpu as pltpu

F32 = jnp.float32
BF16 = jnp.bfloat16
HIGHEST = lax.Precision.HIGHEST

D_MODEL = 1024
DEPTH = 4
N_EVEN = 2
N_ODD = 2
D_FF = 2816
N_MOD = 9
EPS = 1e-6
GN_EPS = 64e-5
CONV_W = 4
H_A, DK_A = 4, 128
H_B, HD_B = 8, 64
H_C, DK_C, DV_C = 4, 64, 128
H_D, DK_D, DV_D = 4, 64, 128
GLA_NORM = 16.0
GROUP_W = 512

LANES = 128
SUBLANES = 8
BF16_ROWS = 16
VMEM_LIMIT_BYTES = 48 * 1024 * 1024

EV_RKV, EV_QKV, EV_Z, EV_WAG, EV_AB, EV_COLS = 0, 1536, 3072, 3584, 3840, 4096
OD_CQK, OD_DQK, OD_CV, OD_CZ, OD_DV, OD_DO, OD_CGK, OD_DIF, OD_COLS = (
    0, 512, 1024, 1536, 2048, 2560, 3072, 3200, 3328)

CHUNK = 64
GLA_SUB = 16
PROBLEMS = 2
PROBLEMS_LONG = 8


def _bf(x):
    if x.dtype == BF16:
        return x
    if x.shape[-2] % BF16_ROWS == 0 and x.shape[-1] % BF16_ROWS == 0:
        return x.astype(BF16)
    return x


def _pair(a, b):
    a, b = _bf(a), _bf(b)
    if a.dtype != b.dtype:
        a, b = a.astype(F32), b.astype(F32)
    return a, b


def _mm(a, b):
    a, b = _pair(a, b)
    return jnp.dot(a, b, preferred_element_type=F32)


def _mm_nt(a, b):
    a, b = _pair(a, b)
    return lax.dot_general(a, b, (((1,), (1,)), ((), ())), preferred_element_type=F32)


def _mm_tn(a, b):
    a, b = _pair(a, b)
    return lax.dot_general(a, b, (((0,), (0,)), ((), ())), preferred_element_type=F32)


def _mm_f32(a, b):
    return jnp.dot(a, b, precision=HIGHEST, preferred_element_type=F32)


def _mm_nt_f32(a, b):
    return lax.dot_general(a, b, (((1,), (1,)), ((), ())), precision=HIGHEST,
                           preferred_element_type=F32)


def _split_bf16(x, parts):
    out, r = [], x
    for i in range(parts):
        p = r.astype(BF16)
        out.append(p)
        if i + 1 < parts:
            r = r - p.astype(F32)
    return out


def _mm01(a01, x, parts):
    a = a01.astype(BF16)
    acc = None
    for p in _split_bf16(x, parts):
        d = jnp.dot(a, p, preferred_element_type=F32)
        acc = d if acc is None else acc + d
    return acc


def _mm_x01(x, b01, parts):
    b = b01.astype(BF16)
    acc = None
    for p in _split_bf16(x, parts):
        d = jnp.dot(p, b, preferred_element_type=F32)
        acc = d if acc is None else acc + d
    return acc


def _mm_nt01(a01, x, parts):
    a = a01.astype(BF16)
    acc = None
    for p in _split_bf16(x, parts):
        d = lax.dot_general(a, p, (((1,), (1,)), ((), ())), preferred_element_type=F32)
        acc = d if acc is None else acc + d
    return acc


def _sigmoid(x):
    return jax.nn.sigmoid(x)


def _silu(x):
    return x * jax.nn.sigmoid(x)


def _softplus(x):
    return jnp.maximum(x, 0.0) + jnp.log1p(jnp.exp(-jnp.abs(x)))


def _log_sigmoid(x):
    return -_softplus(-x)


def _rms(x, eps=EPS):
    return x * lax.rsqrt(jnp.mean(x * x, axis=-1, keepdims=True) + eps)


def _l2n(x):
    return x * lax.rsqrt(jnp.sum(x * x, axis=-1, keepdims=True) + 1e-6)


def _iota2(shape, dim):
    return lax.broadcasted_iota(jnp.int32, shape, dim)


def _tril_masks(n):
    r, c = _iota2((n, n), 0), _iota2((n, n), 1)
    return r >= c, r > c


def _group_masks(size, n):
    r, c = _iota2((size, size), 0), _iota2((size, size), 1)
    same = (r & -n) == (c & -n)
    return jnp.logical_and(same, r >= c), jnp.logical_and(same, r > c)


def _eye(n):
    return (_iota2((n, n), 0) == _iota2((n, n), 1)).astype(F32)


def _transpose_rows(x):
    return _mm_nt_f32(_eye(x.shape[1]), x)


def _unit_lower_inv_many(mats, n):
    eye = _eye(mats[0].shape[0])
    ms = [-a for a in mats]
    ps = [eye + m for m in ms]
    covered = 2
    while covered < n:
        ms = [_mm(m, m) for m in ms]
        ps = [p + _mm(p, m) for p, m in zip(ps, ms)]
        covered *= 2
    return ps


def _cparams(*sem):
    return pltpu.CompilerParams(dimension_semantics=sem, vmem_limit_bytes=VMEM_LIMIT_BYTES)


def _mod_kernel(c_ref, w_ref, b_ref, o_ref):
    cs = _silu(c_ref[...])
    o_ref[...] = _mm(cs, w_ref[...]) + b_ref[...]


def _mod_call(c_all, w_mod, b_mod):
    rows = c_all.shape[0]
    tn = 1024
    width = N_MOD * D_MODEL
    return pl.pallas_call(
        _mod_kernel,
        grid=(DEPTH, width // tn),
        in_specs=[
            pl.BlockSpec((rows, D_MODEL), lambda l, j: (0, 0)),
            pl.BlockSpec((None, D_MODEL, tn), lambda l, j: (l, 0, j)),
            pl.BlockSpec((None, 1, tn), lambda l, j: (l, 0, j)),
        ],
        out_specs=pl.BlockSpec((None, rows, tn), lambda l, j: (l, 0, j)),
        out_shape=jax.ShapeDtypeStruct((DEPTH, rows, width), F32),
        compiler_params=_cparams("parallel", "parallel"),
        name="mod",
    )(c_all, w_mod, b_mod.reshape(DEPTH, 1, width))


def _adaln(x, gain, scale, shift):
    return _rms(x) * gain * (1.0 + scale) + shift


def _ffn_kernel(x_ref, sh_ref, sc_ref, gt_ref, gain_ref, wg_ref, wu_ref, wd_ref, fgain_ref, o_ref,
                h_scr, acc_scr, *, final_norm):
    f = pl.program_id(2)
    bb, tb, d = x_ref.shape

    @pl.when(f == 0)
    def _():
        h = _adaln(x_ref[...], gain_ref[...], sc_ref[...], sh_ref[...])
        h_scr[...] = h.reshape(bb * tb, d).astype(BF16)
        acc_scr[...] = jnp.zeros_like(acc_scr)

    h = h_scr[...]
    g = jnp.dot(h, wg_ref[...], preferred_element_type=F32)
    u = jnp.dot(h, wu_ref[...], preferred_element_type=F32)
    a = (_silu(g) * u).astype(BF16)
    acc_scr[...] += jnp.dot(a, wd_ref[...], preferred_element_type=F32)

    @pl.when(f == pl.num_programs(2) - 1)
    def _():
        y = acc_scr[...].reshape(bb, tb, d)
        out = x_ref[...] + 0.5 * (1.0 + gt_ref[...]) * y
        if final_norm:
            out = _rms(out) * fgain_ref[...]
        o_ref[...] = out


def _mod_spec(bb, j, ngrid):
    if ngrid == 3:
        return pl.BlockSpec((bb, None, 1, D_MODEL), lambda b, t, f: (b, j, 0, 0))
    return pl.BlockSpec((bb, None, 1, D_MODEL), lambda b, t: (b, j, 0, 0))


def _ffn_call(x, mod, gain, wg, wu, wd, fgain, j0, bb, tb, tf, final_norm):
    b, t, d = x.shape
    xspec = pl.BlockSpec((bb, tb, d), lambda i, k, f: (i, k, 0))
    rowspec = pl.BlockSpec((1, d), lambda i, k, f: (0, 0))
    return pl.pallas_call(
        functools.partial(_ffn_kernel, final_norm=final_norm),
        grid=(b // bb, t // tb, D_FF // tf),
        in_specs=[
            xspec,
            _mod_spec(bb, j0, 3), _mod_spec(bb, j0 + 1, 3), _mod_spec(bb, j0 + 2, 3),
            rowspec,
            pl.BlockSpec((d, tf), lambda i, k, f: (0, f)),
            pl.BlockSpec((d, tf), lambda i, k, f: (0, f)),
            pl.BlockSpec((tf, d), lambda i, k, f: (f, 0)),
            rowspec,
        ],
        out_specs=xspec,
        out_shape=jax.ShapeDtypeStruct(x.shape, x.dtype),
        scratch_shapes=[pltpu.VMEM((bb * tb, d), BF16), pltpu.VMEM((bb * tb, d), F32)],
        compiler_params=_cparams("parallel", "parallel", "arbitrary"),
        name="ffn",
    )(x, mod, mod, mod, gain, wg, wu, wd, fgain)


def _adaln_proj_kernel(x_ref, sh_ref, sc_ref, gain_ref, w_ref, *refs, tail):
    p_ref, hb_scr = refs[-2:]
    j = pl.program_id(2)
    bb, tb, d = x_ref.shape

    @pl.when(j == 0)
    def _():
        h = _adaln(x_ref[...], gain_ref[...], sc_ref[...], sh_ref[...])
        if tail:
            refs[0][...] = h[:, tb - SUBLANES:, :]
        hb_scr[...] = h.reshape(bb * tb, d).astype(BF16)

    p = jnp.dot(hb_scr[...], w_ref[...], preferred_element_type=F32)
    p_ref[...] = p.reshape(bb, tb, p.shape[-1])


def _adaln_proj_call(x, mod, gain, w, bb, tb, tn, tail):
    b, t, d = x.shape
    n = w.shape[1]
    xspec = pl.BlockSpec((bb, tb, d), lambda i, k, j: (i, k, 0))
    out_specs = [pl.BlockSpec((bb, tb, tn), lambda i, k, j: (i, k, j))]
    out_shape = [jax.ShapeDtypeStruct((b, t, n), F32)]
    if tail:
        out_specs.insert(0, pl.BlockSpec((bb, SUBLANES, d), lambda i, k, j: (i, 0, 0)))
        out_shape.insert(0, jax.ShapeDtypeStruct((b, SUBLANES, d), F32))
    return pl.pallas_call(
        functools.partial(_adaln_proj_kernel, tail=tail),
        grid=(b // bb, t // tb, n // tn),
        in_specs=[
            xspec, _mod_spec(bb, 3, 3), _mod_spec(bb, 4, 3),
            pl.BlockSpec((1, d), lambda i, k, j: (0, 0)),
            pl.BlockSpec((d, tn), lambda i, k, j: (0, j)),
        ],
        out_specs=out_specs,
        out_shape=out_shape,
        scratch_shapes=[pltpu.VMEM((bb * tb, d), BF16)],
        compiler_params=_cparams("parallel", "arbitrary", "arbitrary"),
        name="adaln_proj",
    )(x, mod, mod, gain, w)


def _rows_proj_kernel(a_ref, w_ref, o_ref):
    o_ref[...] = _mm(a_ref[...], w_ref[...])


def _rows_proj_call(a, w, tn):
    m, k = a.shape
    n = w.shape[1]
    return pl.pallas_call(
        _rows_proj_kernel,
        grid=(n // tn,),
        in_specs=[pl.BlockSpec((m, k), lambda j: (0, 0)), pl.BlockSpec((k, tn), lambda j: (0, j))],
        out_specs=pl.BlockSpec((m, tn), lambda j: (0, j)),
        out_shape=jax.ShapeDtypeStruct((m, n), F32),
        compiler_params=_cparams("parallel"),
        name="rows_proj",
    )(a, w)


def _outproj_kernel(x_ref, oa_ref, ob_ref, gt_ref, w_ref, o_ref):
    bb, tb, d = x_ref.shape
    o = jnp.concatenate([oa_ref[...], ob_ref[...]], axis=-1).reshape(bb * tb, d)
    y = jnp.dot(o.astype(BF16), w_ref[...], preferred_element_type=F32).reshape(bb, tb, d)
    o_ref[...] = x_ref[...] + (1.0 + gt_ref[...]) * y


def _outproj_call(x, oa, ob, mod, w, bb, tb):
    b, t, d = x.shape
    xspec = pl.BlockSpec((bb, tb, d), lambda i, k: (i, k, 0))
    hspec = pl.BlockSpec((bb, tb, GROUP_W), lambda i, k: (i, k, 0))
    return pl.pallas_call(
        _outproj_kernel,
        grid=(b // bb, t // tb),
        in_specs=[xspec, hspec, hspec, _mod_spec(bb, 5, 2),
                  pl.BlockSpec((d, d), lambda i, k: (0, 0))],
        out_specs=xspec,
        out_shape=jax.ShapeDtypeStruct(x.shape, x.dtype),
        compiler_params=_cparams("parallel", "parallel"),
        name="outproj",
    )(x, oa, ob, mod, w)


def _stack_heads(x, nheads, head_w):
    lane = _iota2((1, x.shape[1]), 1)
    return jnp.concatenate(
        [jnp.where(jnp.logical_and(lane >= h * head_w, lane < (h + 1) * head_w), x, 0.0)
         for h in range(nheads)], axis=0)


def _delta_kernel(qkv_ref, z_ref, ab_ref, carry_ref, s0_ref, cw_ref, hp_ref, gain_ref,
                  o_ref, s_ref, ext_scr, s_scr, *, groups):
    t = pl.program_id(1)
    bb, n, _ = qkv_ref.shape
    rows = groups * n
    nprob = bb // groups
    hr = H_A * rows

    @pl.when(t == 0)
    def _():
        ext_scr[:, 0:SUBLANES, :] = carry_ref[...]
        s_scr[...] = s0_ref[...]

    @pl.when(t > 0)
    def _():
        ext_scr[:, 0:SUBLANES, :] = ext_scr[:, n:n + SUBLANES, :]

    ext_scr[:, SUBLANES:SUBLANES + n, :] = qkv_ref[...]
    conv = cw_ref[0:1, :] * ext_scr[:, 5:5 + n, :]
    for j in range(1, CONV_W):
        conv = conv + cw_ref[j:j + 1, :] * ext_scr[:, 5 + j:5 + j + n, :]
    x = _silu(conv).reshape(bb * n, 3 * GROUP_W)
    ab = ab_ref[...].reshape(bb * n, LANES)
    g = -jnp.exp(hp_ref[0:1, :]) * _softplus(ab + hp_ref[1:2, :])
    beta = _sigmoid(ab)
    z = z_ref[...].reshape(bb * n, GROUP_W)

    incl1, _ = _group_masks(rows, n)
    incl, strict = _group_masks(hr, n)
    lane = _iota2((1, LANES), 1)
    ones = jnp.ones((hr, LANES), F32)
    probs = range(nprob)

    def head_rows(a, h):
        return a[:, h * DK_A:(h + 1) * DK_A]

    gcol, bcol, qs, ks, vs, zs = [], [], [], [], [], []
    kst, lhs = [], []
    grow = []
    for p in probs:
        sl = slice(p * rows, (p + 1) * rows)
        gc = _mm01(incl1, g[sl], 3)
        gsel = jnp.concatenate([jnp.where(lane == h, gc, 0.0) for h in range(H_A)], axis=0)
        bsel = jnp.concatenate([jnp.where(lane == H_A + h, beta[sl], 0.0) for h in range(H_A)], axis=0)
        gcol.append(jnp.sum(gsel, axis=-1, keepdims=True))
        bcol.append(jnp.sum(bsel, axis=-1, keepdims=True))
        grow.append(_mm_nt01(ones, gsel, 3))
        xp = x[sl]
        qn = jnp.concatenate([_l2n(head_rows(xp[:, 0:GROUP_W], h)) * (DK_A ** -0.5)
                              for h in range(H_A)], axis=1)
        kn = jnp.concatenate([_l2n(head_rows(xp[:, GROUP_W:2 * GROUP_W], h)) for h in range(H_A)], axis=1)
        qs.append(jnp.concatenate([head_rows(qn, h) for h in range(H_A)], axis=0))
        ks.append(jnp.concatenate([head_rows(kn, h) for h in range(H_A)], axis=0))
        vs.append(jnp.concatenate([head_rows(xp[:, 2 * GROUP_W:], h) for h in range(H_A)], axis=0))
        zs.append(z[sl])
        k_st = _stack_heads(kn, H_A, DK_A)
        kst.append(k_st)
        lhs.append(jnp.concatenate([k_st * bcol[p], _stack_heads(qn, H_A, DK_A)], axis=0))

    kq = [_mm_nt(lhs[p], kst[p]) for p in probs]
    dec = [jnp.exp(jnp.where(incl, gcol[p] - grow[p], -jnp.inf)) for p in probs]
    a_low = [jnp.where(strict, kq[p][:hr] * dec[p], 0.0) for p in probs]
    tinv = _unit_lower_inv_many(a_low, n)
    eg = [jnp.exp(gcol[p]) for p in probs]
    kb = [ks[p] * bcol[p] for p in probs]
    sol = [_mm(tinv[p], jnp.concatenate([vs[p] * bcol[p], kb[p] * eg[p]], axis=1)) for p in probs]
    qg = [qs[p] * eg[p] for p in probs]

    us, oparts = [], []
    for p in probs:
        u_rows, o_rows = [], []
        for h in range(H_A):
            for gi in range(groups):
                r0 = h * rows + gi * n
                s = s_scr[p * groups + gi, h]
                ksq = _mm(jnp.concatenate([sol[p][r0:r0 + n, DK_A:], qg[p][r0:r0 + n]], axis=0), s)
                u_rows.append(sol[p][r0:r0 + n, :DK_A] - ksq[:n])
                o_rows.append(ksq[n:])
        us.append(jnp.concatenate(u_rows, axis=0))
        oparts.append(jnp.concatenate(o_rows, axis=0))
    outs = [oparts[p] + _mm(kq[p][hr:] * dec[p], us[p]) for p in probs]
    for p in probs:
        for h in range(H_A):
            for gi in range(groups):
                r0 = h * rows + gi * n
                glast = gcol[p][r0 + n - 1:r0 + n]
                kd = ks[p][r0:r0 + n] * jnp.exp(glast - gcol[p][r0:r0 + n])
                s = s_scr[p * groups + gi, h]
                s_scr[p * groups + gi, h] = s * jnp.exp(glast) + _mm_tn(kd, us[p][r0:r0 + n])
            o = outs[p][h * rows:(h + 1) * rows]
            val = _rms(o) * gain_ref[...] * _silu(head_rows(zs[p], h))
            o_ref[p * groups:(p + 1) * groups, :, h * DK_A:(h + 1) * DK_A] = val.reshape(groups, n, DK_A)

    @pl.when(t == pl.num_programs(1) - 1)
    def _():
        s_ref[...] = s_scr[...]


def _delta_call(proj, carry, s0, cw, hp, gain, bb, n, groups):
    b, t, _ = proj.shape
    return pl.pallas_call(
        functools.partial(_delta_kernel, groups=groups),
        grid=(b // bb, t // n),
        in_specs=[
            pl.BlockSpec((bb, n, 3 * GROUP_W), lambda i, k: (i, k, EV_QKV // (3 * GROUP_W))),
            pl.BlockSpec((bb, n, GROUP_W), lambda i, k: (i, k, EV_Z // GROUP_W)),
            pl.BlockSpec((bb, n, LANES), lambda i, k: (i, k, EV_AB // LANES)),
            pl.BlockSpec((bb, SUBLANES, 3 * GROUP_W), lambda i, k: (i, 0, EV_QKV // (3 * GROUP_W))),
            pl.BlockSpec((bb, H_A, DK_A, DK_A), lambda i, k: (i, 0, 0, 0)),
            pl.BlockSpec((SUBLANES, 3 * GROUP_W), lambda i, k: (0, 0)),
            pl.BlockSpec((SUBLANES, LANES), lambda i, k: (0, 0)),
            pl.BlockSpec((1, DK_A), lambda i, k: (0, 0)),
        ],
        out_specs=[
            pl.BlockSpec((bb, n, GROUP_W), lambda i, k: (i, k, 0)),
            pl.BlockSpec((bb, H_A, DK_A, DK_A), lambda i, k: (i, 0, 0, 0)),
        ],
        out_shape=[jax.ShapeDtypeStruct((b, t, GROUP_W), F32),
                   jax.ShapeDtypeStruct((b, H_A, DK_A, DK_A), F32)],
        scratch_shapes=[pltpu.VMEM((bb, n + SUBLANES, 3 * GROUP_W), F32),
                        pltpu.VMEM((bb, H_A, DK_A, DK_A), F32)],
        compiler_params=_cparams("parallel", "arbitrary"),
        name="delta",
    )(proj, proj, proj, carry, s0, cw, hp, gain)


def _rwkv_kernel(rkv_ref, wag_ref, crkv_ref, cwag_ref, s0_ref, mur_ref, muw_ref, wlr_ref, prm_ref,
                 y_ref, s_ref, ext1_scr, ext2_scr, sp_scr, *, groups):
    t = pl.program_id(1)
    bb, n, _ = rkv_ref.shape
    rows = groups * n
    nprob = bb // groups
    npair = H_B // 2
    pw = 2 * HD_B
    r2 = 2 * rows

    @pl.when(t == 0)
    def _():
        ext1_scr[:, 0:SUBLANES, :] = crkv_ref[...]
        ext2_scr[:, 0:SUBLANES, :] = cwag_ref[...]
        sp_scr[...] = s0_ref[...]

    @pl.when(t > 0)
    def _():
        ext1_scr[:, 0:SUBLANES, :] = ext1_scr[:, n:n + SUBLANES, :]
        ext2_scr[:, 0:SUBLANES, :] = ext2_scr[:, n:n + SUBLANES, :]

    cur1 = rkv_ref[...]
    cur2 = wag_ref[...]
    ext1_scr[:, SUBLANES:SUBLANES + n, :] = cur1
    ext2_scr[:, SUBLANES:SUBLANES + n, :] = cur2
    xm1 = cur1 + (ext1_scr[:, SUBLANES - 1:SUBLANES - 1 + n, :] - cur1) * mur_ref[...]
    xm2 = cur2 + (ext2_scr[:, SUBLANES - 1:SUBLANES - 1 + n, :] - cur2) * muw_ref[...]
    xm1 = xm1.reshape(bb * n, 3 * GROUP_W)
    xm2 = xm2.reshape(bb * n, 2 * LANES)
    r = xm1[:, 0:GROUP_W]
    kb = xm1[:, GROUP_W:2 * GROUP_W]
    vb = xm1[:, 2 * GROUP_W:3 * GROUP_W]
    lane2 = _iota2(xm2.shape, 1)
    feat = jnp.where(lane2 < 64, jnp.tanh(xm2), jnp.where(lane2 < 128, xm2, _sigmoid(xm2)))
    lr = _mm(feat, wlr_ref[...])
    w_raw = prm_ref[0:1, :] + lr[:, 0:GROUP_W]
    logw = -jnp.exp(-_softplus(-w_raw) - 0.5)
    a = _sigmoid(prm_ref[1:2, :] + lr[:, GROUP_W:2 * GROUP_W])
    gb = lr[:, 2 * GROUP_W:3 * GROUP_W]
    kkraw = kb * prm_ref[2:3, :]
    k = kb * (1.0 + (a - 1.0) * prm_ref[3:4, :])

    incl1, _ = _group_masks(rows, n)
    incl, strict = _group_masks(r2, n)
    blk = (_iota2((pw, pw), 0) & -HD_B) == (_iota2((pw, pw), 1) & -HD_B)
    blkf = blk.astype(F32)
    cums = [_mm01(incl1, logw[p * rows:(p + 1) * rows], 3) for p in range(nprob)]

    units = [(p, q) for p in range(nprob) for q in range(npair)]

    def st(x):
        return _stack_heads(x, 2, HD_B)

    pre = []
    for p, q in units:
        rs = slice(p * rows, (p + 1) * rows)
        sl = slice(q * pw, (q + 1) * pw)
        kkr = kkraw[rs, sl]
        rp, kp, vp, ap = r[rs, sl], k[rs, sl], vb[rs, sl], a[rs, sl]
        sums = _mm_x01(jnp.concatenate([kkr * kkr, rp * kp * prm_ref[4:5, sl]], axis=0), blkf, 1)
        kk = kkr * lax.rsqrt(sums[:rows] + 1e-6)
        cump = cums[p][:, sl]
        ginv = jnp.exp(-cump)
        rt = rp * jnp.exp(cump)
        at = -kk * jnp.exp(cump - logw[rs, sl])
        bt = kk * ap * ginv
        kt = kp * ginv
        pre.append(dict(rt=rt, at=at, bt=bt, kt=kt, vp=vp, cump=cump, bonus=sums[rows:] * vp,
                        vst=st(vp), gb=gb[rs, sl]))
    mats = [_mm_nt(jnp.concatenate([st(u["at"]), st(u["rt"])], axis=0),
                   jnp.concatenate([st(u["bt"]), st(u["kt"])], axis=0)) for u in pre]
    tinv = _unit_lower_inv_many([-jnp.where(strict, m[:r2, :r2], 0.0) for m in mats], n)

    u0s, y0s = [], []
    for (p, q), u in zip(units, pre):
        u0_rows, y0_rows = [], []
        for gi in range(groups):
            gs = slice(gi * n, (gi + 1) * n)
            uy = _mm_nt(jnp.concatenate([u["at"][gs], u["rt"][gs]], axis=0), sp_scr[p * groups + gi, q])
            u0_rows.append(uy[:n])
            y0_rows.append(uy[n:])
        u0s.append(jnp.concatenate(u0_rows, axis=0))
        y0s.append(jnp.concatenate(y0_rows, axis=0))
    x1 = [_mm(jnp.where(strict, m[:r2, r2:], 0.0), u["vst"]) for m, u in zip(mats, pre)]
    ust = [_mm(ti, st(u0) + x) for ti, u0, x in zip(tinv, u0s, x1)]
    yst = [_mm(jnp.concatenate([jnp.where(incl, m[r2:, :r2], 0.0), jnp.where(incl, m[r2:, r2:], 0.0)], axis=1),
               jnp.concatenate([us_, u["vst"]], axis=0)) for m, us_, u in zip(mats, ust, pre)]
    ys = [y0 + ys_[:rows] + ys_[rows:] for y0, ys_ in zip(y0s, yst)]
    means = [_mm_x01(y, blkf, 1) * (1.0 / HD_B) for y in ys]
    ycs = [y - m for y, m in zip(ys, means)]
    variances = [_mm_x01(yc * yc, blkf, 1) * (1.0 / HD_B) for yc in ycs]

    for (p, q), u, us_, yc, var in zip(units, pre, ust, ycs, variances):
        sl = slice(q * pw, (q + 1) * pw)
        uu = us_[:rows] + us_[rows:]
        for gi in range(groups):
            gs = slice(gi * n, (gi + 1) * n)
            upd = _mm_tn(jnp.concatenate([uu[gs], u["vp"][gs]], axis=0),
                         jnp.concatenate([u["bt"][gs], u["kt"][gs]], axis=0))
            glast = jnp.exp(u["cump"][gi * n + n - 1:gi * n + n, :])
            sp = sp_scr[p * groups + gi, q]
            sp_scr[p * groups + gi, q] = jnp.where(blk, sp + upd, 0.0) * glast
        yn = yc * lax.rsqrt(var + GN_EPS) * prm_ref[5:6, sl] + prm_ref[6:7, sl]
        val = (yn + u["bonus"]) * u["gb"]
        y_ref[p * groups:(p + 1) * groups, :, sl] = val.reshape(groups, n, pw)

    @pl.when(t == pl.num_programs(1) - 1)
    def _():
        s_ref[...] = sp_scr[...]


def _rwkv_call(proj, carry, s0, mur, muw, wlr, prm, bb, n, groups):
    b, t, _ = proj.shape
    wag_w = 2 * LANES
    npair, pw = H_B // 2, 2 * HD_B
    return pl.pallas_call(
        functools.partial(_rwkv_kernel, groups=groups),
        grid=(b // bb, t // n),
        in_specs=[
            pl.BlockSpec((bb, n, 3 * GROUP_W), lambda i, k: (i, k, EV_RKV // (3 * GROUP_W))),
            pl.BlockSpec((bb, n, wag_w), lambda i, k: (i, k, EV_WAG // wag_w)),
            pl.BlockSpec((bb, SUBLANES, 3 * GROUP_W), lambda i, k: (i, 0, EV_RKV // (3 * GROUP_W))),
            pl.BlockSpec((bb, SUBLANES, wag_w), lambda i, k: (i, 0, EV_WAG // wag_w)),
            pl.BlockSpec((bb, npair, pw, pw), lambda i, k: (i, 0, 0, 0)),
            pl.BlockSpec((1, 3 * GROUP_W), lambda i, k: (0, 0)),
            pl.BlockSpec((1, wag_w), lambda i, k: (0, 0)),
            pl.BlockSpec((wag_w, 3 * GROUP_W), lambda i, k: (0, 0)),
            pl.BlockSpec((SUBLANES, GROUP_W), lambda i, k: (0, 0)),
        ],
        out_specs=[
            pl.BlockSpec((bb, n, GROUP_W), lambda i, k: (i, k, 0)),
            pl.BlockSpec((bb, npair, pw, pw), lambda i, k: (i, 0, 0, 0)),
        ],
        out_shape=[jax.ShapeDtypeStruct((b, t, GROUP_W), F32),
                   jax.ShapeDtypeStruct((b, npair, pw, pw), F32)],
        scratch_shapes=[pltpu.VMEM((bb, n + SUBLANES, 3 * GROUP_W), F32),
                        pltpu.VMEM((bb, n + SUBLANES, wag_w), F32),
                        pltpu.VMEM((bb, npair, pw, pw), F32)],
        compiler_params=_cparams("parallel", "arbitrary"),
        name="rwkv7",
    )(proj, proj, carry, carry, s0, mur, muw, wlr, prm)


def _rwkv_state_to_pairs(s):
    b = s.shape[0]
    s = s.reshape(b, H_B // 2, 2, HD_B, HD_B)
    zero = jnp.zeros_like(s[:, :, 0])
    top = jnp.concatenate([s[:, :, 0], zero], axis=-1)
    bot = jnp.concatenate([zero, s[:, :, 1]], axis=-1)
    return jnp.concatenate([top, bot], axis=-2)


def _rwkv_state_from_pairs(sp):
    b = sp.shape[0]
    heads = jnp.stack([sp[:, :, :HD_B, :HD_B], sp[:, :, HD_B:, HD_B:]], axis=2)
    return heads.reshape(b, H_B, HD_B, HD_B)


def _stack_pair(x):
    first = (_iota2((1, x.shape[1]), 1) & 64) == 0
    return jnp.concatenate([jnp.where(first, x, 0.0), jnp.where(first, 0.0, x)], axis=0)


def _group_rows(a, rows, n, gi):
    return jnp.concatenate([a[gi * n:(gi + 1) * n], a[rows + gi * n:rows + (gi + 1) * n]], axis=0)


def _ungroup_rows(pieces, n):
    return jnp.concatenate([p[:n] for p in pieces] + [p[n:] for p in pieces], axis=0)


def _gla_kernel(qk_ref, v_ref, z_ref, gkin_ref, s0_ref, wgk_ref, bgk_ref, gain_ref,
                o_ref, s_ref, s_scr, *, groups):
    t = pl.program_id(1)
    bb, n, _ = qk_ref.shape
    rows = groups * n
    nprob = bb // groups
    npair = H_C // 2
    pw = 2 * DK_C
    qkw = H_C * DK_C
    r2 = 2 * rows
    sub = min(GLA_SUB, n)
    nslab = n // sub
    assert groups == 1 or nslab == 1

    @pl.when(t == 0)
    def _():
        s_scr[...] = s0_ref[...]

    qk = qk_ref[...].reshape(bb * n, 2 * qkw)
    q = qk[:, :qkw] * (DK_C ** -0.5)
    k = qk[:, qkw:]
    v = v_ref[...].reshape(bb * n, GROUP_W)
    z = z_ref[...].reshape(bb * n, GROUP_W)
    gk = _log_sigmoid(_mm(gkin_ref[...].reshape(bb * n, LANES), wgk_ref[...]) + bgk_ref[...]) * (1.0 / GLA_NORM)
    incl1, _ = _group_masks(rows, n)
    incl2, _ = _group_masks(r2, n)
    row_t = _iota2((rows, 1), 0) & (n - 1)
    eye_p = _eye(pw)
    cums = [_mm01(incl1, gk[p * rows:(p + 1) * rows], 3) for p in range(nprob)]
    units = [(p, u) for p in range(nprob) for u in range(npair)]

    pre = []
    for p, u in units:
        rs = slice(p * rows, (p + 1) * rows)
        sl = slice(u * pw, (u + 1) * pw)
        bcum, qp, kp = cums[p][:, sl], q[rs, sl], k[rs, sl]
        qparts, kparts = [], []
        for s in range(nslab):
            rho = bcum[s * sub - 1:s * sub, :] if s > 0 else jnp.zeros((1, pw), F32)
            in_slab = jnp.logical_and(row_t >= s * sub, row_t < (s + 1) * sub)
            qparts.append(qp * jnp.exp(jnp.where(in_slab, bcum - rho, -jnp.inf)))
            kparts.append(kp * jnp.exp(jnp.where(row_t < (s + 1) * sub, rho - bcum, -jnp.inf)))
        blast = jnp.concatenate(
            [jnp.broadcast_to(bcum[gi * n + n - 1:gi * n + n, :], (n, pw)) for gi in range(groups)], axis=0)
        vst = jnp.concatenate([v[rs, (2 * u + j) * DV_C:(2 * u + j + 1) * DV_C] for j in range(2)], axis=0)
        pre.append(dict(qcat=_stack_pair(jnp.concatenate(qparts, axis=1)),
                        kcat=_stack_pair(jnp.concatenate(kparts, axis=1)),
                        qdb=_stack_pair(qp * jnp.exp(bcum)), kd=_stack_pair(kp * jnp.exp(blast - bcum)),
                        vst=vst, bcum=bcum))
    att = [jnp.where(incl2, _mm_nt(u["qcat"], u["kcat"]), 0.0) for u in pre]
    intra = [_mm(a, u["vst"]) for a, u in zip(att, pre)]
    inter = []
    for (p, uidx), u in zip(units, pre):
        pieces = [_mm(_group_rows(u["qdb"], rows, n, gi), s_scr[p * groups + gi, uidx])
                  for gi in range(groups)]
        inter.append(_ungroup_rows(pieces, n))
    pick_last = (_iota2((BF16_ROWS, rows), 1)
                 == (_iota2((BF16_ROWS, rows), 0) & (groups - 1)) * n + n - 1).astype(F32)
    glast = [jnp.exp(_mm01(pick_last, u["bcum"], 3)) for u in pre]
    glcols = [_mm_nt01(eye_p, g, 3) for g in glast]
    for (p, uidx), u, o_in, o_x, glc in zip(units, pre, intra, inter, glcols):
        for gi in range(groups):
            upd = _mm_tn(_group_rows(u["kd"], rows, n, gi), _group_rows(u["vst"], rows, n, gi))
            s_scr[p * groups + gi, uidx] = s_scr[p * groups + gi, uidx] * glc[:, gi:gi + 1] + upd
        o = o_in + o_x
        for j in range(2):
            h = 2 * uidx + j
            zh = z[p * rows:(p + 1) * rows, h * DV_C:(h + 1) * DV_C]
            val = _rms(o[j * rows:(j + 1) * rows]) * gain_ref[...] * _silu(zh)
            o_ref[p * groups:(p + 1) * groups, :, h * DV_C:(h + 1) * DV_C] = val.reshape(groups, n, DV_C)

    @pl.when(t == pl.num_programs(1) - 1)
    def _():
        s_ref[...] = s_scr[...]


def _gla_call(proj, s0, wgk, bgk, gain, bb, n, groups):
    b, t, _ = proj.shape
    npair, pw = H_C // 2, 2 * DK_C
    return pl.pallas_call(
        functools.partial(_gla_kernel, groups=groups),
        grid=(b // bb, t // n),
        in_specs=[
            pl.BlockSpec((bb, n, GROUP_W), lambda i, k: (i, k, OD_CQK // GROUP_W)),
            pl.BlockSpec((bb, n, GROUP_W), lambda i, k: (i, k, OD_CV // GROUP_W)),
            pl.BlockSpec((bb, n, GROUP_W), lambda i, k: (i, k, OD_CZ // GROUP_W)),
            pl.BlockSpec((bb, n, LANES), lambda i, k: (i, k, OD_CGK // LANES)),
            pl.BlockSpec((bb, npair, DV_C, pw), lambda i, k: (i, 0, 0, 0)),
            pl.BlockSpec((LANES, H_C * DK_C), lambda i, k: (0, 0)),
            pl.BlockSpec((1, H_C * DK_C), lambda i, k: (0, 0)),
            pl.BlockSpec((1, DV_C), lambda i, k: (0, 0)),
        ],
        out_specs=[
            pl.BlockSpec((bb, n, GROUP_W), lambda i, k: (i, k, 0)),
            pl.BlockSpec((bb, npair, DV_C, pw), lambda i, k: (i, 0, 0, 0)),
        ],
        out_shape=[jax.ShapeDtypeStruct((b, t, GROUP_W), F32),
                   jax.ShapeDtypeStruct((b, npair, DV_C, pw), F32)],
        scratch_shapes=[pltpu.VMEM((bb, npair, DV_C, pw), F32)],
        compiler_params=_cparams("parallel", "arbitrary"),
        name="gla",
    )(proj, proj, proj, proj, s0, wgk, bgk, gain)


def _mlstm_kernel(qk_ref, v_ref, og_ref, if_ref, c0_ref, n0_ref, m0_ref, bif_ref, gain_ref,
                  h_ref, c_ref, nn_ref, m_ref, c_scr, n_scr, m_scr, *, groups):
    t = pl.program_id(1)
    bb, n, _ = qk_ref.shape
    rows = groups * n
    nprob = bb // groups
    npair = H_D // 2
    pw = 2 * DK_D
    qkw = H_D * DK_D
    r2 = 2 * rows

    @pl.when(t == 0)
    def _():
        c_scr[...] = c0_ref[...]
        n_scr[...] = n0_ref[...]
        m_scr[...] = m0_ref[...]

    x = if_ref[...].reshape(bb * n, LANES) + bif_ref[...]
    lf = _log_sigmoid(x)
    qk = qk_ref[...].reshape(bb * n, 2 * qkw)
    q = qk[:, :qkw]
    k = qk[:, qkw:] * (DK_D ** -0.5)
    v = v_ref[...].reshape(bb * n, GROUP_W)
    og = og_ref[...].reshape(bb * n, GROUP_W)
    incl1, _ = _group_masks(rows, n)
    incl2, _ = _group_masks(r2, n)
    r_i, c_i = _iota2((r2, r2), 0), _iota2((r2, r2), 1)
    same2 = (r_i & -n) == (c_i & -n)
    last2 = jnp.logical_and(same2, (c_i & (n - 1)) == n - 1)
    lane = _iota2((1, LANES), 1)
    ones = jnp.ones((r2, LANES), F32)
    fcums = [_mm01(incl1, lf[p * rows:(p + 1) * rows], 3) for p in range(nprob)]
    m_rows = [m_scr[b_] for b_ in range(bb)]
    m_old = list(m_rows)
    units = [(p, u) for p in range(nprob) for u in range(npair)]

    def per_block(fn):
        return jnp.concatenate([jnp.broadcast_to(fn(j, gi), (n, fn(j, gi).shape[1]))
                                for j in range(2) for gi in range(groups)], axis=0)

    pre = []
    for p, u in units:
        rs = slice(p * rows, (p + 1) * rows)
        sl = slice(u * pw, (u + 1) * pw)
        fsel = jnp.concatenate([jnp.where(lane == H_D + 2 * u + j, fcums[p], 0.0) for j in range(2)], axis=0)
        isel = jnp.concatenate([jnp.where(lane == 2 * u + j, x[rs], 0.0) for j in range(2)], axis=0)
        fcol = jnp.sum(fsel, axis=-1, keepdims=True)
        icol = jnp.sum(isel, axis=-1, keepdims=True)
        drow = _mm_nt01(ones, isel - fsel, 3)
        flast = jnp.sum(_mm01(last2, fsel, 3), axis=-1, keepdims=True)
        mprev = per_block(lambda j, gi: m_old[p * groups + gi][:, 2 * u + j:2 * u + j + 1])
        nmat = per_block(lambda j, gi: n_scr[p * groups + gi][:, sl])
        log_d = jnp.where(incl2, fcol + drow, -jnp.inf)
        m_in = jnp.max(log_d, axis=-1, keepdims=True)
        m_e = jnp.max(jnp.where(same2, flast + drow, -jnp.inf), axis=-1, keepdims=True)
        m_t = jnp.maximum(fcol + mprev, m_in)
        w_in = jnp.exp(fcol + mprev - m_t)
        m_new = jnp.maximum(flast + mprev, m_e)
        qst, kst = _stack_pair(q[rs, sl]), _stack_pair(k[rs, sl])
        vst = jnp.concatenate([v[rs, (2 * u + j) * DV_D:(2 * u + j + 1) * DV_D] for j in range(2)], axis=0)
        pre.append(dict(qst=qst, kst=kst, vst=vst, log_d=log_d, m_t=m_t, w_in=w_in, m_new=m_new,
                        cd=jnp.exp(flast + mprev - m_new), nmat=nmat,
                        ke=kst * jnp.exp(flast - fcol + icol - m_new)))
    dms = [jnp.exp(u["log_d"] - u["m_t"]) * _mm_nt(u["qst"], u["kst"]) for u in pre]
    intra = [_mm(dm, u["vst"]) for dm, u in zip(dms, pre)]
    inter = []
    for (p, uidx), u in zip(units, pre):
        qw = u["w_in"] * u["qst"]
        pieces = [_mm(_group_rows(qw, rows, n, gi), c_scr[p * groups + gi, uidx]) for gi in range(groups)]
        inter.append(_ungroup_rows(pieces, n))
    for (p, uidx), u, dm, o_in, o_x in zip(units, pre, dms, intra, inter):
        sl = slice(uidx * pw, (uidx + 1) * pw)
        den = (u["w_in"] * jnp.sum(u["qst"] * u["nmat"], axis=-1, keepdims=True)
               + jnp.sum(dm, axis=-1, keepdims=True))
        hout = (o_in + o_x) / jnp.maximum(jnp.abs(den), jnp.exp(-u["m_t"]))
        for gi in range(groups):
            b_ = p * groups + gi
            r0, r1 = gi * n, rows + gi * n
            cdrow = jnp.where(lane < DK_D, u["cd"][r0:r0 + 1], u["cd"][r1:r1 + 1])
            ke_g = _group_rows(u["ke"], rows, n, gi)
            cdcol = jnp.where(_iota2((pw, 1), 0) < DK_D, u["cd"][r0:r0 + 1], u["cd"][r1:r1 + 1])
            c_scr[b_, uidx] = c_scr[b_, uidx] * cdcol + _mm_tn(ke_g, _group_rows(u["vst"], rows, n, gi))
            n_scr[b_, :, sl] = n_scr[b_][:, sl] * cdrow + jnp.sum(ke_g, axis=0, keepdims=True)
            for j, r in ((0, r0), (1, r1)):
                m_rows[b_] = jnp.where(lane == 2 * uidx + j, u["m_new"][r:r + 1], m_rows[b_])
        for j in range(2):
            h = 2 * uidx + j
            ogh = og[p * rows:(p + 1) * rows, h * DV_D:(h + 1) * DV_D]
            val = _sigmoid(ogh) * (_rms(hout[j * rows:(j + 1) * rows]) * gain_ref[...])
            h_ref[p * groups:(p + 1) * groups, :, h * DV_D:(h + 1) * DV_D] = val.reshape(groups, n, DV_D)
    for b_ in range(bb):
        m_scr[b_] = m_rows[b_]

    @pl.when(t == pl.num_programs(1) - 1)
    def _():
        c_ref[...] = c_scr[...]
        nn_ref[...] = n_scr[...]
        m_ref[...] = m_scr[...]


def _mlstm_call(proj, c0, n0, m0, bif, gain, bb, n, groups):
    b, t, _ = proj.shape
    qkw = H_D * DK_D
    npair, pw = H_D // 2, 2 * DK_D
    return pl.pallas_call(
        functools.partial(_mlstm_kernel, groups=groups),
        grid=(b // bb, t // n),
        in_specs=[
            pl.BlockSpec((bb, n, GROUP_W), lambda i, k: (i, k, OD_DQK // GROUP_W)),
            pl.BlockSpec((bb, n, GROUP_W), lambda i, k: (i, k, OD_DV // GROUP_W)),
            pl.BlockSpec((bb, n, GROUP_W), lambda i, k: (i, k, OD_DO // GROUP_W)),
            pl.BlockSpec((bb, n, LANES), lambda i, k: (i, k, OD_DIF // LANES)),
            pl.BlockSpec((bb, npair, DV_D, pw), lambda i, k: (i, 0, 0, 0)),
            pl.BlockSpec((bb, 1, qkw), lambda i, k: (i, 0, 0)),
            pl.BlockSpec((bb, 1, LANES), lambda i, k: (i, 0, 0)),
            pl.BlockSpec((1, LANES), lambda i, k: (0, 0)),
            pl.BlockSpec((1, DV_D), lambda i, k: (0, 0)),
        ],
        out_specs=[
            pl.BlockSpec((bb, n, GROUP_W), lambda i, k: (i, k, 0)),
            pl.BlockSpec((bb, npair, DV_D, pw), lambda i, k: (i, 0, 0, 0)),
            pl.BlockSpec((bb, 1, qkw), lambda i, k: (i, 0, 0)),
            pl.BlockSpec((bb, 1, LANES), lambda i, k: (i, 0, 0)),
        ],
        out_shape=[jax.ShapeDtypeStruct((b, t, GROUP_W), F32),
                   jax.ShapeDtypeStruct((b, npair, DV_D, pw), F32),
                   jax.ShapeDtypeStruct((b, 1, qkw), F32),
                   jax.ShapeDtypeStruct((b, 1, LANES), F32)],
        scratch_shapes=[pltpu.VMEM((bb, npair, DV_D, pw), F32),
                        pltpu.VMEM((bb, 1, qkw), F32),
                        pltpu.VMEM((bb, 1, LANES), F32)],
        compiler_params=_cparams("parallel", "arbitrary"),
        name="mlstm",
    )(proj, proj, proj, proj, c0, n0, m0, bif, gain)


def _pad_cols(w, width):
    return jnp.pad(w, ((0, 0), (0, width - w.shape[1])))


def _pad_rows(w, rows, at=0):
    return jnp.pad(w, ((at, rows - at - w.shape[0]), (0, 0)))


def _even_in_weight(w):
    pa, pb = w[:, :2056], w[:, 2056:]
    cols = [
        pb[:, 0:1536],
        pa[:, 0:1536],
        pa[:, 1544:2056],
        pb[:, 1536:1792],
        _pad_cols(pa[:, 1536:1544], LANES),
    ]
    return _pad_cols(jnp.concatenate(cols, axis=1), EV_COLS).astype(BF16)


def _odd_in_weight(w):
    pc, pd = w[:, :1552], w[:, 1552:]
    cols = [
        pc[:, 0:512],
        pd[:, 0:512],
        pc[:, 512:1024],
        pc[:, 1040:1552],
        pd[:, 512:1024],
        pd[:, 1032:1544],
        _pad_cols(pc[:, 1024:1040], LANES),
        _pad_cols(pd[:, 1024:1032], LANES),
    ]
    return jnp.concatenate(cols, axis=1).astype(BF16)


def _row(v):
    return v.reshape(1, -1).astype(F32)


def _tiles(x):
    b, t, _ = x.shape
    tb = min(t, 1024)
    bb = min(b, 1024 // tb)
    return bb, tb


def _trunk(x, mod, states, wts):
    xbuf, s_delta, s_rwkv, s_gla, s_mc, s_mn, s_mm = states
    b, t, _ = x.shape
    bb, tb = _tiles(x)
    n = min(CHUNK, t)
    groups = CHUNK // n
    rbb = (PROBLEMS_LONG if t > CHUNK else PROBLEMS) * groups
    new_even = ([], [], [])
    new_odd = ([], [], [], [])
    for l in range(DEPTH):
        lw = wts["layers"][l]
        m_l = mod[l]
        x = _ffn_call(x, m_l, lw["gain0"], lw["wg0"], lw["wu0"], lw["wd0"], wts["final_gain"], 0, bb, tb, 256,
                      False)
        i = l // 2
        if l % 2 == 0:
            h_tail, proj = _adaln_proj_call(x, m_l, lw["gain1"], lw["w_in"], bb, tb, 1024, True)
            if xbuf is None:
                carry = jnp.zeros((b, SUBLANES, EV_COLS), F32)
            else:
                rows = _rows_proj_call(xbuf[i].reshape(b * (CONV_W - 1), D_MODEL), lw["w_in"], 1024)
                carry = jnp.pad(rows.reshape(b, CONV_W - 1, EV_COLS),
                                ((0, 0), (SUBLANES - CONV_W + 1, 0), (0, 0)))
            oa, sd = _delta_call(proj, carry, s_delta[i], lw["conv_w"], lw["delta_hp"], lw["gain_a"],
                                 rbb, n, groups)
            ob, sr = _rwkv_call(proj, carry, _rwkv_state_to_pairs(s_rwkv[i]), lw["mu_rkv"], lw["mu_wag"],
                                lw["w_lora"], lw["rwkv_prm"], rbb, n, groups)
            for lst, val in zip(new_even, (h_tail[:, SUBLANES - (CONV_W - 1):], sd, _rwkv_state_from_pairs(sr))):
                lst.append(val)
        else:
            (proj,) = _adaln_proj_call(x, m_l, lw["gain1"], lw["w_in"], bb, tb, 1664, False)
            pair_shape = (b, H_C // 2, 2 * DK_C, DV_C)
            oa, sg = _gla_call(proj, s_gla[i].reshape(pair_shape), lw["w_gk2"], lw["b_gk"], lw["gain_c"],
                               rbb, n, groups)
            ob, sc, sn, sm = _mlstm_call(proj, s_mc[i].reshape(pair_shape), s_mn[i].reshape(b, 1, H_D * DK_D),
                                         _pad_cols(s_mm[i], LANES).reshape(b, 1, LANES),
                                         lw["b_if"], lw["gain_d"], rbb, n, groups)
            for lst, val in zip(new_odd, (sg.reshape(b, H_C, DK_C, DV_C), sc.reshape(b, H_D, DK_D, DV_D),
                                          sn.reshape(b, H_D, DK_D), sm[:, 0, :H_D])):
                lst.append(val)
        x = _outproj_call(x, oa, ob, m_l, lw["w_out"], bb, tb)
        x = _ffn_call(x, m_l, lw["gain2"], lw["wg1"], lw["wu1"], lw["wd1"], wts["final_gain"], 6, bb, tb, 256,
                      l == DEPTH - 1)
    stacked = [jnp.stack(lst) for lst in new_even + new_odd]
    return (x, *stacked)


def _prepare_weights(norm_gain, final_gain, w_ffn_gate, w_ffn_up, w_ffn_down, w_in_even, w_out_even,
                     conv_w, a_log, dt_bias, gain_a, mu_b, w0_b, w_w2, a0_b, w_a2, w_g2, k_k, k_a, r_k,
                     lnx_gain, lnx_bias, w_in_odd, w_out_odd, w_gk2, b_gk, gain_c, b_i, b_f, gain_d):
    layers = []
    for l in range(DEPTH):
        i = l // 2
        lw = {
            "gain0": _row(norm_gain[l, 0]), "gain1": _row(norm_gain[l, 1]), "gain2": _row(norm_gain[l, 2]),
            "wg0": w_ffn_gate[l, 0].astype(BF16), "wu0": w_ffn_up[l, 0].astype(BF16),
            "wd0": w_ffn_down[l, 0].astype(BF16),
            "wg1": w_ffn_gate[l, 1].astype(BF16), "wu1": w_ffn_up[l, 1].astype(BF16),
            "wd1": w_ffn_down[l, 1].astype(BF16),
        }
        if l % 2 == 0:
            lw["w_in"] = _even_in_weight(w_in_even[i])
            lw["w_out"] = w_out_even[i].astype(BF16)
            lw["conv_w"] = _pad_rows(conv_w[i].astype(F32), SUBLANES)
            lw["delta_hp"] = _pad_rows(jnp.stack([_pad_cols(_row(a_log[i]), LANES)[0],
                                                  _pad_cols(_row(dt_bias[i]), LANES)[0]]), SUBLANES)
            lw["gain_a"] = _row(gain_a[i])
            lw["mu_rkv"] = _row(mu_b[i, :1536])
            lw["mu_wag"] = _row(mu_b[i, 1536:])
            lora = jnp.zeros((2 * LANES, 3 * GROUP_W), F32)
            lora = lora.at[0:64, 0:GROUP_W].set(w_w2[i])
            lora = lora.at[64:128, GROUP_W:2 * GROUP_W].set(w_a2[i])
            lora = lora.at[128:256, 2 * GROUP_W:].set(w_g2[i])
            lw["w_lora"] = lora.astype(BF16)
            lw["rwkv_prm"] = jnp.stack([w0_b[i], a0_b[i], k_k[i], k_a[i], r_k[i].reshape(-1),
                                        lnx_gain[i], lnx_bias[i], jnp.zeros_like(w0_b[i])]).astype(F32)
        else:
            lw["w_in"] = _odd_in_weight(w_in_odd[i])
            lw["w_out"] = w_out_odd[i].astype(BF16)
            lw["w_gk2"] = _pad_rows(w_gk2[i], LANES).astype(BF16)
            lw["b_gk"] = _row(b_gk[i])
            lw["gain_c"] = _row(gain_c[i])
            lw["b_if"] = _pad_cols(_row(jnp.concatenate([b_i[i], b_f[i]])), LANES)
            lw["gain_d"] = _row(gain_d[i])
        layers.append(lw)
    return {"layers": layers, "final_gain": _row(final_gain)}


def kernel(x_prompt, x_sample, c_prompt, c_sample, state_xbuf_even, state_delta, state_rwkv, state_gla,
           state_mlstm_c, state_mlstm_n, state_mlstm_m, w_mod, b_mod, norm_gain, final_gain, w_ffn_gate,
           w_ffn_up, w_ffn_down, w_in_even, w_out_even, conv_w, a_log, dt_bias, gain_a, mu_b, w0_b, w_w2,
           a0_b, w_a2, w_g2, k_k, k_a, r_k, lnx_gain, lnx_bias, w_in_odd, w_out_odd, w_gk2, b_gk, gain_c,
           b_i, b_f, gain_d):
    wts = _prepare_weights(norm_gain, final_gain, w_ffn_gate, w_ffn_up, w_ffn_down, w_in_even, w_out_even,
                           conv_w, a_log, dt_bias, gain_a, mu_b, w0_b, w_w2, a0_b, w_a2, w_g2, k_k, k_a,
                           r_k, lnx_gain, lnx_bias, w_in_odd, w_out_odd, w_gk2, b_gk, gain_c, b_i, b_f,
                           gain_d)
    bp, bs = x_prompt.shape[0], x_sample.shape[0]
    c_all = jnp.concatenate([c_prompt, c_sample], axis=0).astype(F32)
    mod = _mod_call(c_all, w_mod, b_mod).reshape(DEPTH, bp + bs, N_MOD, 1, D_MODEL)
    mod_p, mod_s = mod[:, :bp], mod[:, bp:]

    def zeros(shape):
        return jnp.zeros(shape, F32)

    zero_states = (
        None,
        zeros((N_EVEN, bp, H_A, DK_A, DK_A)),
        zeros((N_EVEN, bp, H_B, HD_B, HD_B)),
        zeros((N_ODD, bp, H_C, DK_C, DV_C)),
        zeros((N_ODD, bp, H_D, DK_D, DV_D)),
        zeros((N_ODD, bp, H_D, DK_D)),
        zeros((N_ODD, bp, H_D)),
    )
    y_p, xb_p, dl_p, rw_p, gl_p, mc_p, mn_p, mm_p = _trunk(x_prompt, mod_p, zero_states, wts)
    sample_states = (state_xbuf_even, state_delta, state_rwkv, state_gla,
                     state_mlstm_c, state_mlstm_n, state_mlstm_m)
    y_s, xb_s, dl_s, rw_s, gl_s, mc_s, mn_s, mm_s = _trunk(x_sample, mod_s, sample_states, wts)
    return (y_p, y_s, xb_p, xb_s, dl_p, dl_s, rw_p, rw_s, gl_p, gl_s, mc_p, mc_s, mn_p, mn_s, mm_p, mm_s)
```

```python
import functools

import jax
import jax.numpy as jnp
from jax import lax
from jax.experimental import pallas as pl
from jax.experimental.pallas import tpu as pltpu

F32 = jnp.float32
BF16 = jnp.bfloat16
HIGHEST = lax.Precision.HIGHEST

D_MODEL = 1024
DEPTH = 4
N_EVEN = 2
N_ODD = 2
D_FF = 2816
N_MOD = 9
EPS = 1e-6
GN_EPS = 64e-5
CONV_W = 4
H_A, DK_A = 4, 128
H_B, HD_B = 8, 64
H_C, DK_C, DV_C = 4, 64, 128
H_D, DK_D, DV_D = 4, 64, 128
GLA_NORM = 16.0
GROUP_W = 512

LANES = 128
SUBLANES = 8
BF16_ROWS = 16
VMEM_LIMIT_BYTES = 48 * 1024 * 1024

EV_RKV, EV_QKV, EV_Z, EV_WAG, EV_AB, EV_COLS = 0, 1536, 3072, 3584, 3840, 4096
OD_CQK, OD_DQK, OD_CV, OD_CZ, OD_DV, OD_DO, OD_CGK, OD_DIF, OD_COLS = (
    0, 512, 1024, 1536, 2048, 2560, 3072, 3200, 3328)

CHUNK = 64
GLA_SUB = 16
PROJ_ROWS = 512
PROJ_COLS = 256
FFN_ROWS = 1024
FFN_COLS = 256
PROBLEMS = 2
PROBLEMS_LONG = 8


def _bf(x):
    if x.dtype == BF16:
        return x
    if x.shape[-2] % BF16_ROWS == 0 and x.shape[-1] % BF16_ROWS == 0:
        return x.astype(BF16)
    return x


def _pair(a, b):
    a, b = _bf(a), _bf(b)
    if a.dtype != b.dtype:
        a, b = a.astype(F32), b.astype(F32)
    return a, b


def _mm(a, b):
    a, b = _pair(a, b)
    return jnp.dot(a, b, preferred_element_type=F32)


def _mm_nt(a, b):
    a, b = _pair(a, b)
    return lax.dot_general(a, b, (((1,), (1,)), ((), ())), preferred_element_type=F32)


def _mm_tn(a, b):
    a, b = _pair(a, b)
    return lax.dot_general(a, b, (((0,), (0,)), ((), ())), preferred_element_type=F32)


def _mm_f32(a, b):
    return jnp.dot(a, b, precision=HIGHEST, preferred_element_type=F32)


def _mm_nt_f32(a, b):
    return lax.dot_general(a, b, (((1,), (1,)), ((), ())), precision=HIGHEST,
                           preferred_element_type=F32)


def _split_bf16(x, parts):
    out, r = [], x
    for i in range(parts):
        p = r.astype(BF16)
        out.append(p)
        if i + 1 < parts:
            r = r - p.astype(F32)
    return out


def _mm01(a01, x, parts):
    a = a01.astype(BF16)
    acc = None
    for p in _split_bf16(x, parts):
        d = jnp.dot(a, p, preferred_element_type=F32)
        acc = d if acc is None else acc + d
    return acc


def _mm_x01(x, b01, parts):
    b = b01.astype(BF16)
    acc = None
    for p in _split_bf16(x, parts):
        d = jnp.dot(p, b, preferred_element_type=F32)
        acc = d if acc is None else acc + d
    return acc


def _mm_nt01(a01, x, parts):
    a = a01.astype(BF16)
    acc = None
    for p in _split_bf16(x, parts):
        d = lax.dot_general(a, p, (((1,), (1,)), ((), ())), preferred_element_type=F32)
        acc = d if acc is None else acc + d
    return acc


def _sigmoid(x):
    return jax.nn.sigmoid(x)


def _silu(x):
    return x * jax.nn.sigmoid(x)


def _softplus(x):
    return jnp.maximum(x, 0.0) + jnp.log1p(jnp.exp(-jnp.abs(x)))


def _log_sigmoid(x):
    return -_softplus(-x)


def _rms(x, eps=EPS):
    return x * lax.rsqrt(jnp.mean(x * x, axis=-1, keepdims=True) + eps)


def _l2n(x):
    return x * lax.rsqrt(jnp.sum(x * x, axis=-1, keepdims=True) + 1e-6)


def _iota2(shape, dim):
    return lax.broadcasted_iota(jnp.int32, shape, dim)


def _tril_masks(n):
    r, c = _iota2((n, n), 0), _iota2((n, n), 1)
    return r >= c, r > c


def _group_masks(size, n):
    r, c = _iota2((size, size), 0), _iota2((size, size), 1)
    same = (r & -n) == (c & -n)
    return jnp.logical_and(same, r >= c), jnp.logical_and(same, r > c)


def _eye(n):
    return (_iota2((n, n), 0) == _iota2((n, n), 1)).astype(F32)


def _transpose_rows(x):
    return _mm_nt_f32(_eye(x.shape[1]), x)


def _unit_lower_inv_many(mats, n):
    eye = _eye(mats[0].shape[0])
    ms = [-a for a in mats]
    ps = [eye + m for m in ms]
    covered = 2
    while covered < n:
        ms = [_mm(m, m) for m in ms]
        ps = [p + _mm(p, m) for p, m in zip(ps, ms)]
        covered *= 2
    return ps


def _cparams(*sem):
    return pltpu.CompilerParams(dimension_semantics=sem, vmem_limit_bytes=VMEM_LIMIT_BYTES)


def _mod_kernel(c_ref, w_ref, b_ref, o_ref):
    cs = _silu(c_ref[...])
    o_ref[...] = _mm(cs, w_ref[...]) + b_ref[...]


def _mod_call(c_all, w_mod, b_mod):
    rows = c_all.shape[0]
    tn = 1024
    width = N_MOD * D_MODEL
    return pl.pallas_call(
        _mod_kernel,
        grid=(DEPTH, width // tn),
        in_specs=[
            pl.BlockSpec((rows, D_MODEL), lambda l, j: (0, 0)),
            pl.BlockSpec((None, D_MODEL, tn), lambda l, j: (l, 0, j)),
            pl.BlockSpec((None, 1, tn), lambda l, j: (l, 0, j)),
        ],
        out_specs=pl.BlockSpec((None, rows, tn), lambda l, j: (l, 0, j)),
        out_shape=jax.ShapeDtypeStruct((DEPTH, rows, width), F32),
        compiler_params=_cparams("parallel", "parallel"),
        name="mod",
    )(c_all, w_mod, b_mod.reshape(DEPTH, 1, width))


def _adaln(x, gain, scale, shift):
    return _rms(x) * gain * (1.0 + scale) + shift


def _ffn_kernel(x_ref, sh_ref, sc_ref, gt_ref, gain_ref, wg_ref, wu_ref, wd_ref, fgain_ref, o_ref,
                h_scr, acc_scr, *, final_norm):
    f = pl.program_id(2)
    bb, tb, d = x_ref.shape

    @pl.when(f == 0)
    def _():
        h = _adaln(x_ref[...], gain_ref[...], sc_ref[...], sh_ref[...])
        h_scr[...] = h.reshape(bb * tb, d).astype(BF16)
        acc_scr[...] = jnp.zeros_like(acc_scr)

    h = h_scr[...]
    g = jnp.dot(h, wg_ref[...], preferred_element_type=F32)
    u = jnp.dot(h, wu_ref[...], preferred_element_type=F32)
    a = (_silu(g) * u).astype(BF16)
    acc_scr[...] += jnp.dot(a, wd_ref[...], preferred_element_type=F32)

    @pl.when(f == pl.num_programs(2) - 1)
    def _():
        y = acc_scr[...].reshape(bb, tb, d)
        out = x_ref[...] + 0.5 * (1.0 + gt_ref[...]) * y
        if final_norm:
            out = _rms(out) * fgain_ref[...]
        o_ref[...] = out


def _mod_spec(bb, j, ngrid):
    if ngrid == 3:
        return pl.BlockSpec((bb, None, 1, D_MODEL), lambda b, t, f: (b, j, 0, 0))
    return pl.BlockSpec((bb, None, 1, D_MODEL), lambda b, t: (b, j, 0, 0))


def _ffn_call(x, mod, gain, wg, wu, wd, fgain, j0, bb, tb, tf, final_norm):
    b, t, d = x.shape
    xspec = pl.BlockSpec((bb, tb, d), lambda i, k, f: (i, k, 0))
    rowspec = pl.BlockSpec((1, d), lambda i, k, f: (0, 0))
    return pl.pallas_call(
        functools.partial(_ffn_kernel, final_norm=final_norm),
        grid=(b // bb, t // tb, D_FF // tf),
        in_specs=[
            xspec,
            _mod_spec(bb, j0, 3), _mod_spec(bb, j0 + 1, 3), _mod_spec(bb, j0 + 2, 3),
            rowspec,
            pl.BlockSpec((d, tf), lambda i, k, f: (0, f)),
            pl.BlockSpec((d, tf), lambda i, k, f: (0, f)),
            pl.BlockSpec((tf, d), lambda i, k, f: (f, 0)),
            rowspec,
        ],
        out_specs=xspec,
        out_shape=jax.ShapeDtypeStruct(x.shape, x.dtype),
        scratch_shapes=[pltpu.VMEM((bb * tb, d), BF16), pltpu.VMEM((bb * tb, d), F32)],
        compiler_params=_cparams("parallel", "parallel", "arbitrary"),
        name="ffn",
    )(x, mod, mod, mod, gain, wg, wu, wd, fgain)


def _adaln_proj_kernel(x_ref, sh_ref, sc_ref, gain_ref, w_ref, *refs, tail):
    p_ref = refs[-1]
    bb, tb, d = x_ref.shape
    h = _adaln(x_ref[...], gain_ref[...], sc_ref[...], sh_ref[...])
    if tail:
        refs[0][...] = h[:, tb - SUBLANES:, :]
    hb = h.reshape(bb * tb, d).astype(BF16)
    for c0 in range(0, w_ref.shape[1], PROJ_COLS):
        p = jnp.dot(hb, w_ref[:, c0:c0 + PROJ_COLS], preferred_element_type=F32)
        p_ref[:, :, c0:c0 + PROJ_COLS] = p.reshape(bb, tb, PROJ_COLS)


def _adaln_proj_call(x, mod, gain, w, bb, tb, tail):
    b, t, d = x.shape
    n = w.shape[1]
    xspec = pl.BlockSpec((bb, tb, d), lambda i, k: (i, k, 0))
    out_specs = [pl.BlockSpec((bb, tb, n), lambda i, k: (i, k, 0))]
    out_shape = [jax.ShapeDtypeStruct((b, t, n), F32)]
    if tail:
        out_specs.insert(0, pl.BlockSpec((bb, SUBLANES, d), lambda i, k: (i, 0, 0)))
        out_shape.insert(0, jax.ShapeDtypeStruct((b, SUBLANES, d), F32))
    return pl.pallas_call(
        functools.partial(_adaln_proj_kernel, tail=tail),
        grid=(b // bb, t // tb),
        in_specs=[
            xspec, _mod_spec(bb, 3, 2), _mod_spec(bb, 4, 2),
            pl.BlockSpec((1, d), lambda i, k: (0, 0)),
            pl.BlockSpec((d, n), lambda i, k: (0, 0)),
        ],
        out_specs=out_specs,
        out_shape=out_shape,
        compiler_params=_cparams("parallel", "arbitrary"),
        name="adaln_proj",
    )(x, mod, mod, gain, w)


def _rows_proj_kernel(a_ref, w_ref, o_ref):
    o_ref[...] = _mm(a_ref[...], w_ref[...])


def _rows_proj_call(a, w, tn):
    m, k = a.shape
    n = w.shape[1]
    return pl.pallas_call(
        _rows_proj_kernel,
        grid=(n // tn,),
        in_specs=[pl.BlockSpec((m, k), lambda j: (0, 0)), pl.BlockSpec((k, tn), lambda j: (0, j))],
        out_specs=pl.BlockSpec((m, tn), lambda j: (0, j)),
        out_shape=jax.ShapeDtypeStruct((m, n), F32),
        compiler_params=_cparams("parallel"),
        name="rows_proj",
    )(a, w)


def _outproj_kernel(x_ref, oa_ref, ob_ref, gt_ref, w_ref, o_ref):
    bb, tb, d = x_ref.shape
    o = jnp.concatenate([oa_ref[...], ob_ref[...]], axis=-1).reshape(bb * tb, d)
    y = jnp.dot(o.astype(BF16), w_ref[...], preferred_element_type=F32).reshape(bb, tb, d)
    o_ref[...] = x_ref[...] + (1.0 + gt_ref[...]) * y


def _outproj_call(x, oa, ob, mod, w, bb, tb):
    b, t, d = x.shape
    xspec = pl.BlockSpec((bb, tb, d), lambda i, k: (i, k, 0))
    hspec = pl.BlockSpec((bb, tb, GROUP_W), lambda i, k: (i, k, 0))
    return pl.pallas_call(
        _outproj_kernel,
        grid=(b // bb, t // tb),
        in_specs=[xspec, hspec, hspec, _mod_spec(bb, 5, 2),
                  pl.BlockSpec((d, d), lambda i, k: (0, 0))],
        out_specs=xspec,
        out_shape=jax.ShapeDtypeStruct(x.shape, x.dtype),
        compiler_params=_cparams("parallel", "parallel"),
        name="outproj",
    )(x, oa, ob, mod, w)


def _stack_heads(x, nheads, head_w):
    lane = _iota2((1, x.shape[1]), 1)
    return jnp.concatenate(
        [jnp.where(jnp.logical_and(lane >= h * head_w, lane < (h + 1) * head_w), x, 0.0)
         for h in range(nheads)], axis=0)


def _delta_kernel(qkv_ref, z_ref, ab_ref, carry_ref, s0_ref, cw_ref, hp_ref, gain_ref,
                  o_ref, s_ref, ext_scr, s_scr, *, groups):
    t = pl.program_id(1)
    bb, n, _ = qkv_ref.shape
    rows = groups * n
    nprob = bb // groups
    hr = H_A * rows

    @pl.when(t == 0)
    def _():
        ext_scr[:, 0:SUBLANES, :] = carry_ref[...]
        s_scr[...] = s0_ref[...]

    @pl.when(t > 0)
    def _():
        ext_scr[:, 0:SUBLANES, :] = ext_scr[:, n:n + SUBLANES, :]

    ext_scr[:, SUBLANES:SUBLANES + n, :] = qkv_ref[...]
    conv = cw_ref[0:1, :] * ext_scr[:, 5:5 + n, :]
    for j in range(1, CONV_W):
        conv = conv + cw_ref[j:j + 1, :] * ext_scr[:, 5 + j:5 + j + n, :]
    x = _silu(conv).reshape(bb * n, 3 * GROUP_W)
    ab = ab_ref[...].reshape(bb * n, LANES)
    g = -jnp.exp(hp_ref[0:1, :]) * _softplus(ab + hp_ref[1:2, :])
    beta = _sigmoid(ab)
    z = z_ref[...].reshape(bb * n, GROUP_W)

    incl1, _ = _group_masks(rows, n)
    incl, strict = _group_masks(hr, n)
    lane = _iota2((1, LANES), 1)
    ones = jnp.ones((hr, LANES), F32)
    probs = range(nprob)

    def head_rows(a, h):
        return a[:, h * DK_A:(h + 1) * DK_A]

    gcol, bcol, qs, ks, vs, zs = [], [], [], [], [], []
    kst, lhs = [], []
    grow = []
    for p in probs:
        sl = slice(p * rows, (p + 1) * rows)
        gc = _mm01(incl1, g[sl], 3)
        gsel = jnp.concatenate([jnp.where(lane == h, gc, 0.0) for h in range(H_A)], axis=0)
        bsel = jnp.concatenate([jnp.where(lane == H_A + h, beta[sl], 0.0) for h in range(H_A)], axis=0)
        gcol.append(jnp.sum(gsel, axis=-1, keepdims=True))
        bcol.append(jnp.sum(bsel, axis=-1, keepdims=True))
        grow.append(_mm_nt01(ones, gsel, 3))
        xp = x[sl]
        qn = jnp.concatenate([_l2n(head_rows(xp[:, 0:GROUP_W], h)) * (DK_A ** -0.5)
                              for h in range(H_A)], axis=1)
        kn = jnp.concatenate([_l2n(head_rows(xp[:, GROUP_W:2 * GROUP_W], h)) for h in range(H_A)], axis=1)
        qs.append(jnp.concatenate([head_rows(qn, h) for h in range(H_A)], axis=0))
        ks.append(jnp.concatenate([head_rows(kn, h) for h in range(H_A)], axis=0))
        vs.append(jnp.concatenate([head_rows(xp[:, 2 * GROUP_W:], h) for h in range(H_A)], axis=0))
        zs.append(z[sl])
        k_st = _stack_heads(kn, H_A, DK_A)
        kst.append(k_st)
        lhs.append(jnp.concatenate([k_st * bcol[p], _stack_heads(qn, H_A, DK_A)], axis=0))

    kq = [_mm_nt(lhs[p], kst[p]) for p in probs]
    dec = [jnp.exp(jnp.where(incl, gcol[p] - grow[p], -jnp.inf)) for p in probs]
    a_low = [jnp.where(strict, kq[p][:hr] * dec[p], 0.0) for p in probs]
    tinv = _unit_lower_inv_many(a_low, n)
    eg = [jnp.exp(gcol[p]) for p in probs]
    kb = [ks[p] * bcol[p] for p in probs]
    sol = [_mm(tinv[p], jnp.concatenate([vs[p] * bcol[p], kb[p] * eg[p]], axis=1)) for p in probs]
    qg = [qs[p] * eg[p] for p in probs]

    us, oparts = [], []
    for p in probs:
        u_rows, o_rows = [], []
        for h in range(H_A):
            for gi in range(groups):
                r0 = h * rows + gi * n
                s = s_scr[p * groups + gi, h]
                ksq = _mm(jnp.concatenate([sol[p][r0:r0 + n, DK_A:], qg[p][r0:r0 + n]], axis=0), s)
                u_rows.append(sol[p][r0:r0 + n, :DK_A] - ksq[:n])
                o_rows.append(ksq[n:])
        us.append(jnp.concatenate(u_rows, axis=0))
        oparts.append(jnp.concatenate(o_rows, axis=0))
    outs = [oparts[p] + _mm(kq[p][hr:] * dec[p], us[p]) for p in probs]
    for p in probs:
        for h in range(H_A):
            for gi in range(groups):
                r0 = h * rows + gi * n
                glast = gcol[p][r0 + n - 1:r0 + n]
                kd = ks[p][r0:r0 + n] * jnp.exp(glast - gcol[p][r0:r0 + n])
                s = s_scr[p * groups + gi, h]
                s_scr[p * groups + gi, h] = s * jnp.exp(glast) + _mm_tn(kd, us[p][r0:r0 + n])
            o = outs[p][h * rows:(h + 1) * rows]
            val = _rms(o) * gain_ref[...] * _silu(head_rows(zs[p], h))
            o_ref[p * groups:(p + 1) * groups, :, h * DK_A:(h + 1) * DK_A] = val.reshape(groups, n, DK_A)

    @pl.when(t == pl.num_programs(1) - 1)
    def _():
        s_ref[...] = s_scr[...]


def _delta_call(proj, carry, s0, cw, hp, gain, bb, n, groups):
    b, t, _ = proj.shape
    return pl.pallas_call(
        functools.partial(_delta_kernel, groups=groups),
        grid=(b // bb, t // n),
        in_specs=[
            pl.BlockSpec((bb, n, 3 * GROUP_W), lambda i, k: (i, k, EV_QKV // (3 * GROUP_W))),
            pl.BlockSpec((bb, n, GROUP_W), lambda i, k: (i, k, EV_Z // GROUP_W)),
            pl.BlockSpec((bb, n, LANES), lambda i, k: (i, k, EV_AB // LANES)),
            pl.BlockSpec((bb, SUBLANES, 3 * GROUP_W), lambda i, k: (i, 0, EV_QKV // (3 * GROUP_W))),
            pl.BlockSpec((bb, H_A, DK_A, DK_A), lambda i, k: (i, 0, 0, 0)),
            pl.BlockSpec((SUBLANES, 3 * GROUP_W), lambda i, k: (0, 0)),
            pl.BlockSpec((SUBLANES, LANES), lambda i, k: (0, 0)),
            pl.BlockSpec((1, DK_A), lambda i, k: (0, 0)),
        ],
        out_specs=[
            pl.BlockSpec((bb, n, GROUP_W), lambda i, k: (i, k, 0)),
            pl.BlockSpec((bb, H_A, DK_A, DK_A), lambda i, k: (i, 0, 0, 0)),
        ],
        out_shape=[jax.ShapeDtypeStruct((b, t, GROUP_W), F32),
                   jax.ShapeDtypeStruct((b, H_A, DK_A, DK_A), F32)],
        scratch_shapes=[pltpu.VMEM((bb, n + SUBLANES, 3 * GROUP_W), F32),
                        pltpu.VMEM((bb, H_A, DK_A, DK_A), F32)],
        compiler_params=_cparams("parallel", "arbitrary"),
        name="delta",
    )(proj, proj, proj, carry, s0, cw, hp, gain)


def _rwkv_kernel(rkv_ref, wag_ref, crkv_ref, cwag_ref, s0_ref, mur_ref, muw_ref, wlr_ref, prm_ref,
                 y_ref, s_ref, ext1_scr, ext2_scr, sp_scr, *, groups):
    t = pl.program_id(1)
    bb, n, _ = rkv_ref.shape
    rows = groups * n
    nprob = bb // groups
    npair = H_B // 2
    pw = 2 * HD_B
    r2 = 2 * rows

    @pl.when(t == 0)
    def _():
        ext1_scr[:, 0:SUBLANES, :] = crkv_ref[...]
        ext2_scr[:, 0:SUBLANES, :] = cwag_ref[...]
        sp_scr[...] = s0_ref[...]

    @pl.when(t > 0)
    def _():
        ext1_scr[:, 0:SUBLANES, :] = ext1_scr[:, n:n + SUBLANES, :]
        ext2_scr[:, 0:SUBLANES, :] = ext2_scr[:, n:n + SUBLANES, :]

    cur1 = rkv_ref[...]
    cur2 = wag_ref[...]
    ext1_scr[:, SUBLANES:SUBLANES + n, :] = cur1
    ext2_scr[:, SUBLANES:SUBLANES + n, :] = cur2
    xm1 = cur1 + (ext1_scr[:, SUBLANES - 1:SUBLANES - 1 + n, :] - cur1) * mur_ref[...]
    xm2 = cur2 + (ext2_scr[:, SUBLANES - 1:SUBLANES - 1 + n, :] - cur2) * muw_ref[...]
    xm1 = xm1.reshape(bb * n, 3 * GROUP_W)
    xm2 = xm2.reshape(bb * n, 2 * LANES)
    r = xm1[:, 0:GROUP_W]
    kb = xm1[:, GROUP_W:2 * GROUP_W]
    vb = xm1[:, 2 * GROUP_W:3 * GROUP_W]
    lane2 = _iota2(xm2.shape, 1)
    feat = jnp.where(lane2 < 64, jnp.tanh(xm2), jnp.where(lane2 < 128, xm2, _sigmoid(xm2)))
    lr = _mm(feat, wlr_ref[...])
    w_raw = prm_ref[0:1, :] + lr[:, 0:GROUP_W]
    logw = -jnp.exp(-_softplus(-w_raw) - 0.5)
    a = _sigmoid(prm_ref[1:2, :] + lr[:, GROUP_W:2 * GROUP_W])
    gb = lr[:, 2 * GROUP_W:3 * GROUP_W]
    kkraw = kb * prm_ref[2:3, :]
    k = kb * (1.0 + (a - 1.0) * prm_ref[3:4, :])

    incl1, _ = _group_masks(rows, n)
    incl, strict = _group_masks(r2, n)
    blk = (_iota2((pw, pw), 0) & -HD_B) == (_iota2((pw, pw), 1) & -HD_B)
    blkf = blk.astype(F32)
    cums = [_mm01(incl1, logw[p * rows:(p + 1) * rows], 3) for p in range(nprob)]

    units = [(p, q) for p in range(nprob) for q in range(npair)]

    def st(x):
        return _stack_heads(x, 2, HD_B)

    pre = []
    for p, q in units:
        rs = slice(p * rows, (p + 1) * rows)
        sl = slice(q * pw, (q + 1) * pw)
        kkr = kkraw[rs, sl]
        rp, kp, vp, ap = r[rs, sl], k[rs, sl], vb[rs, sl], a[rs, sl]
        sums = _mm_x01(jnp.concatenate([kkr * kkr, rp * kp * prm_ref[4:5, sl]], axis=0), blkf, 1)
        kk = kkr * lax.rsqrt(sums[:rows] + 1e-6)
        cump = cums[p][:, sl]
        ginv = jnp.exp(-cump)
        rt = rp * jnp.exp(cump)
        at = -kk * jnp.exp(cump - logw[rs, sl])
        bt = kk * ap * ginv
        kt = kp * ginv
        pre.append(dict(rt=rt, at=at, bt=bt, kt=kt, vp=vp, cump=cump, bonus=sums[rows:] * vp,
                        vst=st(vp), gb=gb[rs, sl]))
    mats = [_mm_nt(jnp.concatenate([st(u["at"]), st(u["rt"])], axis=0),
                   jnp.concatenate([st(u["bt"]), st(u["kt"])], axis=0)) for u in pre]
    tinv = _unit_lower_inv_many([-jnp.where(strict, m[:r2, :r2], 0.0) for m in mats], n)

    u0s, y0s = [], []
    for (p, q), u in zip(units, pre):
        u0_rows, y0_rows = [], []
        for gi in range(groups):
            gs = slice(gi * n, (gi + 1) * n)
            uy = _mm_nt(jnp.concatenate([u["at"][gs], u["rt"][gs]], axis=0), sp_scr[p * groups + gi, q])
            u0_rows.append(uy[:n])
            y0_rows.append(uy[n:])
        u0s.append(jnp.concatenate(u0_rows, axis=0))
        y0s.append(jnp.concatenate(y0_rows, axis=0))
    x1 = [_mm(jnp.where(strict, m[:r2, r2:], 0.0), u["vst"]) for m, u in zip(mats, pre)]
    ust = [_mm(ti, st(u0) + x) for ti, u0, x in zip(tinv, u0s, x1)]
    yst = [_mm(jnp.concatenate([jnp.where(incl, m[r2:, :r2], 0.0), jnp.where(incl, m[r2:, r2:], 0.0)], axis=1),
               jnp.concatenate([us_, u["vst"]], axis=0)) for m, us_, u in zip(mats, ust, pre)]
    ys = [y0 + ys_[:rows] + ys_[rows:] for y0, ys_ in zip(y0s, yst)]
    means = [_mm_x01(y, blkf, 1) * (1.0 / HD_B) for y in ys]
    ycs = [y - m for y, m in zip(ys, means)]
    variances = [_mm_x01(yc * yc, blkf, 1) * (1.0 / HD_B) for yc in ycs]

    for (p, q), u, us_, yc, var in zip(units, pre, ust, ycs, variances):
        sl = slice(q * pw, (q + 1) * pw)
        uu = us_[:rows] + us_[rows:]
        for gi in range(groups):
            gs = slice(gi * n, (gi + 1) * n)
            upd = _mm_tn(jnp.concatenate([uu[gs], u["vp"][gs]], axis=0),
                         jnp.concatenate([u["bt"][gs], u["kt"][gs]], axis=0))
            glast = jnp.exp(u["cump"][gi * n + n - 1:gi * n + n, :])
            sp = sp_scr[p * groups + gi, q]
            sp_scr[p * groups + gi, q] = jnp.where(blk, sp + upd, 0.0) * glast
        yn = yc * lax.rsqrt(var + GN_EPS) * prm_ref[5:6, sl] + prm_ref[6:7, sl]
        val = (yn + u["bonus"]) * u["gb"]
        y_ref[p * groups:(p + 1) * groups, :, sl] = val.reshape(groups, n, pw)

    @pl.when(t == pl.num_programs(1) - 1)
    def _():
        s_ref[...] = sp_scr[...]


def _rwkv_call(proj, carry, s0, mur, muw, wlr, prm, bb, n, groups):
    b, t, _ = proj.shape
    wag_w = 2 * LANES
    npair, pw = H_B // 2, 2 * HD_B
    return pl.pallas_call(
        functools.partial(_rwkv_kernel, groups=groups),
        grid=(b // bb, t // n),
        in_specs=[
            pl.BlockSpec((bb, n, 3 * GROUP_W), lambda i, k: (i, k, EV_RKV // (3 * GROUP_W))),
            pl.BlockSpec((bb, n, wag_w), lambda i, k: (i, k, EV_WAG // wag_w)),
            pl.BlockSpec((bb, SUBLANES, 3 * GROUP_W), lambda i, k: (i, 0, EV_RKV // (3 * GROUP_W))),
            pl.BlockSpec((bb, SUBLANES, wag_w), lambda i, k: (i, 0, EV_WAG // wag_w)),
            pl.BlockSpec((bb, npair, pw, pw), lambda i, k: (i, 0, 0, 0)),
            pl.BlockSpec((1, 3 * GROUP_W), lambda i, k: (0, 0)),
            pl.BlockSpec((1, wag_w), lambda i, k: (0, 0)),
            pl.BlockSpec((wag_w, 3 * GROUP_W), lambda i, k: (0, 0)),
            pl.BlockSpec((SUBLANES, GROUP_W), lambda i, k: (0, 0)),
        ],
        out_specs=[
            pl.BlockSpec((bb, n, GROUP_W), lambda i, k: (i, k, 0)),
            pl.BlockSpec((bb, npair, pw, pw), lambda i, k: (i, 0, 0, 0)),
        ],
        out_shape=[jax.ShapeDtypeStruct((b, t, GROUP_W), F32),
                   jax.ShapeDtypeStruct((b, npair, pw, pw), F32)],
        scratch_shapes=[pltpu.VMEM((bb, n + SUBLANES, 3 * GROUP_W), F32),
                        pltpu.VMEM((bb, n + SUBLANES, wag_w), F32),
                        pltpu.VMEM((bb, npair, pw, pw), F32)],
        compiler_params=_cparams("parallel", "arbitrary"),
        name="rwkv7",
    )(proj, proj, carry, carry, s0, mur, muw, wlr, prm)


def _rwkv_state_to_pairs(s):
    b = s.shape[0]
    s = s.reshape(b, H_B // 2, 2, HD_B, HD_B)
    zero = jnp.zeros_like(s[:, :, 0])
    top = jnp.concatenate([s[:, :, 0], zero], axis=-1)
    bot = jnp.concatenate([zero, s[:, :, 1]], axis=-1)
    return jnp.concatenate([top, bot], axis=-2)


def _rwkv_state_from_pairs(sp):
    b = sp.shape[0]
    heads = jnp.stack([sp[:, :, :HD_B, :HD_B], sp[:, :, HD_B:, HD_B:]], axis=2)
    return heads.reshape(b, H_B, HD_B, HD_B)


def _stack_pair(x):
    first = (_iota2((1, x.shape[1]), 1) & 64) == 0
    return jnp.concatenate([jnp.where(first, x, 0.0), jnp.where(first, 0.0, x)], axis=0)


def _group_rows(a, rows, n, gi):
    return jnp.concatenate([a[gi * n:(gi + 1) * n], a[rows + gi * n:rows + (gi + 1) * n]], axis=0)


def _ungroup_rows(pieces, n):
    return jnp.concatenate([p[:n] for p in pieces] + [p[n:] for p in pieces], axis=0)


def _gla_kernel(qk_ref, v_ref, z_ref, gkin_ref, s0_ref, wgk_ref, bgk_ref, gain_ref,
                o_ref, s_ref, s_scr, *, groups):
    t = pl.program_id(1)
    bb, n, _ = qk_ref.shape
    rows = groups * n
    nprob = bb // groups
    npair = H_C // 2
    pw = 2 * DK_C
    qkw = H_C * DK_C
    r2 = 2 * rows
    sub = min(GLA_SUB, n)
    nslab = n // sub
    assert groups == 1 or nslab == 1

    @pl.when(t == 0)
    def _():
        s_scr[...] = s0_ref[...]

    qk = qk_ref[...].reshape(bb * n, 2 * qkw)
    q = qk[:, :qkw] * (DK_C ** -0.5)
    k = qk[:, qkw:]
    v = v_ref[...].reshape(bb * n, GROUP_W)
    z = z_ref[...].reshape(bb * n, GROUP_W)
    gk = _log_sigmoid(_mm(gkin_ref[...].reshape(bb * n, LANES), wgk_ref[...]) + bgk_ref[...]) * (1.0 / GLA_NORM)
    incl1, _ = _group_masks(rows, n)
    incl2, _ = _group_masks(r2, n)
    row_t = _iota2((rows, 1), 0) & (n - 1)
    eye_p = _eye(pw)
    cums = [_mm01(incl1, gk[p * rows:(p + 1) * rows], 3) for p in range(nprob)]
    units = [(p, u) for p in range(nprob) for u in range(npair)]

    pre = []
    for p, u in units:
        rs = slice(p * rows, (p + 1) * rows)
        sl = slice(u * pw, (u + 1) * pw)
        bcum, qp, kp = cums[p][:, sl], q[rs, sl], k[rs, sl]
        qparts, kparts = [], []
        for s in range(nslab):
            rho = bcum[s * sub - 1:s * sub, :] if s > 0 else jnp.zeros((1, pw), F32)
            in_slab = jnp.logical_and(row_t >= s * sub, row_t < (s + 1) * sub)
            qparts.append(qp * jnp.exp(jnp.where(in_slab, bcum - rho, -jnp.inf)))
            kparts.append(kp * jnp.exp(jnp.where(row_t < (s + 1) * sub, rho - bcum, -jnp.inf)))
        blast = jnp.concatenate(
            [jnp.broadcast_to(bcum[gi * n + n - 1:gi * n + n, :], (n, pw)) for gi in range(groups)], axis=0)
        vst = jnp.concatenate([v[rs, (2 * u + j) * DV_C:(2 * u + j + 1) * DV_C] for j in range(2)], axis=0)
        pre.append(dict(qcat=_stack_pair(jnp.concatenate(qparts, axis=1)),
                        kcat=_stack_pair(jnp.concatenate(kparts, axis=1)),
                        qdb=_stack_pair(qp * jnp.exp(bcum)), kd=_stack_pair(kp * jnp.exp(blast - bcum)),
                        vst=vst, bcum=bcum))
    att = [jnp.where(incl2, _mm_nt(u["qcat"], u["kcat"]), 0.0) for u in pre]
    intra = [_mm(a, u["vst"]) for a, u in zip(att, pre)]
    inter = []
    for (p, uidx), u in zip(units, pre):
        pieces = [_mm(_group_rows(u["qdb"], rows, n, gi), s_scr[p * groups + gi, uidx])
                  for gi in range(groups)]
        inter.append(_ungroup_rows(pieces, n))
    pick_last = (_iota2((BF16_ROWS, rows), 1)
                 == (_iota2((BF16_ROWS, rows), 0) & (groups - 1)) * n + n - 1).astype(F32)
    glast = [jnp.exp(_mm01(pick_last, u["bcum"], 3)) for u in pre]
    glcols = [_mm_nt01(eye_p, g, 3) for g in glast]
    for (p, uidx), u, o_in, o_x, glc in zip(units, pre, intra, inter, glcols):
        for gi in range(groups):
            upd = _mm_tn(_group_rows(u["kd"], rows, n, gi), _group_rows(u["vst"], rows, n, gi))
            s_scr[p * groups + gi, uidx] = s_scr[p * groups + gi, uidx] * glc[:, gi:gi + 1] + upd
        o = o_in + o_x
        for j in range(2):
            h = 2 * uidx + j
            zh = z[p * rows:(p + 1) * rows, h * DV_C:(h + 1) * DV_C]
            val = _rms(o[j * rows:(j + 1) * rows]) * gain_ref[...] * _silu(zh)
            o_ref[p * groups:(p + 1) * groups, :, h * DV_C:(h + 1) * DV_C] = val.reshape(groups, n, DV_C)

    @pl.when(t == pl.num_programs(1) - 1)
    def _():
        s_ref[...] = s_scr[...]


def _gla_call(proj, s0, wgk, bgk, gain, bb, n, groups):
    b, t, _ = proj.shape
    npair, pw = H_C // 2, 2 * DK_C
    return pl.pallas_call(
        functools.partial(_gla_kernel, groups=groups),
        grid=(b // bb, t // n),
        in_specs=[
            pl.BlockSpec((bb, n, GROUP_W), lambda i, k: (i, k, OD_CQK // GROUP_W)),
            pl.BlockSpec((bb, n, GROUP_W), lambda i, k: (i, k, OD_CV // GROUP_W)),
            pl.BlockSpec((bb, n, GROUP_W), lambda i, k: (i, k, OD_CZ // GROUP_W)),
            pl.BlockSpec((bb, n, LANES), lambda i, k: (i, k, OD_CGK // LANES)),
            pl.BlockSpec((bb, npair, DV_C, pw), lambda i, k: (i, 0, 0, 0)),
            pl.BlockSpec((LANES, H_C * DK_C), lambda i, k: (0, 0)),
            pl.BlockSpec((1, H_C * DK_C), lambda i, k: (0, 0)),
            pl.BlockSpec((1, DV_C), lambda i, k: (0, 0)),
        ],
        out_specs=[
            pl.BlockSpec((bb, n, GROUP_W), lambda i, k: (i, k, 0)),
            pl.BlockSpec((bb, npair, DV_C, pw), lambda i, k: (i, 0, 0, 0)),
        ],
        out_shape=[jax.ShapeDtypeStruct((b, t, GROUP_W), F32),
                   jax.ShapeDtypeStruct((b, npair, DV_C, pw), F32)],
        scratch_shapes=[pltpu.VMEM((bb, npair, DV_C, pw), F32)],
        compiler_params=_cparams("parallel", "arbitrary"),
        name="gla",
    )(proj, proj, proj, proj, s0, wgk, bgk, gain)


def _mlstm_kernel(qk_ref, v_ref, og_ref, if_ref, c0_ref, n0_ref, m0_ref, bif_ref, gain_ref,
                  h_ref, c_ref, nn_ref, m_ref, c_scr, n_scr, m_scr, *, groups):
    t = pl.program_id(1)
    bb, n, _ = qk_ref.shape
    rows = groups * n
    nprob = bb // groups
    npair = H_D // 2
    pw = 2 * DK_D
    qkw = H_D * DK_D
    r2 = 2 * rows

    @pl.when(t == 0)
    def _():
        c_scr[...] = c0_ref[...]
        n_scr[...] = n0_ref[...]
        m_scr[...] = m0_ref[...]

    x = if_ref[...].reshape(bb * n, LANES) + bif_ref[...]
    lf = _log_sigmoid(x)
    qk = qk_ref[...].reshape(bb * n, 2 * qkw)
    q = qk[:, :qkw]
    k = qk[:, qkw:] * (DK_D ** -0.5)
    v = v_ref[...].reshape(bb * n, GROUP_W)
    og = og_ref[...].reshape(bb * n, GROUP_W)
    incl1, _ = _group_masks(rows, n)
    incl2, _ = _group_masks(r2, n)
    r_i, c_i = _iota2((r2, r2), 0), _iota2((r2, r2), 1)
    same2 = (r_i & -n) == (c_i & -n)
    last2 = jnp.logical_and(same2, (c_i & (n - 1)) == n - 1)
    lane = _iota2((1, LANES), 1)
    ones = jnp.ones((r2, LANES), F32)
    ones_sq = jnp.ones((LANES, LANES), F32)
    fcums = [_mm01(incl1, lf[p * rows:(p + 1) * rows], 3) for p in range(nprob)]
    m_rows = [m_scr[b_] for b_ in range(bb)]
    m_old = list(m_rows)
    units = [(p, u) for p in range(nprob) for u in range(npair)]

    def per_block(fn):
        return jnp.concatenate([jnp.broadcast_to(fn(j, gi), (n, fn(j, gi).shape[1]))
                                for j in range(2) for gi in range(groups)], axis=0)

    pre = []
    for p, u in units:
        rs = slice(p * rows, (p + 1) * rows)
        sl = slice(u * pw, (u + 1) * pw)
        fsel = jnp.concatenate([jnp.where(lane == H_D + 2 * u + j, fcums[p], 0.0) for j in range(2)], axis=0)
        isel = jnp.concatenate([jnp.where(lane == 2 * u + j, x[rs], 0.0) for j in range(2)], axis=0)
        cols = _mm_x01(jnp.concatenate([fsel, isel], axis=0), ones_sq, 3)
        fcol, icol = cols[:r2], cols[r2:]
        drow = _mm_nt01(ones, isel - fsel, 3)
        flast = _mm01(last2, fcol, 3)
        mprev = per_block(lambda j, gi: m_old[p * groups + gi][:, 2 * u + j:2 * u + j + 1])
        nmat = per_block(lambda j, gi: n_scr[p * groups + gi][:, sl])
        log_d = jnp.where(incl2, fcol + drow, -jnp.inf)
        m_in = jnp.max(log_d, axis=-1, keepdims=True)
        m_e = jnp.max(jnp.where(same2, flast + drow, -jnp.inf), axis=-1, keepdims=True)
        m_t = jnp.maximum(fcol + mprev, m_in)
        w_in = jnp.exp(fcol + mprev - m_t)
        m_new = jnp.maximum(flast + mprev, m_e)
        qst, kst = _stack_pair(q[rs, sl]), _stack_pair(k[rs, sl])
        vst = jnp.concatenate([v[rs, (2 * u + j) * DV_D:(2 * u + j + 1) * DV_D] for j in range(2)], axis=0)
        pre.append(dict(qst=qst, kst=kst, vst=vst, log_d=log_d, m_t=m_t, w_in=w_in, m_new=m_new,
                        cd=jnp.exp(flast + mprev - m_new), nmat=nmat,
                        ke=kst * jnp.exp(flast - fcol + icol - m_new)))
    dms = [jnp.exp(u["log_d"] - u["m_t"]) * _mm_nt(u["qst"], u["kst"]) for u in pre]
    intra = [_mm(dm, u["vst"]) for dm, u in zip(dms, pre)]
    inter = []
    for (p, uidx), u in zip(units, pre):
        qw = u["w_in"] * u["qst"]
        pieces = [_mm(_group_rows(qw, rows, n, gi), c_scr[p * groups + gi, uidx]) for gi in range(groups)]
        inter.append(_ungroup_rows(pieces, n))
    for (p, uidx), u, dm, o_in, o_x in zip(units, pre, dms, intra, inter):
        sl = slice(uidx * pw, (uidx + 1) * pw)
        den = (u["w_in"] * _mm_x01(u["qst"] * u["nmat"], ones_sq, 1)
               + _mm_x01(dm, ones_sq, 2))
        hout = (o_in + o_x) / jnp.maximum(jnp.abs(den), jnp.exp(-u["m_t"]))
        for gi in range(groups):
            b_ = p * groups + gi
            r0, r1 = gi * n, rows + gi * n
            cdrow = jnp.where(lane < DK_D, u["cd"][r0:r0 + 1], u["cd"][r1:r1 + 1])
            ke_g = _group_rows(u["ke"], rows, n, gi)
            cdcol = jnp.where(_iota2((pw, 1), 0) < DK_D, u["cd"][r0:r0 + 1], u["cd"][r1:r1 + 1])
            c_scr[b_, uidx] = c_scr[b_, uidx] * cdcol + _mm_tn(ke_g, _group_rows(u["vst"], rows, n, gi))
            n_scr[b_, :, sl] = n_scr[b_][:, sl] * cdrow + jnp.sum(ke_g, axis=0, keepdims=True)
            for j, r in ((0, r0), (1, r1)):
                m_rows[b_] = jnp.where(lane == 2 * uidx + j, u["m_new"][r:r + 1], m_rows[b_])
        for j in range(2):
            h = 2 * uidx + j
            ogh = og[p * rows:(p + 1) * rows, h * DV_D:(h + 1) * DV_D]
            val = _sigmoid(ogh) * (_rms(hout[j * rows:(j + 1) * rows]) * gain_ref[...])
            h_ref[p * groups:(p + 1) * groups, :, h * DV_D:(h + 1) * DV_D] = val.reshape(groups, n, DV_D)
    for b_ in range(bb):
        m_scr[b_] = m_rows[b_]

    @pl.when(t == pl.num_programs(1) - 1)
    def _():
        c_ref[...] = c_scr[...]
        nn_ref[...] = n_scr[...]
        m_ref[...] = m_scr[...]


def _mlstm_call(proj, c0, n0, m0, bif, gain, bb, n, groups):
    b, t, _ = proj.shape
    qkw = H_D * DK_D
    npair, pw = H_D // 2, 2 * DK_D
    return pl.pallas_call(
        functools.partial(_mlstm_kernel, groups=groups),
        grid=(b // bb, t // n),
        in_specs=[
            pl.BlockSpec((bb, n, GROUP_W), lambda i, k: (i, k, OD_DQK // GROUP_W)),
            pl.BlockSpec((bb, n, GROUP_W), lambda i, k: (i, k, OD_DV // GROUP_W)),
            pl.BlockSpec((bb, n, GROUP_W), lambda i, k: (i, k, OD_DO // GROUP_W)),
            pl.BlockSpec((bb, n, LANES), lambda i, k: (i, k, OD_DIF // LANES)),
            pl.BlockSpec((bb, npair, DV_D, pw), lambda i, k: (i, 0, 0, 0)),
            pl.BlockSpec((bb, 1, qkw), lambda i, k: (i, 0, 0)),
            pl.BlockSpec((bb, 1, LANES), lambda i, k: (i, 0, 0)),
            pl.BlockSpec((1, LANES), lambda i, k: (0, 0)),
            pl.BlockSpec((1, DV_D), lambda i, k: (0, 0)),
        ],
        out_specs=[
            pl.BlockSpec((bb, n, GROUP_W), lambda i, k: (i, k, 0)),
            pl.BlockSpec((bb, npair, DV_D, pw), lambda i, k: (i, 0, 0, 0)),
            pl.BlockSpec((bb, 1, qkw), lambda i, k: (i, 0, 0)),
            pl.BlockSpec((bb, 1, LANES), lambda i, k: (i, 0, 0)),
        ],
        out_shape=[jax.ShapeDtypeStruct((b, t, GROUP_W), F32),
                   jax.ShapeDtypeStruct((b, npair, DV_D, pw), F32),
                   jax.ShapeDtypeStruct((b, 1, qkw), F32),
                   jax.ShapeDtypeStruct((b, 1, LANES), F32)],
        scratch_shapes=[pltpu.VMEM((bb, npair, DV_D, pw), F32),
                        pltpu.VMEM((bb, 1, qkw), F32),
                        pltpu.VMEM((bb, 1, LANES), F32)],
        compiler_params=_cparams("parallel", "arbitrary"),
        name="mlstm",
    )(proj, proj, proj, proj, c0, n0, m0, bif, gain)


def _pad_cols(w, width):
    return jnp.pad(w, ((0, 0), (0, width - w.shape[1])))


def _pad_rows(w, rows, at=0):
    return jnp.pad(w, ((at, rows - at - w.shape[0]), (0, 0)))


def _even_in_weight(w):
    pa, pb = w[:, :2056], w[:, 2056:]
    cols = [
        pb[:, 0:1536],
        pa[:, 0:1536],
        pa[:, 1544:2056],
        pb[:, 1536:1792],
        _pad_cols(pa[:, 1536:1544], LANES),
    ]
    return _pad_cols(jnp.concatenate(cols, axis=1), EV_COLS).astype(BF16)


def _odd_in_weight(w):
    pc, pd = w[:, :1552], w[:, 1552:]
    cols = [
        pc[:, 0:512],
        pd[:, 0:512],
        pc[:, 512:1024],
        pc[:, 1040:1552],
        pd[:, 512:1024],
        pd[:, 1032:1544],
        _pad_cols(pc[:, 1024:1040], LANES),
        _pad_cols(pd[:, 1024:1032], LANES),
    ]
    return jnp.concatenate(cols, axis=1).astype(BF16)


def _row(v):
    return v.reshape(1, -1).astype(F32)


def _tiles(x, rows=1024):
    b, t, _ = x.shape
    tb = min(t, rows)
    bb = min(b, rows // tb)
    return bb, tb


def _trunk(x, mod, states, wts):
    xbuf, s_delta, s_rwkv, s_gla, s_mc, s_mn, s_mm = states
    b, t, _ = x.shape
    bb, tb = _tiles(x)
    fbb, ftb = _tiles(x, FFN_ROWS)
    pbb, ptb = _tiles(x, PROJ_ROWS)
    n = min(CHUNK, t)
    groups = CHUNK // n
    rbb = (PROBLEMS_LONG if t > CHUNK else PROBLEMS) * groups
    new_even = ([], [], [])
    new_odd = ([], [], [], [])
    for l in range(DEPTH):
        lw = wts["layers"][l]
        m_l = mod[l]
        x = _ffn_call(x, m_l, lw["gain0"], lw["wg0"], lw["wu0"], lw["wd0"], wts["final_gain"], 0, fbb, ftb,
                      FFN_COLS, False)
        i = l // 2
        if l % 2 == 0:
            h_tail, proj = _adaln_proj_call(x, m_l, lw["gain1"], lw["w_in"], pbb, ptb, True)
            if xbuf is None:
                carry = jnp.zeros((b, SUBLANES, EV_COLS), F32)
            else:
                rows = _rows_proj_call(xbuf[i].reshape(b * (CONV_W - 1), D_MODEL), lw["w_in"], 1024)
                carry = jnp.pad(rows.reshape(b, CONV_W - 1, EV_COLS),
                                ((0, 0), (SUBLANES - CONV_W + 1, 0), (0, 0)))
            oa, sd = _delta_call(proj, carry, s_delta[i], lw["conv_w"], lw["delta_hp"], lw["gain_a"],
                                 rbb, n, groups)
            ob, sr = _rwkv_call(proj, carry, _rwkv_state_to_pairs(s_rwkv[i]), lw["mu_rkv"], lw["mu_wag"],
                                lw["w_lora"], lw["rwkv_prm"], rbb, n, groups)
            for lst, val in zip(new_even, (h_tail[:, SUBLANES - (CONV_W - 1):], sd, _rwkv_state_from_pairs(sr))):
                lst.append(val)
        else:
            (proj,) = _adaln_proj_call(x, m_l, lw["gain1"], lw["w_in"], pbb, ptb, False)
            pair_shape = (b, H_C // 2, 2 * DK_C, DV_C)
            oa, sg = _gla_call(proj, s_gla[i].reshape(pair_shape), lw["w_gk2"], lw["b_gk"], lw["gain_c"],
                               rbb, n, groups)
            ob, sc, sn, sm = _mlstm_call(proj, s_mc[i].reshape(pair_shape), s_mn[i].reshape(b, 1, H_D * DK_D),
                                         _pad_cols(s_mm[i], LANES).reshape(b, 1, LANES),
                                         lw["b_if"], lw["gain_d"], rbb, n, groups)
            for lst, val in zip(new_odd, (sg.reshape(b, H_C, DK_C, DV_C), sc.reshape(b, H_D, DK_D, DV_D),
                                          sn.reshape(b, H_D, DK_D), sm[:, 0, :H_D])):
                lst.append(val)
        x = _outproj_call(x, oa, ob, m_l, lw["w_out"], bb, tb)
        x = _ffn_call(x, m_l, lw["gain2"], lw["wg1"], lw["wu1"], lw["wd1"], wts["final_gain"], 6, fbb, ftb,
                      FFN_COLS, l == DEPTH - 1)
    stacked = [jnp.stack(lst) for lst in new_even + new_odd]
    return (x, *stacked)


def _prepare_weights(norm_gain, final_gain, w_ffn_gate, w_ffn_up, w_ffn_down, w_in_even, w_out_even,
                     conv_w, a_log, dt_bias, gain_a, mu_b, w0_b, w_w2, a0_b, w_a2, w_g2, k_k, k_a, r_k,
                     lnx_gain, lnx_bias, w_in_odd, w_out_odd, w_gk2, b_gk, gain_c, b_i, b_f, gain_d):
    layers = []
    for l in range(DEPTH):
        i = l // 2
        lw = {
            "gain0": _row(norm_gain[l, 0]), "gain1": _row(norm_gain[l, 1]), "gain2": _row(norm_gain[l, 2]),
            "wg0": w_ffn_gate[l, 0].astype(BF16), "wu0": w_ffn_up[l, 0].astype(BF16),
            "wd0": w_ffn_down[l, 0].astype(BF16),
            "wg1": w_ffn_gate[l, 1].astype(BF16), "wu1": w_ffn_up[l, 1].astype(BF16),
            "wd1": w_ffn_down[l, 1].astype(BF16),
        }
        if l % 2 == 0:
            lw["w_in"] = _even_in_weight(w_in_even[i])
            lw["w_out"] = w_out_even[i].astype(BF16)
            lw["conv_w"] = _pad_rows(conv_w[i].astype(F32), SUBLANES)
            lw["delta_hp"] = _pad_rows(jnp.stack([_pad_cols(_row(a_log[i]), LANES)[0],
                                                  _pad_cols(_row(dt_bias[i]), LANES)[0]]), SUBLANES)
            lw["gain_a"] = _row(gain_a[i])
            lw["mu_rkv"] = _row(mu_b[i, :1536])
            lw["mu_wag"] = _row(mu_b[i, 1536:])
            lora = jnp.zeros((2 * LANES, 3 * GROUP_W), F32)
            lora = lora.at[0:64, 0:GROUP_W].set(w_w2[i])
            lora = lora.at[64:128, GROUP_W:2 * GROUP_W].set(w_a2[i])
            lora = lora.at[128:256, 2 * GROUP_W:].set(w_g2[i])
            lw["w_lora"] = lora.astype(BF16)
            lw["rwkv_prm"] = jnp.stack([w0_b[i], a0_b[i], k_k[i], k_a[i], r_k[i].reshape(-1),
                                        lnx_gain[i], lnx_bias[i], jnp.zeros_like(w0_b[i])]).astype(F32)
        else:
            lw["w_in"] = _odd_in_weight(w_in_odd[i])
            lw["w_out"] = w_out_odd[i].astype(BF16)
            lw["w_gk2"] = _pad_rows(w_gk2[i], LANES).astype(BF16)
            lw["b_gk"] = _row(b_gk[i])
            lw["gain_c"] = _row(gain_c[i])
            lw["b_if"] = _pad_cols(_row(jnp.concatenate([b_i[i], b_f[i]])), LANES)
            lw["gain_d"] = _row(gain_d[i])
        layers.append(lw)
    return {"layers": layers, "final_gain": _row(final_gain)}


def kernel(x_prompt, x_sample, c_prompt, c_sample, state_xbuf_even, state_delta, state_rwkv, state_gla,
           state_mlstm_c, state_mlstm_n, state_mlstm_m, w_mod, b_mod, norm_gain, final_gain, w_ffn_gate,
           w_ffn_up, w_ffn_down, w_in_even, w_out_even, conv_w, a_log, dt_bias, gain_a, mu_b, w0_b, w_w2,
           a0_b, w_a2, w_g2, k_k, k_a, r_k, lnx_gain, lnx_bias, w_in_odd, w_out_odd, w_gk2, b_gk, gain_c,
           b_i, b_f, gain_d):
    wts = _prepare_weights(norm_gain, final_gain, w_ffn_gate, w_ffn_up, w_ffn_down, w_in_even, w_out_even,
                           conv_w, a_log, dt_bias, gain_a, mu_b, w0_b, w_w2, a0_b, w_a2, w_g2, k_k, k_a,
                           r_k, lnx_gain, lnx_bias, w_in_odd, w_out_odd, w_gk2, b_gk, gain_c, b_i, b_f,
                           gain_d)
    bp, bs = x_prompt.shape[0], x_sample.shape[0]
    c_all = jnp.concatenate([c_prompt, c_sample], axis=0).astype(F32)
    mod = _mod_call(c_all, w_mod, b_mod).reshape(DEPTH, bp + bs, N_MOD, 1, D_MODEL)
    mod_p, mod_s = mod[:, :bp], mod[:, bp:]

    def zeros(shape):
        return jnp.zeros(shape, F32)

    zero_states = (
        None,
        zeros((N_EVEN, bp, H_A, DK_A, DK_A)),
        zeros((N_EVEN, bp, H_B, HD_B, HD_B)),
        zeros((N_ODD, bp, H_C, DK_C, DV_C)),
        zeros((N_ODD, bp, H_D, DK_D, DV_D)),
        zeros((N_ODD, bp, H_D, DK_D)),
        zeros((N_ODD, bp, H_D)),
    )
    y_p, xb_p, dl_p, rw_p, gl_p, mc_p, mn_p, mm_p = _trunk(x_prompt, mod_p, zero_states, wts)
    sample_states = (state_xbuf_even, state_delta, state_rwkv, state_gla,
                     state_mlstm_c, state_mlstm_n, state_mlstm_m)
    y_s, xb_s, dl_s, rw_s, gl_s, mc_s, mn_s, mm_s = _trunk(x_sample, mod_s, sample_states, wts)
    return (y_p, y_s, xb_p, xb_s, dl_p, dl_s, rw_p, rw_s, gl_p, gl_s, mc_p, mc_s, mn_p, mn_s, mm_p, mm_s)
```

```python
import functools

import jax
import jax.numpy as jnp
from jax import lax
from jax.experimental import pallas as pl
from jax.experimental.pallas import tpu as pltpu

F32 = jnp.float32
BF16 = jnp.bfloat16
HIGHEST = lax.Precision.HIGHEST

D_MODEL = 1024
DEPTH = 4
N_EVEN = 2
N_ODD = 2
D_FF = 2816
N_MOD = 9
EPS = 1e-6
GN_EPS = 64e-5
CONV_W = 4
H_A, DK_A = 4, 128
H_B, HD_B = 8, 64
H_C, DK_C, DV_C = 4, 64, 128
H_D, DK_D, DV_D = 4, 64, 128
GLA_NORM = 16.0
GROUP_W = 512

LANES = 128
SUBLANES = 8
BF16_ROWS = 16
VMEM_LIMIT_BYTES = 48 * 1024 * 1024

EV_RKV, EV_QKV, EV_Z, EV_WAG, EV_AB, EV_COLS = 0, 1536, 3072, 3584, 3840, 4096
OD_CQK, OD_DQK, OD_CV, OD_CZ, OD_DV, OD_DO, OD_CGK, OD_DIF, OD_COLS = (
    0, 512, 1024, 1536, 2048, 2560, 3072, 3200, 3328)

CHUNK = 64
GLA_SUB = 16
PROJ_ROWS = 512
PROJ_COLS = 256
FFN_ROWS = 1024
FFN_COLS = 256
PROBLEMS = 2
PROBLEMS_LONG = 8


def _bf(x):
    if x.dtype == BF16:
        return x
    if x.shape[-2] % BF16_ROWS == 0 and x.shape[-1] % BF16_ROWS == 0:
        return x.astype(BF16)
    return x


def _pair(a, b):
    a, b = _bf(a), _bf(b)
    if a.dtype != b.dtype:
        a, b = a.astype(F32), b.astype(F32)
    return a, b


def _mm(a, b):
    a, b = _pair(a, b)
    return jnp.dot(a, b, preferred_element_type=F32)


def _mm_nt(a, b):
    a, b = _pair(a, b)
    return lax.dot_general(a, b, (((1,), (1,)), ((), ())), preferred_element_type=F32)


def _mm_tn(a, b):
    a, b = _pair(a, b)
    return lax.dot_general(a, b, (((0,), (0,)), ((), ())), preferred_element_type=F32)


def _mm_f32(a, b):
    return jnp.dot(a, b, precision=HIGHEST, preferred_element_type=F32)


def _mm_nt_f32(a, b):
    return lax.dot_general(a, b, (((1,), (1,)), ((), ())), precision=HIGHEST,
                           preferred_element_type=F32)


def _split_bf16(x, parts):
    out, r = [], x
    for i in range(parts):
        p = r.astype(BF16)
        out.append(p)
        if i + 1 < parts:
            r = r - p.astype(F32)
    return out


def _mm01(a01, x, parts):
    a = a01.astype(BF16)
    acc = None
    for p in _split_bf16(x, parts):
        d = jnp.dot(a, p, preferred_element_type=F32)
        acc = d if acc is None else acc + d
    return acc


def _mm_x01(x, b01, parts):
    b = b01.astype(BF16)
    acc = None
    for p in _split_bf16(x, parts):
        d = jnp.dot(p, b, preferred_element_type=F32)
        acc = d if acc is None else acc + d
    return acc


def _mm_nt01(a01, x, parts):
    a = a01.astype(BF16)
    acc = None
    for p in _split_bf16(x, parts):
        d = lax.dot_general(a, p, (((1,), (1,)), ((), ())), preferred_element_type=F32)
        acc = d if acc is None else acc + d
    return acc


def _sigmoid(x):
    return jax.nn.sigmoid(x)


def _silu(x):
    return x * jax.nn.sigmoid(x)


def _softplus(x):
    return jnp.maximum(x, 0.0) + jnp.log1p(jnp.exp(-jnp.abs(x)))


def _log_sigmoid(x):
    return -_softplus(-x)


def _rms(x, eps=EPS):
    return x * lax.rsqrt(jnp.mean(x * x, axis=-1, keepdims=True) + eps)


def _l2n(x):
    return x * lax.rsqrt(jnp.sum(x * x, axis=-1, keepdims=True) + 1e-6)


def _iota2(shape, dim):
    return lax.broadcasted_iota(jnp.int32, shape, dim)


def _tril_masks(n):
    r, c = _iota2((n, n), 0), _iota2((n, n), 1)
    return r >= c, r > c


def _group_masks(size, n):
    r, c = _iota2((size, size), 0), _iota2((size, size), 1)
    same = (r & -n) == (c & -n)
    return jnp.logical_and(same, r >= c), jnp.logical_and(same, r > c)


def _eye(n):
    return (_iota2((n, n), 0) == _iota2((n, n), 1)).astype(F32)


def _transpose_rows(x):
    return _mm_nt_f32(_eye(x.shape[1]), x)


def _unit_lower_inv_many(mats, n):
    eye = _eye(mats[0].shape[0])
    ms = [-a for a in mats]
    ps = [eye + m for m in ms]
    covered = 2
    while covered < n:
        ms = [_mm(m, m) for m in ms]
        ps = [p + _mm(p, m) for p, m in zip(ps, ms)]
        covered *= 2
    return ps


def _cparams(*sem):
    return pltpu.CompilerParams(dimension_semantics=sem, vmem_limit_bytes=VMEM_LIMIT_BYTES)


def _mod_kernel(c_ref, w_ref, b_ref, o_ref):
    cs = _silu(c_ref[...])
    o_ref[...] = _mm(cs, w_ref[...]) + b_ref[...]


def _mod_call(c_all, w_mod, b_mod):
    rows = c_all.shape[0]
    tn = 1024
    width = N_MOD * D_MODEL
    return pl.pallas_call(
        _mod_kernel,
        grid=(DEPTH, width // tn),
        in_specs=[
            pl.BlockSpec((rows, D_MODEL), lambda l, j: (0, 0)),
            pl.BlockSpec((None, D_MODEL, tn), lambda l, j: (l, 0, j)),
            pl.BlockSpec((None, 1, tn), lambda l, j: (l, 0, j)),
        ],
        out_specs=pl.BlockSpec((None, rows, tn), lambda l, j: (l, 0, j)),
        out_shape=jax.ShapeDtypeStruct((DEPTH, rows, width), F32),
        compiler_params=_cparams("parallel", "parallel"),
        name="mod",
    )(c_all, w_mod, b_mod.reshape(DEPTH, 1, width))


def _adaln(x, gain, scale, shift):
    return _rms(x) * gain * (1.0 + scale) + shift


def _ffn_kernel(x_ref, sh_ref, sc_ref, gt_ref, gain_ref, wg_ref, wu_ref, wd_ref, fgain_ref, *refs,
                final_norm, mixer_in):
    if mixer_in:
        oa_ref, ob_ref, mg_ref, wo_ref, o_ref, h_scr, acc_scr, xm_scr = refs
    else:
        o_ref, h_scr, acc_scr = refs
    f = pl.program_id(2)
    bb, tb, d = x_ref.shape

    @pl.when(f == 0)
    def _():
        x = x_ref[...]
        if mixer_in:
            o = jnp.concatenate([oa_ref[...], ob_ref[...]], axis=-1).reshape(bb * tb, d)
            y = jnp.dot(o.astype(BF16), wo_ref[...], preferred_element_type=F32).reshape(bb, tb, d)
            x = x + (1.0 + mg_ref[...]) * y
            xm_scr[...] = x
        h = _adaln(x, gain_ref[...], sc_ref[...], sh_ref[...])
        h_scr[...] = h.reshape(bb * tb, d).astype(BF16)
        acc_scr[...] = jnp.zeros_like(acc_scr)

    h = h_scr[...]
    g = jnp.dot(h, wg_ref[...], preferred_element_type=F32)
    u = jnp.dot(h, wu_ref[...], preferred_element_type=F32)
    a = (_silu(g) * u).astype(BF16)
    acc_scr[...] += jnp.dot(a, wd_ref[...], preferred_element_type=F32)

    @pl.when(f == pl.num_programs(2) - 1)
    def _():
        y = acc_scr[...].reshape(bb, tb, d)
        base = xm_scr[...] if mixer_in else x_ref[...]
        out = base + 0.5 * (1.0 + gt_ref[...]) * y
        if final_norm:
            out = _rms(out) * fgain_ref[...]
        o_ref[...] = out


def _mod_spec(bb, j, ngrid):
    if ngrid == 3:
        return pl.BlockSpec((bb, None, 1, D_MODEL), lambda b, t, f: (b, j, 0, 0))
    return pl.BlockSpec((bb, None, 1, D_MODEL), lambda b, t: (b, j, 0, 0))


def _ffn_call(x, mod, gain, wg, wu, wd, fgain, j0, bb, tb, tf, final_norm, mixer=None):
    b, t, d = x.shape
    xspec = pl.BlockSpec((bb, tb, d), lambda i, k, f: (i, k, 0))
    rowspec = pl.BlockSpec((1, d), lambda i, k, f: (0, 0))
    in_specs = [
        xspec,
        _mod_spec(bb, j0, 3), _mod_spec(bb, j0 + 1, 3), _mod_spec(bb, j0 + 2, 3),
        rowspec,
        pl.BlockSpec((d, tf), lambda i, k, f: (0, f)),
        pl.BlockSpec((d, tf), lambda i, k, f: (0, f)),
        pl.BlockSpec((tf, d), lambda i, k, f: (f, 0)),
        rowspec,
    ]
    args = [x, mod, mod, mod, gain, wg, wu, wd, fgain]
    scratch = [pltpu.VMEM((bb * tb, d), BF16), pltpu.VMEM((bb * tb, d), F32)]
    if mixer is not None:
        oa, ob, wo = mixer
        hspec = pl.BlockSpec((bb, tb, GROUP_W), lambda i, k, f: (i, k, 0))
        in_specs += [hspec, hspec, _mod_spec(bb, 5, 3), pl.BlockSpec((d, d), lambda i, k, f: (0, 0))]
        args += [oa, ob, mod, wo]
        scratch.append(pltpu.VMEM((bb, tb, d), F32))
    return pl.pallas_call(
        functools.partial(_ffn_kernel, final_norm=final_norm, mixer_in=mixer is not None),
        grid=(b // bb, t // tb, D_FF // tf),
        in_specs=in_specs,
        out_specs=xspec,
        out_shape=jax.ShapeDtypeStruct(x.shape, x.dtype),
        scratch_shapes=scratch,
        compiler_params=_cparams("parallel", "parallel", "arbitrary"),
        name="ffn",
    )(*args)


def _adaln_proj_kernel(x_ref, sh_ref, sc_ref, gain_ref, w_ref, *refs, tail):
    p_ref = refs[-1]
    bb, tb, d = x_ref.shape
    h = _adaln(x_ref[...], gain_ref[...], sc_ref[...], sh_ref[...])
    if tail:
        refs[0][...] = h[:, tb - SUBLANES:, :]
    hb = h.reshape(bb * tb, d).astype(BF16)
    for c0 in range(0, w_ref.shape[1], PROJ_COLS):
        p = jnp.dot(hb, w_ref[:, c0:c0 + PROJ_COLS], preferred_element_type=F32)
        p_ref[:, :, c0:c0 + PROJ_COLS] = p.reshape(bb, tb, PROJ_COLS)


def _adaln_proj_call(x, mod, gain, w, bb, tb, tail):
    b, t, d = x.shape
    n = w.shape[1]
    xspec = pl.BlockSpec((bb, tb, d), lambda i, k: (i, k, 0))
    out_specs = [pl.BlockSpec((bb, tb, n), lambda i, k: (i, k, 0))]
    out_shape = [jax.ShapeDtypeStruct((b, t, n), F32)]
    if tail:
        out_specs.insert(0, pl.BlockSpec((bb, SUBLANES, d), lambda i, k: (i, 0, 0)))
        out_shape.insert(0, jax.ShapeDtypeStruct((b, SUBLANES, d), F32))
    return pl.pallas_call(
        functools.partial(_adaln_proj_kernel, tail=tail),
        grid=(b // bb, t // tb),
        in_specs=[
            xspec, _mod_spec(bb, 3, 2), _mod_spec(bb, 4, 2),
            pl.BlockSpec((1, d), lambda i, k: (0, 0)),
            pl.BlockSpec((d, n), lambda i, k: (0, 0)),
        ],
        out_specs=out_specs,
        out_shape=out_shape,
        compiler_params=_cparams("parallel", "arbitrary"),
        name="adaln_proj",
    )(x, mod, mod, gain, w)


def _rows_proj_kernel(a_ref, w_ref, o_ref):
    o_ref[...] = _mm(a_ref[...], w_ref[...])


def _rows_proj_call(a, w, tn):
    m, k = a.shape
    n = w.shape[1]
    return pl.pallas_call(
        _rows_proj_kernel,
        grid=(n // tn,),
        in_specs=[pl.BlockSpec((m, k), lambda j: (0, 0)), pl.BlockSpec((k, tn), lambda j: (0, j))],
        out_specs=pl.BlockSpec((m, tn), lambda j: (0, j)),
        out_shape=jax.ShapeDtypeStruct((m, n), F32),
        compiler_params=_cparams("parallel"),
        name="rows_proj",
    )(a, w)


def _stack_heads(x, nheads, head_w):
    lane = _iota2((1, x.shape[1]), 1)
    return jnp.concatenate(
        [jnp.where(jnp.logical_and(lane >= h * head_w, lane < (h + 1) * head_w), x, 0.0)
         for h in range(nheads)], axis=0)


def _delta_kernel(qkv_ref, z_ref, ab_ref, carry_ref, s0_ref, cw_ref, hp_ref, gain_ref,
                  o_ref, s_ref, ext_scr, s_scr, *, groups):
    t = pl.program_id(1)
    bb, n, _ = qkv_ref.shape
    rows = groups * n
    nprob = bb // groups
    hr = H_A * rows

    @pl.when(t == 0)
    def _():
        ext_scr[:, 0:SUBLANES, :] = carry_ref[...]
        s_scr[...] = s0_ref[...]

    @pl.when(t > 0)
    def _():
        ext_scr[:, 0:SUBLANES, :] = ext_scr[:, n:n + SUBLANES, :]

    ext_scr[:, SUBLANES:SUBLANES + n, :] = qkv_ref[...]
    conv = cw_ref[0:1, :] * ext_scr[:, 5:5 + n, :]
    for j in range(1, CONV_W):
        conv = conv + cw_ref[j:j + 1, :] * ext_scr[:, 5 + j:5 + j + n, :]
    x = _silu(conv).reshape(bb * n, 3 * GROUP_W)
    ab = ab_ref[...].reshape(bb * n, LANES)
    g = -jnp.exp(hp_ref[0:1, :]) * _softplus(ab + hp_ref[1:2, :])
    beta = _sigmoid(ab)
    z = z_ref[...].reshape(bb * n, GROUP_W)

    incl1, _ = _group_masks(rows, n)
    incl, strict = _group_masks(hr, n)
    lane = _iota2((1, LANES), 1)
    ones = jnp.ones((hr, LANES), F32)
    probs = range(nprob)

    def head_rows(a, h):
        return a[:, h * DK_A:(h + 1) * DK_A]

    gcol, bcol, qs, ks, vs, zs = [], [], [], [], [], []
    kst, lhs = [], []
    grow = []
    for p in probs:
        sl = slice(p * rows, (p + 1) * rows)
        gc = _mm01(incl1, g[sl], 3)
        gsel = jnp.concatenate([jnp.where(lane == h, gc, 0.0) for h in range(H_A)], axis=0)
        bsel = jnp.concatenate([jnp.where(lane == H_A + h, beta[sl], 0.0) for h in range(H_A)], axis=0)
        gcol.append(jnp.sum(gsel, axis=-1, keepdims=True))
        bcol.append(jnp.sum(bsel, axis=-1, keepdims=True))
        grow.append(_mm_nt01(ones, gsel, 3))
        xp = x[sl]
        qn = jnp.concatenate([_l2n(head_rows(xp[:, 0:GROUP_W], h)) * (DK_A ** -0.5)
                              for h in range(H_A)], axis=1)
        kn = jnp.concatenate([_l2n(head_rows(xp[:, GROUP_W:2 * GROUP_W], h)) for h in range(H_A)], axis=1)
        qs.append(jnp.concatenate([head_rows(qn, h) for h in range(H_A)], axis=0))
        ks.append(jnp.concatenate([head_rows(kn, h) for h in range(H_A)], axis=0))
        vs.append(jnp.concatenate([head_rows(xp[:, 2 * GROUP_W:], h) for h in range(H_A)], axis=0))
        zs.append(z[sl])
        k_st = _stack_heads(kn, H_A, DK_A)
        kst.append(k_st)
        lhs.append(jnp.concatenate([k_st * bcol[p], _stack_heads(qn, H_A, DK_A)], axis=0))

    kq = [_mm_nt(lhs[p], kst[p]) for p in probs]
    dec = [jnp.exp(jnp.where(incl, gcol[p] - grow[p], -jnp.inf)) for p in probs]
    a_low = [jnp.where(strict, kq[p][:hr] * dec[p], 0.0) for p in probs]
    tinv = _unit_lower_inv_many(a_low, n)
    eg = [jnp.exp(gcol[p]) for p in probs]
    kb = [ks[p] * bcol[p] for p in probs]
    sol = [_mm(tinv[p], jnp.concatenate([vs[p] * bcol[p], kb[p] * eg[p]], axis=1)) for p in probs]
    qg = [qs[p] * eg[p] for p in probs]

    us, oparts = [], []
    for p in probs:
        u_rows, o_rows = [], []
        for h in range(H_A):
            for gi in range(groups):
                r0 = h * rows + gi * n
                s = s_scr[p * groups + gi, h]
                ksq = _mm(jnp.concatenate([sol[p][r0:r0 + n, DK_A:], qg[p][r0:r0 + n]], axis=0), s)
                u_rows.append(sol[p][r0:r0 + n, :DK_A] - ksq[:n])
                o_rows.append(ksq[n:])
        us.append(jnp.concatenate(u_rows, axis=0))
        oparts.append(jnp.concatenate(o_rows, axis=0))
    outs = [oparts[p] + _mm(kq[p][hr:] * dec[p], us[p]) for p in probs]
    for p in probs:
        for h in range(H_A):
            for gi in range(groups):
                r0 = h * rows + gi * n
                glast = gcol[p][r0 + n - 1:r0 + n]
                kd = ks[p][r0:r0 + n] * jnp.exp(glast - gcol[p][r0:r0 + n])
                s = s_scr[p * groups + gi, h]
                s_scr[p * groups + gi, h] = s * jnp.exp(glast) + _mm_tn(kd, us[p][r0:r0 + n])
            o = outs[p][h * rows:(h + 1) * rows]
            val = _rms(o) * gain_ref[...] * _silu(head_rows(zs[p], h))
            o_ref[p * groups:(p + 1) * groups, :, h * DK_A:(h + 1) * DK_A] = val.reshape(groups, n, DK_A)

    @pl.when(t == pl.num_programs(1) - 1)
    def _():
        s_ref[...] = s_scr[...]


def _delta_call(proj, carry, s0, cw, hp, gain, bb, n, groups):
    b, t, _ = proj.shape
    return pl.pallas_call(
        functools.partial(_delta_kernel, groups=groups),
        grid=(b // bb, t // n),
        in_specs=[
            pl.BlockSpec((bb, n, 3 * GROUP_W), lambda i, k: (i, k, EV_QKV // (3 * GROUP_W))),
            pl.BlockSpec((bb, n, GROUP_W), lambda i, k: (i, k, EV_Z // GROUP_W)),
            pl.BlockSpec((bb, n, LANES), lambda i, k: (i, k, EV_AB // LANES)),
            pl.BlockSpec((bb, SUBLANES, 3 * GROUP_W), lambda i, k: (i, 0, EV_QKV // (3 * GROUP_W))),
            pl.BlockSpec((bb, H_A, DK_A, DK_A), lambda i, k: (i, 0, 0, 0)),
            pl.BlockSpec((SUBLANES, 3 * GROUP_W), lambda i, k: (0, 0)),
            pl.BlockSpec((SUBLANES, LANES), lambda i, k: (0, 0)),
            pl.BlockSpec((1, DK_A), lambda i, k: (0, 0)),
        ],
        out_specs=[
            pl.BlockSpec((bb, n, GROUP_W), lambda i, k: (i, k, 0)),
            pl.BlockSpec((bb, H_A, DK_A, DK_A), lambda i, k: (i, 0, 0, 0)),
        ],
        out_shape=[jax.ShapeDtypeStruct((b, t, GROUP_W), F32),
                   jax.ShapeDtypeStruct((b, H_A, DK_A, DK_A), F32)],
        scratch_shapes=[pltpu.VMEM((bb, n + SUBLANES, 3 * GROUP_W), F32),
                        pltpu.VMEM((bb, H_A, DK_A, DK_A), F32)],
        compiler_params=_cparams("parallel", "arbitrary"),
        name="delta",
    )(proj, proj, proj, carry, s0, cw, hp, gain)


def _rwkv_kernel(rkv_ref, wag_ref, crkv_ref, cwag_ref, s0_ref, mur_ref, muw_ref, wlr_ref, prm_ref,
                 y_ref, s_ref, ext1_scr, ext2_scr, sp_scr, *, groups):
    t = pl.program_id(1)
    bb, n, _ = rkv_ref.shape
    rows = groups * n
    nprob = bb // groups
    npair = H_B // 2
    pw = 2 * HD_B
    r2 = 2 * rows

    @pl.when(t == 0)
    def _():
        ext1_scr[:, 0:SUBLANES, :] = crkv_ref[...]
        ext2_scr[:, 0:SUBLANES, :] = cwag_ref[...]
        sp_scr[...] = s0_ref[...]

    @pl.when(t > 0)
    def _():
        ext1_scr[:, 0:SUBLANES, :] = ext1_scr[:, n:n + SUBLANES, :]
        ext2_scr[:, 0:SUBLANES, :] = ext2_scr[:, n:n + SUBLANES, :]

    cur1 = rkv_ref[...]
    cur2 = wag_ref[...]
    ext1_scr[:, SUBLANES:SUBLANES + n, :] = cur1
    ext2_scr[:, SUBLANES:SUBLANES + n, :] = cur2
    xm1 = cur1 + (ext1_scr[:, SUBLANES - 1:SUBLANES - 1 + n, :] - cur1) * mur_ref[...]
    xm2 = cur2 + (ext2_scr[:, SUBLANES - 1:SUBLANES - 1 + n, :] - cur2) * muw_ref[...]
    xm1 = xm1.reshape(bb * n, 3 * GROUP_W)
    xm2 = xm2.reshape(bb * n, 2 * LANES)
    r = xm1[:, 0:GROUP_W]
    kb = xm1[:, GROUP_W:2 * GROUP_W]
    vb = xm1[:, 2 * GROUP_W:3 * GROUP_W]
    lane2 = _iota2(xm2.shape, 1)
    feat = jnp.where(lane2 < 64, jnp.tanh(xm2), jnp.where(lane2 < 128, xm2, _sigmoid(xm2)))
    lr = _mm(feat, wlr_ref[...])
    w_raw = prm_ref[0:1, :] + lr[:, 0:GROUP_W]
    logw = -jnp.exp(-_softplus(-w_raw) - 0.5)
    a = _sigmoid(prm_ref[1:2, :] + lr[:, GROUP_W:2 * GROUP_W])
    gb = lr[:, 2 * GROUP_W:3 * GROUP_W]
    kkraw = kb * prm_ref[2:3, :]
    k = kb * (1.0 + (a - 1.0) * prm_ref[3:4, :])

    incl1, _ = _group_masks(rows, n)
    incl, strict = _group_masks(r2, n)
    blk = (_iota2((pw, pw), 0) & -HD_B) == (_iota2((pw, pw), 1) & -HD_B)
    blkf = blk.astype(F32)
    cums = [_mm01(incl1, logw[p * rows:(p + 1) * rows], 3) for p in range(nprob)]

    units = [(p, q) for p in range(nprob) for q in range(npair)]

    def st(x):
        return _stack_heads(x, 2, HD_B)

    pre = []
    for p, q in units:
        rs = slice(p * rows, (p + 1) * rows)
        sl = slice(q * pw, (q + 1) * pw)
        kkr = kkraw[rs, sl]
        rp, kp, vp, ap = r[rs, sl], k[rs, sl], vb[rs, sl], a[rs, sl]
        sums = _mm_x01(jnp.concatenate([kkr * kkr, rp * kp * prm_ref[4:5, sl]], axis=0), blkf, 1)
        kk = kkr * lax.rsqrt(sums[:rows] + 1e-6)
        cump = cums[p][:, sl]
        ginv = jnp.exp(-cump)
        rt = rp * jnp.exp(cump)
        at = -kk * jnp.exp(cump - logw[rs, sl])
        bt = kk * ap * ginv
        kt = kp * ginv
        pre.append(dict(rt=rt, at=at, bt=bt, kt=kt, vp=vp, cump=cump, bonus=sums[rows:] * vp,
                        vst=st(vp), gb=gb[rs, sl]))
    mats = [_mm_nt(jnp.concatenate([st(u["at"]), st(u["rt"])], axis=0),
                   jnp.concatenate([st(u["bt"]), st(u["kt"])], axis=0)) for u in pre]
    tinv = _unit_lower_inv_many([-jnp.where(strict, m[:r2, :r2], 0.0) for m in mats], n)

    u0s, y0s = [], []
    for (p, q), u in zip(units, pre):
        u0_rows, y0_rows = [], []
        for gi in range(groups):
            gs = slice(gi * n, (gi + 1) * n)
            uy = _mm_nt(jnp.concatenate([u["at"][gs], u["rt"][gs]], axis=0), sp_scr[p * groups + gi, q])
            u0_rows.append(uy[:n])
            y0_rows.append(uy[n:])
        u0s.append(jnp.concatenate(u0_rows, axis=0))
        y0s.append(jnp.concatenate(y0_rows, axis=0))
    x1 = [_mm(jnp.where(strict, m[:r2, r2:], 0.0), u["vst"]) for m, u in zip(mats, pre)]
    ust = [_mm(ti, st(u0) + x) for ti, u0, x in zip(tinv, u0s, x1)]
    yst = [_mm(jnp.concatenate([jnp.where(incl, m[r2:, :r2], 0.0), jnp.where(incl, m[r2:, r2:], 0.0)], axis=1),
               jnp.concatenate([us_, u["vst"]], axis=0)) for m, us_, u in zip(mats, ust, pre)]
    ys = [y0 + ys_[:rows] + ys_[rows:] for y0, ys_ in zip(y0s, yst)]
    means = [_mm_x01(y, blkf, 1) * (1.0 / HD_B) for y in ys]
    ycs = [y - m for y, m in zip(ys, means)]
    variances = [_mm_x01(yc * yc, blkf, 1) * (1.0 / HD_B) for yc in ycs]

    for (p, q), u, us_, yc, var in zip(units, pre, ust, ycs, variances):
        sl = slice(q * pw, (q + 1) * pw)
        uu = us_[:rows] + us_[rows:]
        for gi in range(groups):
            gs = slice(gi * n, (gi + 1) * n)
            upd = _mm_tn(jnp.concatenate([uu[gs], u["vp"][gs]], axis=0),
                         jnp.concatenate([u["bt"][gs], u["kt"][gs]], axis=0))
            glast = jnp.exp(u["cump"][gi * n + n - 1:gi * n + n, :])
            sp = sp_scr[p * groups + gi, q]
            sp_scr[p * groups + gi, q] = jnp.where(blk, sp + upd, 0.0) * glast
        yn = yc * lax.rsqrt(var + GN_EPS) * prm_ref[5:6, sl] + prm_ref[6:7, sl]
        val = (yn + u["bonus"]) * u["gb"]
        y_ref[p * groups:(p + 1) * groups, :, sl] = val.reshape(groups, n, pw)

    @pl.when(t == pl.num_programs(1) - 1)
    def _():
        s_ref[...] = sp_scr[...]


def _rwkv_call(proj, carry, s0, mur, muw, wlr, prm, bb, n, groups):
    b, t, _ = proj.shape
    wag_w = 2 * LANES
    npair, pw = H_B // 2, 2 * HD_B
    return pl.pallas_call(
        functools.partial(_rwkv_kernel, groups=groups),
        grid=(b // bb, t // n),
        in_specs=[
            pl.BlockSpec((bb, n, 3 * GROUP_W), lambda i, k: (i, k, EV_RKV // (3 * GROUP_W))),
            pl.BlockSpec((bb, n, wag_w), lambda i, k: (i, k, EV_WAG // wag_w)),
            pl.BlockSpec((bb, SUBLANES, 3 * GROUP_W), lambda i, k: (i, 0, EV_RKV // (3 * GROUP_W))),
            pl.BlockSpec((bb, SUBLANES, wag_w), lambda i, k: (i, 0, EV_WAG // wag_w)),
            pl.BlockSpec((bb, npair, pw, pw), lambda i, k: (i, 0, 0, 0)),
            pl.BlockSpec((1, 3 * GROUP_W), lambda i, k: (0, 0)),
            pl.BlockSpec((1, wag_w), lambda i, k: (0, 0)),
            pl.BlockSpec((wag_w, 3 * GROUP_W), lambda i, k: (0, 0)),
            pl.BlockSpec((SUBLANES, GROUP_W), lambda i, k: (0, 0)),
        ],
        out_specs=[
            pl.BlockSpec((bb, n, GROUP_W), lambda i, k: (i, k, 0)),
            pl.BlockSpec((bb, npair, pw, pw), lambda i, k: (i, 0, 0, 0)),
        ],
        out_shape=[jax.ShapeDtypeStruct((b, t, GROUP_W), F32),
                   jax.ShapeDtypeStruct((b, npair, pw, pw), F32)],
        scratch_shapes=[pltpu.VMEM((bb, n + SUBLANES, 3 * GROUP_W), F32),
                        pltpu.VMEM((bb, n + SUBLANES, wag_w), F32),
                        pltpu.VMEM((bb, npair, pw, pw), F32)],
        compiler_params=_cparams("parallel", "arbitrary"),
        name="rwkv7",
    )(proj, proj, carry, carry, s0, mur, muw, wlr, prm)


def _rwkv_state_to_pairs(s):
    b = s.shape[0]
    s = s.reshape(b, H_B // 2, 2, HD_B, HD_B)
    zero = jnp.zeros_like(s[:, :, 0])
    top = jnp.concatenate([s[:, :, 0], zero], axis=-1)
    bot = jnp.concatenate([zero, s[:, :, 1]], axis=-1)
    return jnp.concatenate([top, bot], axis=-2)


def _rwkv_state_from_pairs(sp):
    b = sp.shape[0]
    heads = jnp.stack([sp[:, :, :HD_B, :HD_B], sp[:, :, HD_B:, HD_B:]], axis=2)
    return heads.reshape(b, H_B, HD_B, HD_B)


def _stack_pair(x):
    first = (_iota2((1, x.shape[1]), 1) & 64) == 0
    return jnp.concatenate([jnp.where(first, x, 0.0), jnp.where(first, 0.0, x)], axis=0)


def _group_rows(a, rows, n, gi):
    return jnp.concatenate([a[gi * n:(gi + 1) * n], a[rows + gi * n:rows + (gi + 1) * n]], axis=0)


def _ungroup_rows(pieces, n):
    return jnp.concatenate([p[:n] for p in pieces] + [p[n:] for p in pieces], axis=0)


def _gla_kernel(qk_ref, v_ref, z_ref, gkin_ref, s0_ref, wgk_ref, bgk_ref, gain_ref,
                o_ref, s_ref, s_scr, *, groups):
    t = pl.program_id(1)
    bb, n, _ = qk_ref.shape
    rows = groups * n
    nprob = bb // groups
    npair = H_C // 2
    pw = 2 * DK_C
    qkw = H_C * DK_C
    r2 = 2 * rows
    sub = min(GLA_SUB, n)
    nslab = n // sub
    assert groups == 1 or nslab == 1

    @pl.when(t == 0)
    def _():
        s_scr[...] = s0_ref[...]

    qk = qk_ref[...].reshape(bb * n, 2 * qkw)
    q = qk[:, :qkw] * (DK_C ** -0.5)
    k = qk[:, qkw:]
    v = v_ref[...].reshape(bb * n, GROUP_W)
    z = z_ref[...].reshape(bb * n, GROUP_W)
    gk = _log_sigmoid(_mm(gkin_ref[...].reshape(bb * n, LANES), wgk_ref[...]) + bgk_ref[...]) * (1.0 / GLA_NORM)
    incl1, _ = _group_masks(rows, n)
    incl2, _ = _group_masks(r2, n)
    row_t = _iota2((rows, 1), 0) & (n - 1)
    eye_p = _eye(pw)
    cums = [_mm01(incl1, gk[p * rows:(p + 1) * rows], 3) for p in range(nprob)]
    units = [(p, u) for p in range(nprob) for u in range(npair)]

    pre = []
    for p, u in units:
        rs = slice(p * rows, (p + 1) * rows)
        sl = slice(u * pw, (u + 1) * pw)
        bcum, qp, kp = cums[p][:, sl], q[rs, sl], k[rs, sl]
        qparts, kparts = [], []
        for s in range(nslab):
            rho = bcum[s * sub - 1:s * sub, :] if s > 0 else jnp.zeros((1, pw), F32)
            in_slab = jnp.logical_and(row_t >= s * sub, row_t < (s + 1) * sub)
            qparts.append(qp * jnp.exp(jnp.where(in_slab, bcum - rho, -jnp.inf)))
            kparts.append(kp * jnp.exp(jnp.where(row_t < (s + 1) * sub, rho - bcum, -jnp.inf)))
        blast = jnp.concatenate(
            [jnp.broadcast_to(bcum[gi * n + n - 1:gi * n + n, :], (n, pw)) for gi in range(groups)], axis=0)
        vst = jnp.concatenate([v[rs, (2 * u + j) * DV_C:(2 * u + j + 1) * DV_C] for j in range(2)], axis=0)
        pre.append(dict(qcat=_stack_pair(jnp.concatenate(qparts, axis=1)),
                        kcat=_stack_pair(jnp.concatenate(kparts, axis=1)),
                        qdb=_stack_pair(qp * jnp.exp(bcum)), kd=_stack_pair(kp * jnp.exp(blast - bcum)),
                        vst=vst, bcum=bcum))
    att = [jnp.where(incl2, _mm_nt(u["qcat"], u["kcat"]), 0.0) for u in pre]
    intra = [_mm(a, u["vst"]) for a, u in zip(att, pre)]
    inter = []
    for (p, uidx), u in zip(units, pre):
        pieces = [_mm(_group_rows(u["qdb"], rows, n, gi), s_scr[p * groups + gi, uidx])
                  for gi in range(groups)]
        inter.append(_ungroup_rows(pieces, n))
    pick_last = (_iota2((BF16_ROWS, rows), 1)
                 == (_iota2((BF16_ROWS, rows), 0) & (groups - 1)) * n + n - 1).astype(F32)
    glast = [jnp.exp(_mm01(pick_last, u["bcum"], 3)) for u in pre]
    glcols = [_mm_nt01(eye_p, g, 3) for g in glast]
    for (p, uidx), u, o_in, o_x, glc in zip(units, pre, intra, inter, glcols):
        for gi in range(groups):
            upd = _mm_tn(_group_rows(u["kd"], rows, n, gi), _group_rows(u["vst"], rows, n, gi))
            s_scr[p * groups + gi, uidx] = s_scr[p * groups + gi, uidx] * glc[:, gi:gi + 1] + upd
        o = o_in + o_x
        for j in range(2):
            h = 2 * uidx + j
            zh = z[p * rows:(p + 1) * rows, h * DV_C:(h + 1) * DV_C]
            val = _rms(o[j * rows:(j + 1) * rows]) * gain_ref[...] * _silu(zh)
            o_ref[p * groups:(p + 1) * groups, :, h * DV_C:(h + 1) * DV_C] = val.reshape(groups, n, DV_C)

    @pl.when(t == pl.num_programs(1) - 1)
    def _():
        s_ref[...] = s_scr[...]


def _gla_call(proj, s0, wgk, bgk, gain, bb, n, groups):
    b, t, _ = proj.shape
    npair, pw = H_C // 2, 2 * DK_C
    return pl.pallas_call(
        functools.partial(_gla_kernel, groups=groups),
        grid=(b // bb, t // n),
        in_specs=[
            pl.BlockSpec((bb, n, GROUP_W), lambda i, k: (i, k, OD_CQK // GROUP_W)),
            pl.BlockSpec((bb, n, GROUP_W), lambda i, k: (i, k, OD_CV // GROUP_W)),
            pl.BlockSpec((bb, n, GROUP_W), lambda i, k: (i, k, OD_CZ // GROUP_W)),
            pl.BlockSpec((bb, n, LANES), lambda i, k: (i, k, OD_CGK // LANES)),
            pl.BlockSpec((bb, npair, DV_C, pw), lambda i, k: (i, 0, 0, 0)),
            pl.BlockSpec((LANES, H_C * DK_C), lambda i, k: (0, 0)),
            pl.BlockSpec((1, H_C * DK_C), lambda i, k: (0, 0)),
            pl.BlockSpec((1, DV_C), lambda i, k: (0, 0)),
        ],
        out_specs=[
            pl.BlockSpec((bb, n, GROUP_W), lambda i, k: (i, k, 0)),
            pl.BlockSpec((bb, npair, DV_C, pw), lambda i, k: (i, 0, 0, 0)),
        ],
        out_shape=[jax.ShapeDtypeStruct((b, t, GROUP_W), F32),
                   jax.ShapeDtypeStruct((b, npair, DV_C, pw), F32)],
        scratch_shapes=[pltpu.VMEM((bb, npair, DV_C, pw), F32)],
        compiler_params=_cparams("parallel", "arbitrary"),
        name="gla",
    )(proj, proj, proj, proj, s0, wgk, bgk, gain)


def _mlstm_kernel(qk_ref, v_ref, og_ref, if_ref, c0_ref, n0_ref, m0_ref, bif_ref, gain_ref,
                  h_ref, c_ref, nn_ref, m_ref, c_scr, n_scr, m_scr, *, groups):
    t = pl.program_id(1)
    bb, n, _ = qk_ref.shape
    rows = groups * n
    nprob = bb // groups
    npair = H_D // 2
    pw = 2 * DK_D
    qkw = H_D * DK_D
    r2 = 2 * rows

    @pl.when(t == 0)
    def _():
        c_scr[...] = c0_ref[...]
        n_scr[...] = n0_ref[...]
        m_scr[...] = m0_ref[...]

    x = if_ref[...].reshape(bb * n, LANES) + bif_ref[...]
    lf = _log_sigmoid(x)
    qk = qk_ref[...].reshape(bb * n, 2 * qkw)
    q = qk[:, :qkw]
    k = qk[:, qkw:] * (DK_D ** -0.5)
    v = v_ref[...].reshape(bb * n, GROUP_W)
    og = og_ref[...].reshape(bb * n, GROUP_W)
    incl1, _ = _group_masks(rows, n)
    incl2, _ = _group_masks(r2, n)
    r_i, c_i = _iota2((r2, r2), 0), _iota2((r2, r2), 1)
    same2 = (r_i & -n) == (c_i & -n)
    last2 = jnp.logical_and(same2, (c_i & (n - 1)) == n - 1)
    lane = _iota2((1, LANES), 1)
    ones = jnp.ones((r2, LANES), F32)
    ones_sq = jnp.ones((LANES, LANES), F32)
    fcums = [_mm01(incl1, lf[p * rows:(p + 1) * rows], 3) for p in range(nprob)]
    m_rows = [m_scr[b_] for b_ in range(bb)]
    m_old = list(m_rows)
    units = [(p, u) for p in range(nprob) for u in range(npair)]

    def per_block(fn):
        return jnp.concatenate([jnp.broadcast_to(fn(j, gi), (n, fn(j, gi).shape[1]))
                                for j in range(2) for gi in range(groups)], axis=0)

    pre = []
    for p, u in units:
        rs = slice(p * rows, (p + 1) * rows)
        sl = slice(u * pw, (u + 1) * pw)
        fsel = jnp.concatenate([jnp.where(lane == H_D + 2 * u + j, fcums[p], 0.0) for j in range(2)], axis=0)
        isel = jnp.concatenate([jnp.where(lane == 2 * u + j, x[rs], 0.0) for j in range(2)], axis=0)
        cols = _mm_x01(jnp.concatenate([fsel, isel], axis=0), ones_sq, 3)
        fcol, icol = cols[:r2], cols[r2:]
        drow = _mm_nt01(ones, isel - fsel, 3)
        flast = _mm01(last2, fcol, 3)
        mprev = per_block(lambda j, gi: m_old[p * groups + gi][:, 2 * u + j:2 * u + j + 1])
        nmat = per_block(lambda j, gi: n_scr[p * groups + gi][:, sl])
        log_d = jnp.where(incl2, fcol + drow, -jnp.inf)
        m_in = jnp.max(log_d, axis=-1, keepdims=True)
        m_e = jnp.max(jnp.where(same2, flast + drow, -jnp.inf), axis=-1, keepdims=True)
        m_t = jnp.maximum(fcol + mprev, m_in)
        w_in = jnp.exp(fcol + mprev - m_t)
        m_new = jnp.maximum(flast + mprev, m_e)
        qst, kst = _stack_pair(q[rs, sl]), _stack_pair(k[rs, sl])
        vst = jnp.concatenate([v[rs, (2 * u + j) * DV_D:(2 * u + j + 1) * DV_D] for j in range(2)], axis=0)
        pre.append(dict(qst=qst, kst=kst, vst=vst, log_d=log_d, m_t=m_t, w_in=w_in, m_new=m_new,
                        cd=jnp.exp(flast + mprev - m_new), nmat=nmat,
                        ke=kst * jnp.exp(flast - fcol + icol - m_new)))
    dms = [jnp.exp(u["log_d"] - u["m_t"]) * _mm_nt(u["qst"], u["kst"]) for u in pre]
    intra = [_mm(dm, u["vst"]) for dm, u in zip(dms, pre)]
    inter = []
    for (p, uidx), u in zip(units, pre):
        qw = u["w_in"] * u["qst"]
        pieces = [_mm(_group_rows(qw, rows, n, gi), c_scr[p * groups + gi, uidx]) for gi in range(groups)]
        inter.append(_ungroup_rows(pieces, n))
    for (p, uidx), u, dm, o_in, o_x in zip(units, pre, dms, intra, inter):
        sl = slice(uidx * pw, (uidx + 1) * pw)
        den = (u["w_in"] * _mm_x01(u["qst"] * u["nmat"], ones_sq, 1)
               + _mm_x01(dm, ones_sq, 2))
        hout = (o_in + o_x) / jnp.maximum(jnp.abs(den), jnp.exp(-u["m_t"]))
        for gi in range(groups):
            b_ = p * groups + gi
            r0, r1 = gi * n, rows + gi * n
            cdrow = jnp.where(lane < DK_D, u["cd"][r0:r0 + 1], u["cd"][r1:r1 + 1])
            ke_g = _group_rows(u["ke"], rows, n, gi)
            cdcol = jnp.where(_iota2((pw, 1), 0) < DK_D, u["cd"][r0:r0 + 1], u["cd"][r1:r1 + 1])
            c_scr[b_, uidx] = c_scr[b_, uidx] * cdcol + _mm_tn(ke_g, _group_rows(u["vst"], rows, n, gi))
            n_scr[b_, :, sl] = n_scr[b_][:, sl] * cdrow + jnp.sum(ke_g, axis=0, keepdims=True)
            for j, r in ((0, r0), (1, r1)):
                m_rows[b_] = jnp.where(lane == 2 * uidx + j, u["m_new"][r:r + 1], m_rows[b_])
        for j in range(2):
            h = 2 * uidx + j
            ogh = og[p * rows:(p + 1) * rows, h * DV_D:(h + 1) * DV_D]
            val = _sigmoid(ogh) * (_rms(hout[j * rows:(j + 1) * rows]) * gain_ref[...])
            h_ref[p * groups:(p + 1) * groups, :, h * DV_D:(h + 1) * DV_D] = val.reshape(groups, n, DV_D)
    for b_ in range(bb):
        m_scr[b_] = m_rows[b_]

    @pl.when(t == pl.num_programs(1) - 1)
    def _():
        c_ref[...] = c_scr[...]
        nn_ref[...] = n_scr[...]
        m_ref[...] = m_scr[...]


def _mlstm_call(proj, c0, n0, m0, bif, gain, bb, n, groups):
    b, t, _ = proj.shape
    qkw = H_D * DK_D
    npair, pw = H_D // 2, 2 * DK_D
    return pl.pallas_call(
        functools.partial(_mlstm_kernel, groups=groups),
        grid=(b // bb, t // n),
        in_specs=[
            pl.BlockSpec((bb, n, GROUP_W), lambda i, k: (i, k, OD_DQK // GROUP_W)),
            pl.BlockSpec((bb, n, GROUP_W), lambda i, k: (i, k, OD_DV // GROUP_W)),
            pl.BlockSpec((bb, n, GROUP_W), lambda i, k: (i, k, OD_DO // GROUP_W)),
            pl.BlockSpec((bb, n, LANES), lambda i, k: (i, k, OD_DIF // LANES)),
            pl.BlockSpec((bb, npair, DV_D, pw), lambda i, k: (i, 0, 0, 0)),
            pl.BlockSpec((bb, 1, qkw), lambda i, k: (i, 0, 0)),
            pl.BlockSpec((bb, 1, LANES), lambda i, k: (i, 0, 0)),
            pl.BlockSpec((1, LANES), lambda i, k: (0, 0)),
            pl.BlockSpec((1, DV_D), lambda i, k: (0, 0)),
        ],
        out_specs=[
            pl.BlockSpec((bb, n, GROUP_W), lambda i, k: (i, k, 0)),
            pl.BlockSpec((bb, npair, DV_D, pw), lambda i, k: (i, 0, 0, 0)),
            pl.BlockSpec((bb, 1, qkw), lambda i, k: (i, 0, 0)),
            pl.BlockSpec((bb, 1, LANES), lambda i, k: (i, 0, 0)),
        ],
        out_shape=[jax.ShapeDtypeStruct((b, t, GROUP_W), F32),
                   jax.ShapeDtypeStruct((b, npair, DV_D, pw), F32),
                   jax.ShapeDtypeStruct((b, 1, qkw), F32),
                   jax.ShapeDtypeStruct((b, 1, LANES), F32)],
        scratch_shapes=[pltpu.VMEM((bb, npair, DV_D, pw), F32),
                        pltpu.VMEM((bb, 1, qkw), F32),
                        pltpu.VMEM((bb, 1, LANES), F32)],
        compiler_params=_cparams("parallel", "arbitrary"),
        name="mlstm",
    )(proj, proj, proj, proj, c0, n0, m0, bif, gain)


def _pad_cols(w, width):
    return jnp.pad(w, ((0, 0), (0, width - w.shape[1])))


def _pad_rows(w, rows, at=0):
    return jnp.pad(w, ((at, rows - at - w.shape[0]), (0, 0)))


def _even_in_weight(w):
    pa, pb = w[:, :2056], w[:, 2056:]
    cols = [
        pb[:, 0:1536],
        pa[:, 0:1536],
        pa[:, 1544:2056],
        pb[:, 1536:1792],
        _pad_cols(pa[:, 1536:1544], LANES),
    ]
    return _pad_cols(jnp.concatenate(cols, axis=1), EV_COLS).astype(BF16)


def _odd_in_weight(w):
    pc, pd = w[:, :1552], w[:, 1552:]
    cols = [
        pc[:, 0:512],
        pd[:, 0:512],
        pc[:, 512:1024],
        pc[:, 1040:1552],
        pd[:, 512:1024],
        pd[:, 1032:1544],
        _pad_cols(pc[:, 1024:1040], LANES),
        _pad_cols(pd[:, 1024:1032], LANES),
    ]
    return jnp.concatenate(cols, axis=1).astype(BF16)


def _row(v):
    return v.reshape(1, -1).astype(F32)


def _tiles(x, rows=1024):
    b, t, _ = x.shape
    tb = min(t, rows)
    bb = min(b, rows // tb)
    return bb, tb


def _trunk(x, mod, states, wts):
    xbuf, s_delta, s_rwkv, s_gla, s_mc, s_mn, s_mm = states
    b, t, _ = x.shape
    bb, tb = _tiles(x)
    fbb, ftb = _tiles(x, FFN_ROWS)
    pbb, ptb = _tiles(x, PROJ_ROWS)
    n = min(CHUNK, t)
    groups = CHUNK // n
    rbb = (PROBLEMS_LONG if t > CHUNK else PROBLEMS) * groups
    new_even = ([], [], [])
    new_odd = ([], [], [], [])
    for l in range(DEPTH):
        lw = wts["layers"][l]
        m_l = mod[l]
        x = _ffn_call(x, m_l, lw["gain0"], lw["wg0"], lw["wu0"], lw["wd0"], wts["final_gain"], 0, fbb, ftb,
                      FFN_COLS, False)
        i = l // 2
        if l % 2 == 0:
            h_tail, proj = _adaln_proj_call(x, m_l, lw["gain1"], lw["w_in"], pbb, ptb, True)
            if xbuf is None:
                carry = jnp.zeros((b, SUBLANES, EV_COLS), F32)
            else:
                rows = _rows_proj_call(xbuf[i].reshape(b * (CONV_W - 1), D_MODEL), lw["w_in"], 1024)
                carry = jnp.pad(rows.reshape(b, CONV_W - 1, EV_COLS),
                                ((0, 0), (SUBLANES - CONV_W + 1, 0), (0, 0)))
            oa, sd = _delta_call(proj, carry, s_delta[i], lw["conv_w"], lw["delta_hp"], lw["gain_a"],
                                 rbb, n, groups)
            ob, sr = _rwkv_call(proj, carry, _rwkv_state_to_pairs(s_rwkv[i]), lw["mu_rkv"], lw["mu_wag"],
                                lw["w_lora"], lw["rwkv_prm"], rbb, n, groups)
            for lst, val in zip(new_even, (h_tail[:, SUBLANES - (CONV_W - 1):], sd, _rwkv_state_from_pairs(sr))):
                lst.append(val)
        else:
            (proj,) = _adaln_proj_call(x, m_l, lw["gain1"], lw["w_in"], pbb, ptb, False)
            pair_shape = (b, H_C // 2, 2 * DK_C, DV_C)
            oa, sg = _gla_call(proj, s_gla[i].reshape(pair_shape), lw["w_gk2"], lw["b_gk"], lw["gain_c"],
                               rbb, n, groups)
            ob, sc, sn, sm = _mlstm_call(proj, s_mc[i].reshape(pair_shape), s_mn[i].reshape(b, 1, H_D * DK_D),
                                         _pad_cols(s_mm[i], LANES).reshape(b, 1, LANES),
                                         lw["b_if"], lw["gain_d"], rbb, n, groups)
            for lst, val in zip(new_odd, (sg.reshape(b, H_C, DK_C, DV_C), sc.reshape(b, H_D, DK_D, DV_D),
                                          sn.reshape(b, H_D, DK_D), sm[:, 0, :H_D])):
                lst.append(val)
        x = _ffn_call(x, m_l, lw["gain2"], lw["wg1"], lw["wu1"], lw["wd1"], wts["final_gain"], 6, fbb, ftb,
                      FFN_COLS, l == DEPTH - 1, mixer=(oa, ob, lw["w_out"]))
    stacked = [jnp.stack(lst) for lst in new_even + new_odd]
    return (x, *stacked)


def _prepare_weights(norm_gain, final_gain, w_ffn_gate, w_ffn_up, w_ffn_down, w_in_even, w_out_even,
                     conv_w, a_log, dt_bias, gain_a, mu_b, w0_b, w_w2, a0_b, w_a2, w_g2, k_k, k_a, r_k,
                     lnx_gain, lnx_bias, w_in_odd, w_out_odd, w_gk2, b_gk, gain_c, b_i, b_f, gain_d):
    layers = []
    for l in range(DEPTH):
        i = l // 2
        lw = {
            "gain0": _row(norm_gain[l, 0]), "gain1": _row(norm_gain[l, 1]), "gain2": _row(norm_gain[l, 2]),
            "wg0": w_ffn_gate[l, 0].astype(BF16), "wu0": w_ffn_up[l, 0].astype(BF16),
            "wd0": w_ffn_down[l, 0].astype(BF16),
            "wg1": w_ffn_gate[l, 1].astype(BF16), "wu1": w_ffn_up[l, 1].astype(BF16),
            "wd1": w_ffn_down[l, 1].astype(BF16),
        }
        if l % 2 == 0:
            lw["w_in"] = _even_in_weight(w_in_even[i])
            lw["w_out"] = w_out_even[i].astype(BF16)
            lw["conv_w"] = _pad_rows(conv_w[i].astype(F32), SUBLANES)
            lw["delta_hp"] = _pad_rows(jnp.stack([_pad_cols(_row(a_log[i]), LANES)[0],
                                                  _pad_cols(_row(dt_bias[i]), LANES)[0]]), SUBLANES)
            lw["gain_a"] = _row(gain_a[i])
            lw["mu_rkv"] = _row(mu_b[i, :1536])
            lw["mu_wag"] = _row(mu_b[i, 1536:])
            lora = jnp.zeros((2 * LANES, 3 * GROUP_W), F32)
            lora = lora.at[0:64, 0:GROUP_W].set(w_w2[i])
            lora = lora.at[64:128, GROUP_W:2 * GROUP_W].set(w_a2[i])
            lora = lora.at[128:256, 2 * GROUP_W:].set(w_g2[i])
            lw["w_lora"] = lora.astype(BF16)
            lw["rwkv_prm"] = jnp.stack([w0_b[i], a0_b[i], k_k[i], k_a[i], r_k[i].reshape(-1),
                                        lnx_gain[i], lnx_bias[i], jnp.zeros_like(w0_b[i])]).astype(F32)
        else:
            lw["w_in"] = _odd_in_weight(w_in_odd[i])
            lw["w_out"] = w_out_odd[i].astype(BF16)
            lw["w_gk2"] = _pad_rows(w_gk2[i], LANES).astype(BF16)
            lw["b_gk"] = _row(b_gk[i])
            lw["gain_c"] = _row(gain_c[i])
            lw["b_if"] = _pad_cols(_row(jnp.concatenate([b_i[i], b_f[i]])), LANES)
            lw["gain_d"] = _row(gain_d[i])
        layers.append(lw)
    return {"layers": layers, "final_gain": _row(final_gain)}


def kernel(x_prompt, x_sample, c_prompt, c_sample, state_xbuf_even, state_delta, state_rwkv, state_gla,
           state_mlstm_c, state_mlstm_n, state_mlstm_m, w_mod, b_mod, norm_gain, final_gain, w_ffn_gate,
           w_ffn_up, w_ffn_down, w_in_even, w_out_even, conv_w, a_log, dt_bias, gain_a, mu_b, w0_b, w_w2,
           a0_b, w_a2, w_g2, k_k, k_a, r_k, lnx_gain, lnx_bias, w_in_odd, w_out_odd, w_gk2, b_gk, gain_c,
           b_i, b_f, gain_d):
    wts = _prepare_weights(norm_gain, final_gain, w_ffn_gate, w_ffn_up, w_ffn_down, w_in_even, w_out_even,
                           conv_w, a_log, dt_bias, gain_a, mu_b, w0_b, w_w2, a0_b, w_a2, w_g2, k_k, k_a,
                           r_k, lnx_gain, lnx_bias, w_in_odd, w_out_odd, w_gk2, b_gk, gain_c, b_i, b_f,
                           gain_d)
    bp, bs = x_prompt.shape[0], x_sample.shape[0]
    c_all = jnp.concatenate([c_prompt, c_sample], axis=0).astype(F32)
    mod = _mod_call(c_all, w_mod, b_mod).reshape(DEPTH, bp + bs, N_MOD, 1, D_MODEL)
    mod_p, mod_s = mod[:, :bp], mod[:, bp:]

    def zeros(shape):
        return jnp.zeros(shape, F32)

    zero_states = (
        None,
        zeros((N_EVEN, bp, H_A, DK_A, DK_A)),
        zeros((N_EVEN, bp, H_B, HD_B, HD_B)),
        zeros((N_ODD, bp, H_C, DK_C, DV_C)),
        zeros((N_ODD, bp, H_D, DK_D, DV_D)),
        zeros((N_ODD, bp, H_D, DK_D)),
        zeros((N_ODD, bp, H_D)),
    )
    y_p, xb_p, dl_p, rw_p, gl_p, mc_p, mn_p, mm_p = _trunk(x_prompt, mod_p, zero_states, wts)
    sample_states = (state_xbuf_even, state_delta, state_rwkv, state_gla,
                     state_mlstm_c, state_mlstm_n, state_mlstm_m)
    y_s, xb_s, dl_s, rw_s, gl_s, mc_s, mn_s, mm_s = _trunk(x_sample, mod_s, sample_states, wts)
    return (y_p, y_s, xb_p, xb_s, dl_p, dl_s, rw_p, rw_s, gl_p, gl_s, mc_p, mc_s, mn_p, mn_s, mm_p, mm_s)
```

```python
import functools

import jax
import jax.numpy as jnp
from jax import lax
from jax.experimental import pallas as pl
from jax.experimental.pallas import tpu as pltpu

F32 = jnp.float32
BF16 = jnp.bfloat16
HIGHEST = lax.Precision.HIGHEST

D_MODEL = 1024
DEPTH = 4
N_EVEN = 2
N_ODD = 2
D_FF = 2816
N_MOD = 9
EPS = 1e-6
GN_EPS = 64e-5
CONV_W = 4
H_A, DK_A = 4, 128
H_B, HD_B = 8, 64
H_C, DK_C, DV_C = 4, 64, 128
H_D, DK_D, DV_D = 4, 64, 128
GLA_NORM = 16.0
GROUP_W = 512

LANES = 128
SUBLANES = 8
BF16_ROWS = 16
VMEM_LIMIT_BYTES = 48 * 1024 * 1024

EV_RKV, EV_QKV, EV_Z, EV_WAG, EV_AB, EV_COLS = 0, 1536, 3072, 3584, 3840, 4096
OD_CQK, OD_DQK, OD_CV, OD_CZ, OD_DV, OD_DO, OD_CGK, OD_DIF, OD_COLS = (
    0, 512, 1024, 1536, 2048, 2560, 3072, 3200, 3328)

CHUNK = 64
GLA_SUB = 16
PROJ_ROWS = 512
PROJ_COLS = 256
FFN_ROWS = 1024
FFN_COLS = 256
PROBLEMS = 2
PROBLEMS_LONG = 8


def _bf(x):
    if x.dtype == BF16:
        return x
    if x.shape[-2] % BF16_ROWS == 0 and x.shape[-1] % BF16_ROWS == 0:
        return x.astype(BF16)
    return x


def _pair(a, b):
    a, b = _bf(a), _bf(b)
    if a.dtype != b.dtype:
        a, b = a.astype(F32), b.astype(F32)
    return a, b


def _mm(a, b):
    a, b = _pair(a, b)
    return jnp.dot(a, b, preferred_element_type=F32)


def _mm_nt(a, b):
    a, b = _pair(a, b)
    return lax.dot_general(a, b, (((1,), (1,)), ((), ())), preferred_element_type=F32)


def _mm_tn(a, b):
    a, b = _pair(a, b)
    return lax.dot_general(a, b, (((0,), (0,)), ((), ())), preferred_element_type=F32)


def _mm_f32(a, b):
    return jnp.dot(a, b, precision=HIGHEST, preferred_element_type=F32)


def _mm_nt_f32(a, b):
    return lax.dot_general(a, b, (((1,), (1,)), ((), ())), precision=HIGHEST,
                           preferred_element_type=F32)


def _split_bf16(x, parts):
    out, r = [], x
    for i in range(parts):
        p = r.astype(BF16)
        out.append(p)
        if i + 1 < parts:
            r = r - p.astype(F32)
    return out


def _mm01(a01, x, parts):
    a = a01.astype(BF16)
    acc = None
    for p in _split_bf16(x, parts):
        d = jnp.dot(a, p, preferred_element_type=F32)
        acc = d if acc is None else acc + d
    return acc


def _mm_x01(x, b01, parts):
    b = b01.astype(BF16)
    acc = None
    for p in _split_bf16(x, parts):
        d = jnp.dot(p, b, preferred_element_type=F32)
        acc = d if acc is None else acc + d
    return acc


def _mm_nt01(a01, x, parts):
    a = a01.astype(BF16)
    acc = None
    for p in _split_bf16(x, parts):
        d = lax.dot_general(a, p, (((1,), (1,)), ((), ())), preferred_element_type=F32)
        acc = d if acc is None else acc + d
    return acc


def _sigmoid(x):
    return jax.nn.sigmoid(x)


def _silu(x):
    return x * jax.nn.sigmoid(x)


def _softplus(x):
    return jnp.maximum(x, 0.0) + jnp.log1p(jnp.exp(-jnp.abs(x)))


def _log_sigmoid(x):
    return -_softplus(-x)


def _rms(x, eps=EPS):
    return x * lax.rsqrt(jnp.mean(x * x, axis=-1, keepdims=True) + eps)


def _l2n(x):
    return x * lax.rsqrt(jnp.sum(x * x, axis=-1, keepdims=True) + 1e-6)


def _iota2(shape, dim):
    return lax.broadcasted_iota(jnp.int32, shape, dim)


def _tril_masks(n):
    r, c = _iota2((n, n), 0), _iota2((n, n), 1)
    return r >= c, r > c


def _group_masks(size, n):
    r, c = _iota2((size, size), 0), _iota2((size, size), 1)
    same = (r & -n) == (c & -n)
    return jnp.logical_and(same, r >= c), jnp.logical_and(same, r > c)


def _eye(n):
    return (_iota2((n, n), 0) == _iota2((n, n), 1)).astype(F32)


def _transpose_rows(x):
    return _mm_nt_f32(_eye(x.shape[1]), x)


def _unit_lower_inv_many(mats, n):
    eye = _eye(mats[0].shape[0])
    ms = [-a for a in mats]
    ps = [eye + m for m in ms]
    covered = 2
    while covered < n:
        ms = [_mm(m, m) for m in ms]
        ps = [p + _mm(p, m) for p, m in zip(ps, ms)]
        covered *= 2
    return ps


def _cparams(*sem):
    return pltpu.CompilerParams(dimension_semantics=sem, vmem_limit_bytes=VMEM_LIMIT_BYTES)


def _mod_kernel(c_ref, w_ref, b_ref, op_ref, os_ref):
    cs = _silu(c_ref[...])
    m = _mm(cs, w_ref[...]) + b_ref[...]
    nprompt = op_ref.shape[0]
    op_ref[...] = m[:nprompt]
    os_ref[...] = m[nprompt:]


def _mod_call(c_all, nprompt, w_mod, b_mod):
    rows = c_all.shape[0]
    tn = 1024
    width = N_MOD * D_MODEL
    return pl.pallas_call(
        _mod_kernel,
        grid=(DEPTH, width // tn),
        in_specs=[
            pl.BlockSpec((rows, D_MODEL), lambda l, j: (0, 0)),
            pl.BlockSpec((None, D_MODEL, tn), lambda l, j: (l, 0, j)),
            pl.BlockSpec((None, 1, tn), lambda l, j: (l, 0, j)),
        ],
        out_specs=[pl.BlockSpec((None, nprompt, tn), lambda l, j: (l, 0, j)),
                   pl.BlockSpec((None, rows - nprompt, tn), lambda l, j: (l, 0, j))],
        out_shape=[jax.ShapeDtypeStruct((DEPTH, nprompt, width), F32),
                   jax.ShapeDtypeStruct((DEPTH, rows - nprompt, width), F32)],
        compiler_params=_cparams("parallel", "parallel"),
        name="mod",
    )(c_all, w_mod, b_mod.reshape(DEPTH, 1, width))


def _mod_rows(mod_ref, j, batch_block, bb, tb):
    cols = slice(j * D_MODEL, (j + 1) * D_MODEL)
    if bb == 1:
        return mod_ref[pl.ds(batch_block, 1), cols]
    rep = (_iota2((bb * tb, bb), 0) & -tb) == _iota2((bb * tb, bb), 1) * tb
    return _mm01(rep.astype(F32), mod_ref[:, cols], 3).reshape(bb, tb, D_MODEL)


def _mod_table_spec(mod, bb, ngrid):
    nb, width = mod.shape
    if bb == 1:
        return pl.BlockSpec((nb, width), (lambda i, k, f: (0, 0)) if ngrid == 3 else (lambda i, k: (0, 0)))
    return pl.BlockSpec((bb, width), (lambda i, k, f: (i, 0)) if ngrid == 3 else (lambda i, k: (i, 0)))


def _adaln(x, gain, scale, shift):
    return _rms(x) * gain * (1.0 + scale) + shift


def _ffn_kernel(x_ref, mod_ref, gain_ref, wg_ref, wu_ref, wd_ref, fgain_ref, *refs,
                j0, final_norm, mixer_in):
    if mixer_in:
        oa_ref, ob_ref, wo_ref, o_ref, h_scr, acc_scr, xm_scr = refs
    else:
        o_ref, h_scr, acc_scr = refs
    ib = pl.program_id(0)
    f = pl.program_id(2)
    bb, tb, d = x_ref.shape

    @pl.when(f == 0)
    def _():
        x = x_ref[...]
        if mixer_in:
            o = jnp.concatenate([oa_ref[...], ob_ref[...]], axis=-1).reshape(bb * tb, d)
            y = jnp.dot(o.astype(BF16), wo_ref[...], preferred_element_type=F32).reshape(bb, tb, d)
            x = x + (1.0 + _mod_rows(mod_ref, 5, ib, bb, tb)) * y
            xm_scr[...] = x
        h = _adaln(x, gain_ref[...], _mod_rows(mod_ref, j0 + 1, ib, bb, tb), _mod_rows(mod_ref, j0, ib, bb, tb))
        h_scr[...] = h.reshape(bb * tb, d).astype(BF16)
        acc_scr[...] = jnp.zeros_like(acc_scr)

    h = h_scr[...]
    g = jnp.dot(h, wg_ref[...], preferred_element_type=F32)
    u = jnp.dot(h, wu_ref[...], preferred_element_type=F32)
    a = (_silu(g) * u).astype(BF16)
    acc_scr[...] += jnp.dot(a, wd_ref[...], preferred_element_type=F32)

    @pl.when(f == pl.num_programs(2) - 1)
    def _():
        y = acc_scr[...].reshape(bb, tb, d)
        base = xm_scr[...] if mixer_in else x_ref[...]
        out = base + 0.5 * (1.0 + _mod_rows(mod_ref, j0 + 2, ib, bb, tb)) * y
        if final_norm:
            out = _rms(out) * fgain_ref[...]
        o_ref[...] = out


def _ffn_call(x, mod, gain, wg, wu, wd, fgain, j0, bb, tb, tf, final_norm, mixer=None):
    b, t, d = x.shape
    xspec = pl.BlockSpec((bb, tb, d), lambda i, k, f: (i, k, 0))
    rowspec = pl.BlockSpec((1, d), lambda i, k, f: (0, 0))
    in_specs = [
        xspec,
        _mod_table_spec(mod, bb, 3),
        rowspec,
        pl.BlockSpec((d, tf), lambda i, k, f: (0, f)),
        pl.BlockSpec((d, tf), lambda i, k, f: (0, f)),
        pl.BlockSpec((tf, d), lambda i, k, f: (f, 0)),
        rowspec,
    ]
    args = [x, mod, gain, wg, wu, wd, fgain]
    scratch = [pltpu.VMEM((bb * tb, d), BF16), pltpu.VMEM((bb * tb, d), F32)]
    if mixer is not None:
        oa, ob, wo = mixer
        hspec = pl.BlockSpec((bb, tb, GROUP_W), lambda i, k, f: (i, k, 0))
        in_specs += [hspec, hspec, pl.BlockSpec((d, d), lambda i, k, f: (0, 0))]
        args += [oa, ob, wo]
        scratch.append(pltpu.VMEM((bb, tb, d), F32))
    return pl.pallas_call(
        functools.partial(_ffn_kernel, j0=j0, final_norm=final_norm, mixer_in=mixer is not None),
        grid=(b // bb, t // tb, D_FF // tf),
        in_specs=in_specs,
        out_specs=xspec,
        out_shape=jax.ShapeDtypeStruct(x.shape, x.dtype),
        scratch_shapes=scratch,
        compiler_params=_cparams("parallel", "parallel", "arbitrary"),
        name="ffn",
    )(*args)


def _adaln_proj_kernel(x_ref, mod_ref, gain_ref, w_ref, *refs, tail):
    p_ref = refs[-1]
    ib = pl.program_id(0)
    bb, tb, d = x_ref.shape
    h = _adaln(x_ref[...], gain_ref[...], _mod_rows(mod_ref, 4, ib, bb, tb), _mod_rows(mod_ref, 3, ib, bb, tb))
    if tail:
        refs[0][...] = h[:, tb - SUBLANES:, :]
    hb = h.reshape(bb * tb, d).astype(BF16)
    for c0 in range(0, w_ref.shape[1], PROJ_COLS):
        p = jnp.dot(hb, w_ref[:, c0:c0 + PROJ_COLS], preferred_element_type=F32)
        p_ref[:, :, c0:c0 + PROJ_COLS] = p.reshape(bb, tb, PROJ_COLS)


def _adaln_proj_call(x, mod, gain, w, bb, tb, tail):
    b, t, d = x.shape
    n = w.shape[1]
    xspec = pl.BlockSpec((bb, tb, d), lambda i, k: (i, k, 0))
    out_specs = [pl.BlockSpec((bb, tb, n), lambda i, k: (i, k, 0))]
    out_shape = [jax.ShapeDtypeStruct((b, t, n), F32)]
    if tail:
        out_specs.insert(0, pl.BlockSpec((bb, SUBLANES, d), lambda i, k: (i, 0, 0)))
        out_shape.insert(0, jax.ShapeDtypeStruct((b, SUBLANES, d), F32))
    return pl.pallas_call(
        functools.partial(_adaln_proj_kernel, tail=tail),
        grid=(b // bb, t // tb),
        in_specs=[
            xspec, _mod_table_spec(mod, bb, 2),
            pl.BlockSpec((1, d), lambda i, k: (0, 0)),
            pl.BlockSpec((d, n), lambda i, k: (0, 0)),
        ],
        out_specs=out_specs,
        out_shape=out_shape,
        compiler_params=_cparams("parallel", "arbitrary"),
        name="adaln_proj",
    )(x, mod, gain, w)


def _rows_proj_kernel(a_ref, w_ref, o_ref):
    o_ref[...] = _mm(a_ref[...], w_ref[...])


def _rows_proj_call(a, w, tn):
    m, k = a.shape
    n = w.shape[1]
    return pl.pallas_call(
        _rows_proj_kernel,
        grid=(n // tn,),
        in_specs=[pl.BlockSpec((m, k), lambda j: (0, 0)), pl.BlockSpec((k, tn), lambda j: (0, j))],
        out_specs=pl.BlockSpec((m, tn), lambda j: (0, j)),
        out_shape=jax.ShapeDtypeStruct((m, n), F32),
        compiler_params=_cparams("parallel"),
        name="rows_proj",
    )(a, w)


def _stack_heads(x, nheads, head_w):
    lane = _iota2((1, x.shape[1]), 1)
    return jnp.concatenate(
        [jnp.where(jnp.logical_and(lane >= h * head_w, lane < (h + 1) * head_w), x, 0.0)
         for h in range(nheads)], axis=0)


def _delta_kernel(qkv_ref, z_ref, ab_ref, carry_ref, s0_ref, cw_ref, hp_ref, gain_ref,
                  o_ref, s_ref, ext_scr, s_scr, *, groups):
    t = pl.program_id(1)
    bb, n, _ = qkv_ref.shape
    rows = groups * n
    nprob = bb // groups
    hr = H_A * rows

    @pl.when(t == 0)
    def _():
        ext_scr[:, 0:SUBLANES, :] = carry_ref[...]
        s_scr[...] = s0_ref[...]

    @pl.when(t > 0)
    def _():
        ext_scr[:, 0:SUBLANES, :] = ext_scr[:, n:n + SUBLANES, :]

    ext_scr[:, SUBLANES:SUBLANES + n, :] = qkv_ref[...]
    conv = cw_ref[0:1, :] * ext_scr[:, 5:5 + n, :]
    for j in range(1, CONV_W):
        conv = conv + cw_ref[j:j + 1, :] * ext_scr[:, 5 + j:5 + j + n, :]
    x = _silu(conv).reshape(bb * n, 3 * GROUP_W)
    ab = ab_ref[...].reshape(bb * n, LANES)
    g = -jnp.exp(hp_ref[0:1, :]) * _softplus(ab + hp_ref[1:2, :])
    beta = _sigmoid(ab)
    z = z_ref[...].reshape(bb * n, GROUP_W)

    incl1, _ = _group_masks(rows, n)
    incl, strict = _group_masks(hr, n)
    lane = _iota2((1, LANES), 1)
    ones = jnp.ones((hr, LANES), F32)
    probs = range(nprob)

    def head_rows(a, h):
        return a[:, h * DK_A:(h + 1) * DK_A]

    gcol, bcol, qs, ks, vs, zs = [], [], [], [], [], []
    kst, lhs = [], []
    grow = []
    for p in probs:
        sl = slice(p * rows, (p + 1) * rows)
        gc = _mm01(incl1, g[sl], 3)
        gsel = jnp.concatenate([jnp.where(lane == h, gc, 0.0) for h in range(H_A)], axis=0)
        bsel = jnp.concatenate([jnp.where(lane == H_A + h, beta[sl], 0.0) for h in range(H_A)], axis=0)
        gcol.append(jnp.sum(gsel, axis=-1, keepdims=True))
        bcol.append(jnp.sum(bsel, axis=-1, keepdims=True))
        grow.append(_mm_nt01(ones, gsel, 3))
        xp = x[sl]
        qn = jnp.concatenate([_l2n(head_rows(xp[:, 0:GROUP_W], h)) * (DK_A ** -0.5)
                              for h in range(H_A)], axis=1)
        kn = jnp.concatenate([_l2n(head_rows(xp[:, GROUP_W:2 * GROUP_W], h)) for h in range(H_A)], axis=1)
        qs.append(jnp.concatenate([head_rows(qn, h) for h in range(H_A)], axis=0))
        ks.append(jnp.concatenate([head_rows(kn, h) for h in range(H_A)], axis=0))
        vs.append(jnp.concatenate([head_rows(xp[:, 2 * GROUP_W:], h) for h in range(H_A)], axis=0))
        zs.append(z[sl])
        k_st = _stack_heads(kn, H_A, DK_A)
        kst.append(k_st)
        lhs.append(jnp.concatenate([k_st * bcol[p], _stack_heads(qn, H_A, DK_A)], axis=0))

    kq = [_mm_nt(lhs[p], kst[p]) for p in probs]
    dec = [jnp.exp(jnp.where(incl, gcol[p] - grow[p], -jnp.inf)) for p in probs]
    a_low = [jnp.where(strict, kq[p][:hr] * dec[p], 0.0) for p in probs]
    tinv = _unit_lower_inv_many(a_low, n)
    eg = [jnp.exp(gcol[p]) for p in probs]
    kb = [ks[p] * bcol[p] for p in probs]
    sol = [_mm(tinv[p], jnp.concatenate([vs[p] * bcol[p], kb[p] * eg[p]], axis=1)) for p in probs]
    qg = [qs[p] * eg[p] for p in probs]

    us, oparts = [], []
    for p in probs:
        u_rows, o_rows = [], []
        for h in range(H_A):
            for gi in range(groups):
                r0 = h * rows + gi * n
                s = s_scr[p * groups + gi, h]
                ksq = _mm(jnp.concatenate([sol[p][r0:r0 + n, DK_A:], qg[p][r0:r0 + n]], axis=0), s)
                u_rows.append(sol[p][r0:r0 + n, :DK_A] - ksq[:n])
                o_rows.append(ksq[n:])
        us.append(jnp.concatenate(u_rows, axis=0))
        oparts.append(jnp.concatenate(o_rows, axis=0))
    outs = [oparts[p] + _mm(kq[p][hr:] * dec[p], us[p]) for p in probs]
    for p in probs:
        for h in range(H_A):
            for gi in range(groups):
                r0 = h * rows + gi * n
                glast = gcol[p][r0 + n - 1:r0 + n]
                kd = ks[p][r0:r0 + n] * jnp.exp(glast - gcol[p][r0:r0 + n])
                s = s_scr[p * groups + gi, h]
                s_scr[p * groups + gi, h] = s * jnp.exp(glast) + _mm_tn(kd, us[p][r0:r0 + n])
            o = outs[p][h * rows:(h + 1) * rows]
            val = _rms(o) * gain_ref[...] * _silu(head_rows(zs[p], h))
            o_ref[p * groups:(p + 1) * groups, :, h * DK_A:(h + 1) * DK_A] = val.reshape(groups, n, DK_A)

    @pl.when(t == pl.num_programs(1) - 1)
    def _():
        s_ref[...] = s_scr[...]


def _delta_call(proj, carry, s0, cw, hp, gain, bb, n, groups):
    b, t, _ = proj.shape
    return pl.pallas_call(
        functools.partial(_delta_kernel, groups=groups),
        grid=(b // bb, t // n),
        in_specs=[
            pl.BlockSpec((bb, n, 3 * GROUP_W), lambda i, k: (i, k, EV_QKV // (3 * GROUP_W))),
            pl.BlockSpec((bb, n, GROUP_W), lambda i, k: (i, k, EV_Z // GROUP_W)),
            pl.BlockSpec((bb, n, LANES), lambda i, k: (i, k, EV_AB // LANES)),
            pl.BlockSpec((bb, SUBLANES, 3 * GROUP_W), lambda i, k: (i, 0, EV_QKV // (3 * GROUP_W))),
            pl.BlockSpec((bb, H_A, DK_A, DK_A), lambda i, k: (i, 0, 0, 0)),
            pl.BlockSpec((SUBLANES, 3 * GROUP_W), lambda i, k: (0, 0)),
            pl.BlockSpec((SUBLANES, LANES), lambda i, k: (0, 0)),
            pl.BlockSpec((1, DK_A), lambda i, k: (0, 0)),
        ],
        out_specs=[
            pl.BlockSpec((bb, n, GROUP_W), lambda i, k: (i, k, 0)),
            pl.BlockSpec((bb, H_A, DK_A, DK_A), lambda i, k: (i, 0, 0, 0)),
        ],
        out_shape=[jax.ShapeDtypeStruct((b, t, GROUP_W), F32),
                   jax.ShapeDtypeStruct((b, H_A, DK_A, DK_A), F32)],
        scratch_shapes=[pltpu.VMEM((bb, n + SUBLANES, 3 * GROUP_W), F32),
                        pltpu.VMEM((bb, H_A, DK_A, DK_A), F32)],
        compiler_params=_cparams("parallel", "arbitrary"),
        name="delta",
    )(proj, proj, proj, carry, s0, cw, hp, gain)


def _rwkv_kernel(rkv_ref, wag_ref, crkv_ref, cwag_ref, s0_ref, mur_ref, muw_ref, wlr_ref, prm_ref,
                 y_ref, s_ref, ext1_scr, ext2_scr, sp_scr, *, groups):
    t = pl.program_id(1)
    bb, n, _ = rkv_ref.shape
    rows = groups * n
    nprob = bb // groups
    npair = H_B // 2
    pw = 2 * HD_B
    r2 = 2 * rows

    @pl.when(t == 0)
    def _():
        ext1_scr[:, 0:SUBLANES, :] = crkv_ref[...]
        ext2_scr[:, 0:SUBLANES, :] = cwag_ref[...]
        sp_scr[...] = s0_ref[...]

    @pl.when(t > 0)
    def _():
        ext1_scr[:, 0:SUBLANES, :] = ext1_scr[:, n:n + SUBLANES, :]
        ext2_scr[:, 0:SUBLANES, :] = ext2_scr[:, n:n + SUBLANES, :]

    cur1 = rkv_ref[...]
    cur2 = wag_ref[...]
    ext1_scr[:, SUBLANES:SUBLANES + n, :] = cur1
    ext2_scr[:, SUBLANES:SUBLANES + n, :] = cur2
    xm1 = cur1 + (ext1_scr[:, SUBLANES - 1:SUBLANES - 1 + n, :] - cur1) * mur_ref[...]
    xm2 = cur2 + (ext2_scr[:, SUBLANES - 1:SUBLANES - 1 + n, :] - cur2) * muw_ref[...]
    xm1 = xm1.reshape(bb * n, 3 * GROUP_W)
    xm2 = xm2.reshape(bb * n, 2 * LANES)
    r = xm1[:, 0:GROUP_W]
    kb = xm1[:, GROUP_W:2 * GROUP_W]
    vb = xm1[:, 2 * GROUP_W:3 * GROUP_W]
    lane2 = _iota2(xm2.shape, 1)
    feat = jnp.where(lane2 < 64, jnp.tanh(xm2), jnp.where(lane2 < 128, xm2, _sigmoid(xm2)))
    lr = _mm(feat, wlr_ref[...])
    w_raw = prm_ref[0:1, :] + lr[:, 0:GROUP_W]
    logw = -jnp.exp(-_softplus(-w_raw) - 0.5)
    a = _sigmoid(prm_ref[1:2, :] + lr[:, GROUP_W:2 * GROUP_W])
    gb = lr[:, 2 * GROUP_W:3 * GROUP_W]
    kkraw = kb * prm_ref[2:3, :]
    k = kb * (1.0 + (a - 1.0) * prm_ref[3:4, :])

    incl1, _ = _group_masks(rows, n)
    incl, strict = _group_masks(r2, n)
    blk = (_iota2((pw, pw), 0) & -HD_B) == (_iota2((pw, pw), 1) & -HD_B)
    blkf = blk.astype(F32)
    cums = [_mm01(incl1, logw[p * rows:(p + 1) * rows], 3) for p in range(nprob)]

    units = [(p, q) for p in range(nprob) for q in range(npair)]

    def st(x):
        return _stack_heads(x, 2, HD_B)

    pre = []
    for p, q in units:
        rs = slice(p * rows, (p + 1) * rows)
        sl = slice(q * pw, (q + 1) * pw)
        kkr = kkraw[rs, sl]
        rp, kp, vp, ap = r[rs, sl], k[rs, sl], vb[rs, sl], a[rs, sl]
        sums = _mm_x01(jnp.concatenate([kkr * kkr, rp * kp * prm_ref[4:5, sl]], axis=0), blkf, 1)
        kk = kkr * lax.rsqrt(sums[:rows] + 1e-6)
        cump = cums[p][:, sl]
        ginv = jnp.exp(-cump)
        rt = rp * jnp.exp(cump)
        at = -kk * jnp.exp(cump - logw[rs, sl])
        bt = kk * ap * ginv
        kt = kp * ginv
        pre.append(dict(rt=rt, at=at, bt=bt, kt=kt, vp=vp, cump=cump, bonus=sums[rows:] * vp,
                        vst=st(vp), gb=gb[rs, sl]))
    mats = [_mm_nt(jnp.concatenate([st(u["at"]), st(u["rt"])], axis=0),
                   jnp.concatenate([st(u["bt"]), st(u["kt"])], axis=0)) for u in pre]
    tinv = _unit_lower_inv_many([-jnp.where(strict, m[:r2, :r2], 0.0) for m in mats], n)

    u0s, y0s = [], []
    for (p, q), u in zip(units, pre):
        u0_rows, y0_rows = [], []
        for gi in range(groups):
            gs = slice(gi * n, (gi + 1) * n)
            uy = _mm_nt(jnp.concatenate([u["at"][gs], u["rt"][gs]], axis=0), sp_scr[p * groups + gi, q])
            u0_rows.append(uy[:n])
            y0_rows.append(uy[n:])
        u0s.append(jnp.concatenate(u0_rows, axis=0))
        y0s.append(jnp.concatenate(y0_rows, axis=0))
    x1 = [_mm(jnp.where(strict, m[:r2, r2:], 0.0), u["vst"]) for m, u in zip(mats, pre)]
    ust = [_mm(ti, st(u0) + x) for ti, u0, x in zip(tinv, u0s, x1)]
    yst = [_mm(jnp.concatenate([jnp.where(incl, m[r2:, :r2], 0.0), jnp.where(incl, m[r2:, r2:], 0.0)], axis=1),
               jnp.concatenate([us_, u["vst"]], axis=0)) for m, us_, u in zip(mats, ust, pre)]
    ys = [y0 + ys_[:rows] + ys_[rows:] for y0, ys_ in zip(y0s, yst)]
    means = [_mm_x01(y, blkf, 1) * (1.0 / HD_B) for y in ys]
    ycs = [y - m for y, m in zip(ys, means)]
    variances = [_mm_x01(yc * yc, blkf, 1) * (1.0 / HD_B) for yc in ycs]

    for (p, q), u, us_, yc, var in zip(units, pre, ust, ycs, variances):
        sl = slice(q * pw, (q + 1) * pw)
        uu = us_[:rows] + us_[rows:]
        for gi in range(groups):
            gs = slice(gi * n, (gi + 1) * n)
            upd = _mm_tn(jnp.concatenate([uu[gs], u["vp"][gs]], axis=0),
                         jnp.concatenate([u["bt"][gs], u["kt"][gs]], axis=0))
            glast = jnp.exp(u["cump"][gi * n + n - 1:gi * n + n, :])
            sp = sp_scr[p * groups + gi, q]
            sp_scr[p * groups + gi, q] = jnp.where(blk, sp + upd, 0.0) * glast
        yn = yc * lax.rsqrt(var + GN_EPS) * prm_ref[5:6, sl] + prm_ref[6:7, sl]
        val = (yn + u["bonus"]) * u["gb"]
        y_ref[p * groups:(p + 1) * groups, :, sl] = val.reshape(groups, n, pw)

    @pl.when(t == pl.num_programs(1) - 1)
    def _():
        s_ref[...] = sp_scr[...]


def _rwkv_call(proj, carry, s0, mur, muw, wlr, prm, bb, n, groups):
    b, t, _ = proj.shape
    wag_w = 2 * LANES
    npair, pw = H_B // 2, 2 * HD_B
    return pl.pallas_call(
        functools.partial(_rwkv_kernel, groups=groups),
        grid=(b // bb, t // n),
        in_specs=[
            pl.BlockSpec((bb, n, 3 * GROUP_W), lambda i, k: (i, k, EV_RKV // (3 * GROUP_W))),
            pl.BlockSpec((bb, n, wag_w), lambda i, k: (i, k, EV_WAG // wag_w)),
            pl.BlockSpec((bb, SUBLANES, 3 * GROUP_W), lambda i, k: (i, 0, EV_RKV // (3 * GROUP_W))),
            pl.BlockSpec((bb, SUBLANES, wag_w), lambda i, k: (i, 0, EV_WAG // wag_w)),
            pl.BlockSpec((bb, npair, pw, pw), lambda i, k: (i, 0, 0, 0)),
            pl.BlockSpec((1, 3 * GROUP_W), lambda i, k: (0, 0)),
            pl.BlockSpec((1, wag_w), lambda i, k: (0, 0)),
            pl.BlockSpec((wag_w, 3 * GROUP_W), lambda i, k: (0, 0)),
            pl.BlockSpec((SUBLANES, GROUP_W), lambda i, k: (0, 0)),
        ],
        out_specs=[
            pl.BlockSpec((bb, n, GROUP_W), lambda i, k: (i, k, 0)),
            pl.BlockSpec((bb, npair, pw, pw), lambda i, k: (i, 0, 0, 0)),
        ],
        out_shape=[jax.ShapeDtypeStruct((b, t, GROUP_W), F32),
                   jax.ShapeDtypeStruct((b, npair, pw, pw), F32)],
        scratch_shapes=[pltpu.VMEM((bb, n + SUBLANES, 3 * GROUP_W), F32),
                        pltpu.VMEM((bb, n + SUBLANES, wag_w), F32),
                        pltpu.VMEM((bb, npair, pw, pw), F32)],
        compiler_params=_cparams("parallel", "arbitrary"),
        name="rwkv7",
    )(proj, proj, carry, carry, s0, mur, muw, wlr, prm)


def _rwkv_state_to_pairs(s):
    b = s.shape[0]
    s = s.reshape(b, H_B // 2, 2, HD_B, HD_B)
    zero = jnp.zeros_like(s[:, :, 0])
    top = jnp.concatenate([s[:, :, 0], zero], axis=-1)
    bot = jnp.concatenate([zero, s[:, :, 1]], axis=-1)
    return jnp.concatenate([top, bot], axis=-2)


def _rwkv_state_from_pairs(sp):
    b = sp.shape[0]
    heads = jnp.stack([sp[:, :, :HD_B, :HD_B], sp[:, :, HD_B:, HD_B:]], axis=2)
    return heads.reshape(b, H_B, HD_B, HD_B)


def _stack_pair(x):
    first = (_iota2((1, x.shape[1]), 1) & 64) == 0
    return jnp.concatenate([jnp.where(first, x, 0.0), jnp.where(first, 0.0, x)], axis=0)


def _group_rows(a, rows, n, gi):
    return jnp.concatenate([a[gi * n:(gi + 1) * n], a[rows + gi * n:rows + (gi + 1) * n]], axis=0)


def _ungroup_rows(pieces, n):
    return jnp.concatenate([p[:n] for p in pieces] + [p[n:] for p in pieces], axis=0)


def _gla_kernel(qk_ref, v_ref, z_ref, gkin_ref, s0_ref, wgk_ref, bgk_ref, gain_ref,
                o_ref, s_ref, s_scr, *, groups):
    t = pl.program_id(1)
    bb, n, _ = qk_ref.shape
    rows = groups * n
    nprob = bb // groups
    npair = H_C // 2
    pw = 2 * DK_C
    qkw = H_C * DK_C
    r2 = 2 * rows
    sub = min(GLA_SUB, n)
    nslab = n // sub
    assert groups == 1 or nslab == 1

    @pl.when(t == 0)
    def _():
        s_scr[...] = s0_ref[...]

    qk = qk_ref[...].reshape(bb * n, 2 * qkw)
    q = qk[:, :qkw] * (DK_C ** -0.5)
    k = qk[:, qkw:]
    v = v_ref[...].reshape(bb * n, GROUP_W)
    z = z_ref[...].reshape(bb * n, GROUP_W)
    gk = _log_sigmoid(_mm(gkin_ref[...].reshape(bb * n, LANES), wgk_ref[...]) + bgk_ref[...]) * (1.0 / GLA_NORM)
    incl1, _ = _group_masks(rows, n)
    incl2, _ = _group_masks(r2, n)
    row_t = _iota2((rows, 1), 0) & (n - 1)
    eye_p = _eye(pw)
    cums = [_mm01(incl1, gk[p * rows:(p + 1) * rows], 3) for p in range(nprob)]
    units = [(p, u) for p in range(nprob) for u in range(npair)]

    pre = []
    for p, u in units:
        rs = slice(p * rows, (p + 1) * rows)
        sl = slice(u * pw, (u + 1) * pw)
        bcum, qp, kp = cums[p][:, sl], q[rs, sl], k[rs, sl]
        qparts, kparts = [], []
        for s in range(nslab):
            rho = bcum[s * sub - 1:s * sub, :] if s > 0 else jnp.zeros((1, pw), F32)
            in_slab = jnp.logical_and(row_t >= s * sub, row_t < (s + 1) * sub)
            qparts.append(qp * jnp.exp(jnp.where(in_slab, bcum - rho, -jnp.inf)))
            kparts.append(kp * jnp.exp(jnp.where(row_t < (s + 1) * sub, rho - bcum, -jnp.inf)))
        blast = jnp.concatenate(
            [jnp.broadcast_to(bcum[gi * n + n - 1:gi * n + n, :], (n, pw)) for gi in range(groups)], axis=0)
        vst = jnp.concatenate([v[rs, (2 * u + j) * DV_C:(2 * u + j + 1) * DV_C] for j in range(2)], axis=0)
        pre.append(dict(qcat=_stack_pair(jnp.concatenate(qparts, axis=1)),
                        kcat=_stack_pair(jnp.concatenate(kparts, axis=1)),
                        qdb=_stack_pair(qp * jnp.exp(bcum)), kd=_stack_pair(kp * jnp.exp(blast - bcum)),
                        vst=vst, bcum=bcum))
    att = [jnp.where(incl2, _mm_nt(u["qcat"], u["kcat"]), 0.0) for u in pre]
    intra = [_mm(a, u["vst"]) for a, u in zip(att, pre)]
    inter = []
    for (p, uidx), u in zip(units, pre):
        pieces = [_mm(_group_rows(u["qdb"], rows, n, gi), s_scr[p * groups + gi, uidx])
                  for gi in range(groups)]
        inter.append(_ungroup_rows(pieces, n))
    pick_last = (_iota2((BF16_ROWS, rows), 1)
                 == (_iota2((BF16_ROWS, rows), 0) & (groups - 1)) * n + n - 1).astype(F32)
    glast = [jnp.exp(_mm01(pick_last, u["bcum"], 3)) for u in pre]
    glcols = [_mm_nt01(eye_p, g, 3) for g in glast]
    for (p, uidx), u, o_in, o_x, glc in zip(units, pre, intra, inter, glcols):
        for gi in range(groups):
            upd = _mm_tn(_group_rows(u["kd"], rows, n, gi), _group_rows(u["vst"], rows, n, gi))
            s_scr[p * groups + gi, uidx] = s_scr[p * groups + gi, uidx] * glc[:, gi:gi + 1] + upd
        o = o_in + o_x
        for j in range(2):
            h = 2 * uidx + j
            zh = z[p * rows:(p + 1) * rows, h * DV_C:(h + 1) * DV_C]
            val = _rms(o[j * rows:(j + 1) * rows]) * gain_ref[...] * _silu(zh)
            o_ref[p * groups:(p + 1) * groups, :, h * DV_C:(h + 1) * DV_C] = val.reshape(groups, n, DV_C)

    @pl.when(t == pl.num_programs(1) - 1)
    def _():
        s_ref[...] = s_scr[...]


def _gla_call(proj, s0, wgk, bgk, gain, bb, n, groups):
    b, t, _ = proj.shape
    npair, pw = H_C // 2, 2 * DK_C
    return pl.pallas_call(
        functools.partial(_gla_kernel, groups=groups),
        grid=(b // bb, t // n),
        in_specs=[
            pl.BlockSpec((bb, n, GROUP_W), lambda i, k: (i, k, OD_CQK // GROUP_W)),
            pl.BlockSpec((bb, n, GROUP_W), lambda i, k: (i, k, OD_CV // GROUP_W)),
            pl.BlockSpec((bb, n, GROUP_W), lambda i, k: (i, k, OD_CZ // GROUP_W)),
            pl.BlockSpec((bb, n, LANES), lambda i, k: (i, k, OD_CGK // LANES)),
            pl.BlockSpec((bb, npair, DV_C, pw), lambda i, k: (i, 0, 0, 0)),
            pl.BlockSpec((LANES, H_C * DK_C), lambda i, k: (0, 0)),
            pl.BlockSpec((1, H_C * DK_C), lambda i, k: (0, 0)),
            pl.BlockSpec((1, DV_C), lambda i, k: (0, 0)),
        ],
        out_specs=[
            pl.BlockSpec((bb, n, GROUP_W), lambda i, k: (i, k, 0)),
            pl.BlockSpec((bb, npair, DV_C, pw), lambda i, k: (i, 0, 0, 0)),
        ],
        out_shape=[jax.ShapeDtypeStruct((b, t, GROUP_W), F32),
                   jax.ShapeDtypeStruct((b, npair, DV_C, pw), F32)],
        scratch_shapes=[pltpu.VMEM((bb, npair, DV_C, pw), F32)],
        compiler_params=_cparams("parallel", "arbitrary"),
        name="gla",
    )(proj, proj, proj, proj, s0, wgk, bgk, gain)


def _mlstm_kernel(qk_ref, v_ref, og_ref, if_ref, c0_ref, n0_ref, m0_ref, bif_ref, gain_ref,
                  h_ref, c_ref, nn_ref, m_ref, c_scr, n_scr, m_scr, *, groups):
    t = pl.program_id(1)
    bb, n, _ = qk_ref.shape
    rows = groups * n
    nprob = bb // groups
    npair = H_D // 2
    pw = 2 * DK_D
    qkw = H_D * DK_D
    r2 = 2 * rows

    @pl.when(t == 0)
    def _():
        c_scr[...] = c0_ref[...]
        n_scr[...] = n0_ref[...]
        m_scr[...] = m0_ref[...]

    x = if_ref[...].reshape(bb * n, LANES) + bif_ref[...]
    lf = _log_sigmoid(x)
    qk = qk_ref[...].reshape(bb * n, 2 * qkw)
    q = qk[:, :qkw]
    k = qk[:, qkw:] * (DK_D ** -0.5)
    v = v_ref[...].reshape(bb * n, GROUP_W)
    og = og_ref[...].reshape(bb * n, GROUP_W)
    incl1, _ = _group_masks(rows, n)
    incl2, _ = _group_masks(r2, n)
    r_i, c_i = _iota2((r2, r2), 0), _iota2((r2, r2), 1)
    same2 = (r_i & -n) == (c_i & -n)
    last2 = jnp.logical_and(same2, (c_i & (n - 1)) == n - 1)
    lane = _iota2((1, LANES), 1)
    ones = jnp.ones((r2, LANES), F32)
    ones_sq = jnp.ones((LANES, LANES), F32)
    fcums = [_mm01(incl1, lf[p * rows:(p + 1) * rows], 3) for p in range(nprob)]
    m_rows = [m_scr[b_] for b_ in range(bb)]
    m_old = list(m_rows)
    units = [(p, u) for p in range(nprob) for u in range(npair)]

    def per_block(fn):
        return jnp.concatenate([jnp.broadcast_to(fn(j, gi), (n, fn(j, gi).shape[1]))
                                for j in range(2) for gi in range(groups)], axis=0)

    pre = []
    for p, u in units:
        rs = slice(p * rows, (p + 1) * rows)
        sl = slice(u * pw, (u + 1) * pw)
        fsel = jnp.concatenate([jnp.where(lane == H_D + 2 * u + j, fcums[p], 0.0) for j in range(2)], axis=0)
        isel = jnp.concatenate([jnp.where(lane == 2 * u + j, x[rs], 0.0) for j in range(2)], axis=0)
        cols = _mm_x01(jnp.concatenate([fsel, isel], axis=0), ones_sq, 3)
        fcol, icol = cols[:r2], cols[r2:]
        drow = _mm_nt01(ones, isel - fsel, 3)
        flast = _mm01(last2, fcol, 3)
        mprev = per_block(lambda j, gi: m_old[p * groups + gi][:, 2 * u + j:2 * u + j + 1])
        nmat = per_block(lambda j, gi: n_scr[p * groups + gi][:, sl])
        log_d = jnp.where(incl2, fcol + drow, -jnp.inf)
        m_in = jnp.max(log_d, axis=-1, keepdims=True)
        m_e = jnp.max(jnp.where(same2, flast + drow, -jnp.inf), axis=-1, keepdims=True)
        m_t = jnp.maximum(fcol + mprev, m_in)
        w_in = jnp.exp(fcol + mprev - m_t)
        m_new = jnp.maximum(flast + mprev, m_e)
        qst, kst = _stack_pair(q[rs, sl]), _stack_pair(k[rs, sl])
        vst = jnp.concatenate([v[rs, (2 * u + j) * DV_D:(2 * u + j + 1) * DV_D] for j in range(2)], axis=0)
        pre.append(dict(qst=qst, kst=kst, vst=vst, log_d=log_d, m_t=m_t, w_in=w_in, m_new=m_new,
                        cd=jnp.exp(flast + mprev - m_new), nmat=nmat,
                        ke=kst * jnp.exp(flast - fcol + icol - m_new)))
    dms = [jnp.exp(u["log_d"] - u["m_t"]) * _mm_nt(u["qst"], u["kst"]) for u in pre]
    intra = [_mm(dm, u["vst"]) for dm, u in zip(dms, pre)]
    inter = []
    for (p, uidx), u in zip(units, pre):
        qw = u["w_in"] * u["qst"]
        pieces = [_mm(_group_rows(qw, rows, n, gi), c_scr[p * groups + gi, uidx]) for gi in range(groups)]
        inter.append(_ungroup_rows(pieces, n))
    for (p, uidx), u, dm, o_in, o_x in zip(units, pre, dms, intra, inter):
        sl = slice(uidx * pw, (uidx + 1) * pw)
        den = (u["w_in"] * _mm_x01(u["qst"] * u["nmat"], ones_sq, 1)
               + _mm_x01(dm, ones_sq, 2))
        hout = (o_in + o_x) / jnp.maximum(jnp.abs(den), jnp.exp(-u["m_t"]))
        for gi in range(groups):
            b_ = p * groups + gi
            r0, r1 = gi * n, rows + gi * n
            cdrow = jnp.where(lane < DK_D, u["cd"][r0:r0 + 1], u["cd"][r1:r1 + 1])
            ke_g = _group_rows(u["ke"], rows, n, gi)
            cdcol = jnp.where(_iota2((pw, 1), 0) < DK_D, u["cd"][r0:r0 + 1], u["cd"][r1:r1 + 1])
            c_scr[b_, uidx] = c_scr[b_, uidx] * cdcol + _mm_tn(ke_g, _group_rows(u["vst"], rows, n, gi))
            n_scr[b_, :, sl] = n_scr[b_][:, sl] * cdrow + jnp.sum(ke_g, axis=0, keepdims=True)
            for j, r in ((0, r0), (1, r1)):
                m_rows[b_] = jnp.where(lane == 2 * uidx + j, u["m_new"][r:r + 1], m_rows[b_])
        for j in range(2):
            h = 2 * uidx + j
            ogh = og[p * rows:(p + 1) * rows, h * DV_D:(h + 1) * DV_D]
            val = _sigmoid(ogh) * (_rms(hout[j * rows:(j + 1) * rows]) * gain_ref[...])
            h_ref[p * groups:(p + 1) * groups, :, h * DV_D:(h + 1) * DV_D] = val.reshape(groups, n, DV_D)
    for b_ in range(bb):
        m_scr[b_] = m_rows[b_]

    @pl.when(t == pl.num_programs(1) - 1)
    def _():
        c_ref[...] = c_scr[...]
        nn_ref[...] = n_scr[...]
        m_ref[...] = m_scr[...]


def _mlstm_call(proj, c0, n0, m0, bif, gain, bb, n, groups):
    b, t, _ = proj.shape
    qkw = H_D * DK_D
    npair, pw = H_D // 2, 2 * DK_D
    return pl.pallas_call(
        functools.partial(_mlstm_kernel, groups=groups),
        grid=(b // bb, t // n),
        in_specs=[
            pl.BlockSpec((bb, n, GROUP_W), lambda i, k: (i, k, OD_DQK // GROUP_W)),
            pl.BlockSpec((bb, n, GROUP_W), lambda i, k: (i, k, OD_DV // GROUP_W)),
            pl.BlockSpec((bb, n, GROUP_W), lambda i, k: (i, k, OD_DO // GROUP_W)),
            pl.BlockSpec((bb, n, LANES), lambda i, k: (i, k, OD_DIF // LANES)),
            pl.BlockSpec((bb, npair, DV_D, pw), lambda i, k: (i, 0, 0, 0)),
            pl.BlockSpec((bb, 1, qkw), lambda i, k: (i, 0, 0)),
            pl.BlockSpec((bb, 1, LANES), lambda i, k: (i, 0, 0)),
            pl.BlockSpec((1, LANES), lambda i, k: (0, 0)),
            pl.BlockSpec((1, DV_D), lambda i, k: (0, 0)),
        ],
        out_specs=[
            pl.BlockSpec((bb, n, GROUP_W), lambda i, k: (i, k, 0)),
            pl.BlockSpec((bb, npair, DV_D, pw), lambda i, k: (i, 0, 0, 0)),
            pl.BlockSpec((bb, 1, qkw), lambda i, k: (i, 0, 0)),
            pl.BlockSpec((bb, 1, LANES), lambda i, k: (i, 0, 0)),
        ],
        out_shape=[jax.ShapeDtypeStruct((b, t, GROUP_W), F32),
                   jax.ShapeDtypeStruct((b, npair, DV_D, pw), F32),
                   jax.ShapeDtypeStruct((b, 1, qkw), F32),
                   jax.ShapeDtypeStruct((b, 1, LANES), F32)],
        scratch_shapes=[pltpu.VMEM((bb, npair, DV_D, pw), F32),
                        pltpu.VMEM((bb, 1, qkw), F32),
                        pltpu.VMEM((bb, 1, LANES), F32)],
        compiler_params=_cparams("parallel", "arbitrary"),
        name="mlstm",
    )(proj, proj, proj, proj, c0, n0, m0, bif, gain)


def _pad_cols(w, width):
    return jnp.pad(w, ((0, 0), (0, width - w.shape[1])))


def _pad_rows(w, rows, at=0):
    return jnp.pad(w, ((at, rows - at - w.shape[0]), (0, 0)))


def _even_in_weight(w):
    pa, pb = w[:, :2056], w[:, 2056:]
    cols = [
        pb[:, 0:1536],
        pa[:, 0:1536],
        pa[:, 1544:2056],
        pb[:, 1536:1792],
        _pad_cols(pa[:, 1536:1544], LANES),
    ]
    return _pad_cols(jnp.concatenate(cols, axis=1), EV_COLS).astype(BF16)


def _odd_in_weight(w):
    pc, pd = w[:, :1552], w[:, 1552:]
    cols = [
        pc[:, 0:512],
        pd[:, 0:512],
        pc[:, 512:1024],
        pc[:, 1040:1552],
        pd[:, 512:1024],
        pd[:, 1032:1544],
        _pad_cols(pc[:, 1024:1040], LANES),
        _pad_cols(pd[:, 1024:1032], LANES),
    ]
    return jnp.concatenate(cols, axis=1).astype(BF16)


def _row(v):
    return v.reshape(1, -1).astype(F32)


def _tiles(x, rows=1024):
    b, t, _ = x.shape
    tb = min(t, rows)
    bb = min(b, rows // tb)
    return bb, tb


def _trunk(x, mod, states, wts):
    xbuf, s_delta, s_rwkv, s_gla, s_mc, s_mn, s_mm = states
    b, t, _ = x.shape
    bb, tb = _tiles(x)
    fbb, ftb = _tiles(x, FFN_ROWS)
    pbb, ptb = _tiles(x, PROJ_ROWS)
    n = min(CHUNK, t)
    groups = CHUNK // n
    rbb = (PROBLEMS_LONG if t > CHUNK else PROBLEMS) * groups
    new_even = ([], [], [])
    new_odd = ([], [], [], [])
    for l in range(DEPTH):
        lw = wts["layers"][l]
        m_l = mod[l]
        x = _ffn_call(x, m_l, lw["gain0"], lw["wg0"], lw["wu0"], lw["wd0"], wts["final_gain"], 0, fbb, ftb,
                      FFN_COLS, False)
        i = l // 2
        if l % 2 == 0:
            h_tail, proj = _adaln_proj_call(x, m_l, lw["gain1"], lw["w_in"], pbb, ptb, True)
            if xbuf is None:
                carry = jnp.zeros((b, SUBLANES, EV_COLS), F32)
            else:
                rows = _rows_proj_call(xbuf[i].reshape(b * (CONV_W - 1), D_MODEL), lw["w_in"], 1024)
                carry = jnp.pad(rows.reshape(b, CONV_W - 1, EV_COLS),
                                ((0, 0), (SUBLANES - CONV_W + 1, 0), (0, 0)))
            oa, sd = _delta_call(proj, carry, s_delta[i], lw["conv_w"], lw["delta_hp"], lw["gain_a"],
                                 rbb, n, groups)
            ob, sr = _rwkv_call(proj, carry, _rwkv_state_to_pairs(s_rwkv[i]), lw["mu_rkv"], lw["mu_wag"],
                                lw["w_lora"], lw["rwkv_prm"], rbb, n, groups)
            for lst, val in zip(new_even, (h_tail[:, SUBLANES - (CONV_W - 1):], sd, _rwkv_state_from_pairs(sr))):
                lst.append(val)
        else:
            (proj,) = _adaln_proj_call(x, m_l, lw["gain1"], lw["w_in"], pbb, ptb, False)
            pair_shape = (b, H_C // 2, 2 * DK_C, DV_C)
            oa, sg = _gla_call(proj, s_gla[i].reshape(pair_shape), lw["w_gk2"], lw["b_gk"], lw["gain_c"],
                               rbb, n, groups)
            ob, sc, sn, sm = _mlstm_call(proj, s_mc[i].reshape(pair_shape), s_mn[i].reshape(b, 1, H_D * DK_D),
                                         _pad_cols(s_mm[i], LANES).reshape(b, 1, LANES),
                                         lw["b_if"], lw["gain_d"], rbb, n, groups)
            for lst, val in zip(new_odd, (sg.reshape(b, H_C, DK_C, DV_C), sc.reshape(b, H_D, DK_D, DV_D),
                                          sn.reshape(b, H_D, DK_D), sm[:, 0, :H_D])):
                lst.append(val)
        x = _ffn_call(x, m_l, lw["gain2"], lw["wg1"], lw["wu1"], lw["wd1"], wts["final_gain"], 6, fbb, ftb,
                      FFN_COLS, l == DEPTH - 1, mixer=(oa, ob, lw["w_out"]))
    stacked = [jnp.stack(lst) for lst in new_even + new_odd]
    return (x, *stacked)


def _prepare_weights(norm_gain, final_gain, w_ffn_gate, w_ffn_up, w_ffn_down, w_in_even, w_out_even,
                     conv_w, a_log, dt_bias, gain_a, mu_b, w0_b, w_w2, a0_b, w_a2, w_g2, k_k, k_a, r_k,
                     lnx_gain, lnx_bias, w_in_odd, w_out_odd, w_gk2, b_gk, gain_c, b_i, b_f, gain_d):
    layers = []
    for l in range(DEPTH):
        i = l // 2
        lw = {
            "gain0": _row(norm_gain[l, 0]), "gain1": _row(norm_gain[l, 1]), "gain2": _row(norm_gain[l, 2]),
            "wg0": w_ffn_gate[l, 0].astype(BF16), "wu0": w_ffn_up[l, 0].astype(BF16),
            "wd0": w_ffn_down[l, 0].astype(BF16),
            "wg1": w_ffn_gate[l, 1].astype(BF16), "wu1": w_ffn_up[l, 1].astype(BF16),
            "wd1": w_ffn_down[l, 1].astype(BF16),
        }
        if l % 2 == 0:
            lw["w_in"] = _even_in_weight(w_in_even[i])
            lw["w_out"] = w_out_even[i].astype(BF16)
            lw["conv_w"] = _pad_rows(conv_w[i].astype(F32), SUBLANES)
            lw["delta_hp"] = _pad_rows(jnp.stack([_pad_cols(_row(a_log[i]), LANES)[0],
                                                  _pad_cols(_row(dt_bias[i]), LANES)[0]]), SUBLANES)
            lw["gain_a"] = _row(gain_a[i])
            lw["mu_rkv"] = _row(mu_b[i, :1536])
            lw["mu_wag"] = _row(mu_b[i, 1536:])
            lora = jnp.zeros((2 * LANES, 3 * GROUP_W), F32)
            lora = lora.at[0:64, 0:GROUP_W].set(w_w2[i])
            lora = lora.at[64:128, GROUP_W:2 * GROUP_W].set(w_a2[i])
            lora = lora.at[128:256, 2 * GROUP_W:].set(w_g2[i])
            lw["w_lora"] = lora.astype(BF16)
            lw["rwkv_prm"] = jnp.stack([w0_b[i], a0_b[i], k_k[i], k_a[i], r_k[i].reshape(-1),
                                        lnx_gain[i], lnx_bias[i], jnp.zeros_like(w0_b[i])]).astype(F32)
        else:
            lw["w_in"] = _odd_in_weight(w_in_odd[i])
            lw["w_out"] = w_out_odd[i].astype(BF16)
            lw["w_gk2"] = _pad_rows(w_gk2[i], LANES).astype(BF16)
            lw["b_gk"] = _row(b_gk[i])
            lw["gain_c"] = _row(gain_c[i])
            lw["b_if"] = _pad_cols(_row(jnp.concatenate([b_i[i], b_f[i]])), LANES)
            lw["gain_d"] = _row(gain_d[i])
        layers.append(lw)
    return {"layers": layers, "final_gain": _row(final_gain)}


def kernel(x_prompt, x_sample, c_prompt, c_sample, state_xbuf_even, state_delta, state_rwkv, state_gla,
           state_mlstm_c, state_mlstm_n, state_mlstm_m, w_mod, b_mod, norm_gain, final_gain, w_ffn_gate,
           w_ffn_up, w_ffn_down, w_in_even, w_out_even, conv_w, a_log, dt_bias, gain_a, mu_b, w0_b, w_w2,
           a0_b, w_a2, w_g2, k_k, k_a, r_k, lnx_gain, lnx_bias, w_in_odd, w_out_odd, w_gk2, b_gk, gain_c,
           b_i, b_f, gain_d):
    wts = _prepare_weights(norm_gain, final_gain, w_ffn_gate, w_ffn_up, w_ffn_down, w_in_even, w_out_even,
                           conv_w, a_log, dt_bias, gain_a, mu_b, w0_b, w_w2, a0_b, w_a2, w_g2, k_k, k_a,
                           r_k, lnx_gain, lnx_bias, w_in_odd, w_out_odd, w_gk2, b_gk, gain_c, b_i, b_f,
                           gain_d)
    bp, bs = x_prompt.shape[0], x_sample.shape[0]
    c_all = jnp.concatenate([c_prompt, c_sample], axis=0).astype(F32)
    mod_p, mod_s = _mod_call(c_all, bp, w_mod, b_mod)

    def zeros(shape):
        return jnp.zeros(shape, F32)

    zero_states = (
        None,
        zeros((N_EVEN, bp, H_A, DK_A, DK_A)),
        zeros((N_EVEN, bp, H_B, HD_B, HD_B)),
        zeros((N_ODD, bp, H_C, DK_C, DV_C)),
        zeros((N_ODD, bp, H_D, DK_D, DV_D)),
        zeros((N_ODD, bp, H_D, DK_D)),
        zeros((N_ODD, bp, H_D)),
    )
    y_p, xb_p, dl_p, rw_p, gl_p, mc_p, mn_p, mm_p = _trunk(x_prompt, mod_p, zero_states, wts)
    sample_states = (state_xbuf_even, state_delta, state_rwkv, state_gla,
                     state_mlstm_c, state_mlstm_n, state_mlstm_m)
    y_s, xb_s, dl_s, rw_s, gl_s, mc_s, mn_s, mm_s = _trunk(x_sample, mod_s, sample_states, wts)
    return (y_p, y_s, xb_p, xb_s, dl_p, dl_s, rw_p, rw_s, gl_p, gl_s, mc_p, mc_s, mn_p, mn_s, mm_p, mm_s)
```

```python
import functools

import jax
import jax.numpy as jnp
from jax import lax
from jax.experimental import pallas as pl
from jax.experimental.pallas import tpu as pltpu

F32 = jnp.float32
BF16 = jnp.bfloat16
HIGHEST = lax.Precision.HIGHEST

D_MODEL = 1024
DEPTH = 4
N_EVEN = 2
N_ODD = 2
D_FF = 2816
N_MOD = 9
EPS = 1e-6
GN_EPS = 64e-5
CONV_W = 4
H_A, DK_A = 4, 128
H_B, HD_B = 8, 64
H_C, DK_C, DV_C = 4, 64, 128
H_D, DK_D, DV_D = 4, 64, 128
GLA_NORM = 16.0
GROUP_W = 512

LANES = 128
SUBLANES = 8
BF16_ROWS = 16
VMEM_LIMIT_BYTES = 48 * 1024 * 1024

EV_RKV, EV_QKV, EV_Z, EV_WAG, EV_AB, EV_COLS = 0, 1536, 3072, 3584, 3840, 4096
OD_CQK, OD_DQK, OD_CV, OD_CZ, OD_DV, OD_DO, OD_CGK, OD_DIF, OD_COLS = (
    0, 512, 1024, 1536, 2048, 2560, 3072, 3200, 3328)

CHUNK = 64
GLA_SUB = 16
PROJ_ROWS = 512
PROJ_COLS = 256
FFN_ROWS = 1024
FFN_COLS = 256
PROBLEMS = 2
PROBLEMS_LONG = 8


def _bf(x):
    if x.dtype == BF16:
        return x
    if x.shape[-2] % BF16_ROWS == 0 and x.shape[-1] % BF16_ROWS == 0:
        return x.astype(BF16)
    return x


def _pair(a, b):
    a, b = _bf(a), _bf(b)
    if a.dtype != b.dtype:
        a, b = a.astype(F32), b.astype(F32)
    return a, b


def _mm(a, b):
    a, b = _pair(a, b)
    return jnp.dot(a, b, preferred_element_type=F32)


def _mm_nt(a, b):
    a, b = _pair(a, b)
    return lax.dot_general(a, b, (((1,), (1,)), ((), ())), preferred_element_type=F32)


def _mm_tn(a, b):
    a, b = _pair(a, b)
    return lax.dot_general(a, b, (((0,), (0,)), ((), ())), preferred_element_type=F32)


def _mm_f32(a, b):
    return jnp.dot(a, b, precision=HIGHEST, preferred_element_type=F32)


def _mm_nt_f32(a, b):
    return lax.dot_general(a, b, (((1,), (1,)), ((), ())), precision=HIGHEST,
                           preferred_element_type=F32)


def _split_bf16(x, parts):
    out, r = [], x
    for i in range(parts):
        p = r.astype(BF16)
        out.append(p)
        if i + 1 < parts:
            r = r - p.astype(F32)
    return out


def _mm01(a01, x, parts):
    a = a01.astype(BF16)
    acc = None
    for p in _split_bf16(x, parts):
        d = jnp.dot(a, p, preferred_element_type=F32)
        acc = d if acc is None else acc + d
    return acc


def _mm_x01(x, b01, parts):
    b = b01.astype(BF16)
    acc = None
    for p in _split_bf16(x, parts):
        d = jnp.dot(p, b, preferred_element_type=F32)
        acc = d if acc is None else acc + d
    return acc


def _mm_nt01(a01, x, parts):
    a = a01.astype(BF16)
    acc = None
    for p in _split_bf16(x, parts):
        d = lax.dot_general(a, p, (((1,), (1,)), ((), ())), preferred_element_type=F32)
        acc = d if acc is None else acc + d
    return acc


def _sigmoid(x):
    return jax.nn.sigmoid(x)


def _silu(x):
    return x * jax.nn.sigmoid(x)


def _softplus(x):
    return jnp.maximum(x, 0.0) + jnp.log1p(jnp.exp(-jnp.abs(x)))


def _log_sigmoid(x):
    return -_softplus(-x)


def _rms(x, eps=EPS):
    return x * lax.rsqrt(jnp.mean(x * x, axis=-1, keepdims=True) + eps)


def _l2n(x):
    return x * lax.rsqrt(jnp.sum(x * x, axis=-1, keepdims=True) + 1e-6)


def _iota2(shape, dim):
    return lax.broadcasted_iota(jnp.int32, shape, dim)


def _tril_masks(n):
    r, c = _iota2((n, n), 0), _iota2((n, n), 1)
    return r >= c, r > c


def _group_masks(size, n):
    r, c = _iota2((size, size), 0), _iota2((size, size), 1)
    same = (r & -n) == (c & -n)
    return jnp.logical_and(same, r >= c), jnp.logical_and(same, r > c)


def _eye(n):
    return (_iota2((n, n), 0) == _iota2((n, n), 1)).astype(F32)


def _transpose_rows(x):
    return _mm_nt_f32(_eye(x.shape[1]), x)


def _unit_lower_inv_many(mats, n):
    eye = _eye(mats[0].shape[0])
    ms = [-a for a in mats]
    ps = [eye + m for m in ms]
    covered = 2
    while covered < n:
        ms = [_mm(m, m) for m in ms]
        ps = [p + _mm(p, m) for p, m in zip(ps, ms)]
        covered *= 2
    return ps


def _cparams(*sem):
    return pltpu.CompilerParams(dimension_semantics=sem, vmem_limit_bytes=VMEM_LIMIT_BYTES)


def _mod_kernel(c_ref, w_ref, b_ref, op_ref, os_ref):
    cs = _silu(c_ref[...])
    m = _mm(cs, w_ref[...]) + b_ref[...]
    nprompt = op_ref.shape[0]
    op_ref[...] = m[:nprompt]
    os_ref[...] = m[nprompt:]


def _mod_call(c_all, nprompt, w_mod, b_mod):
    rows = c_all.shape[0]
    tn = 1024
    width = N_MOD * D_MODEL
    return pl.pallas_call(
        _mod_kernel,
        grid=(DEPTH, width // tn),
        in_specs=[
            pl.BlockSpec((rows, D_MODEL), lambda l, j: (0, 0)),
            pl.BlockSpec((None, D_MODEL, tn), lambda l, j: (l, 0, j)),
            pl.BlockSpec((None, 1, tn), lambda l, j: (l, 0, j)),
        ],
        out_specs=[pl.BlockSpec((None, nprompt, tn), lambda l, j: (l, 0, j)),
                   pl.BlockSpec((None, rows - nprompt, tn), lambda l, j: (l, 0, j))],
        out_shape=[jax.ShapeDtypeStruct((DEPTH, nprompt, width), F32),
                   jax.ShapeDtypeStruct((DEPTH, rows - nprompt, width), F32)],
        compiler_params=_cparams("parallel", "parallel"),
        name="mod",
    )(c_all, w_mod, b_mod.reshape(DEPTH, 1, width))


def _mod_rows(mod_ref, j, batch_block, bb, tb):
    cols = slice(j * D_MODEL, (j + 1) * D_MODEL)
    if bb == 1:
        return mod_ref[pl.ds(batch_block, 1), cols]
    rep = (_iota2((bb * tb, bb), 0) & -tb) == _iota2((bb * tb, bb), 1) * tb
    return _mm01(rep.astype(F32), mod_ref[:, cols], 3).reshape(bb, tb, D_MODEL)


def _mod_table_spec(mod, bb, ngrid):
    nb, width = mod.shape
    if bb == 1:
        return pl.BlockSpec((nb, width), (lambda i, k, f: (0, 0)) if ngrid == 3 else (lambda i, k: (0, 0)))
    return pl.BlockSpec((bb, width), (lambda i, k, f: (i, 0)) if ngrid == 3 else (lambda i, k: (i, 0)))


def _adaln(x, gain, scale, shift):
    return _rms(x) * gain * (1.0 + scale) + shift


def _ffn_kernel(x_ref, mod_ref, gain_ref, wg_ref, wu_ref, wd_ref, fgain_ref, *refs,
                j0, final_norm, mixer_in):
    if mixer_in:
        oa_ref, ob_ref, wo_ref, o_ref, h_scr, acc_scr, xm_scr = refs
    else:
        o_ref, h_scr, acc_scr = refs
    ib = pl.program_id(0)
    f = pl.program_id(2)
    bb, tb, d = x_ref.shape

    @pl.when(f == 0)
    def _():
        x = x_ref[...]
        if mixer_in:
            o = jnp.concatenate([oa_ref[...], ob_ref[...]], axis=-1).reshape(bb * tb, d)
            y = jnp.dot(o.astype(BF16), wo_ref[...], preferred_element_type=F32).reshape(bb, tb, d)
            x = x + (1.0 + _mod_rows(mod_ref, 5, ib, bb, tb)) * y
            xm_scr[...] = x
        h = _adaln(x, gain_ref[...], _mod_rows(mod_ref, j0 + 1, ib, bb, tb), _mod_rows(mod_ref, j0, ib, bb, tb))
        h_scr[...] = h.reshape(bb * tb, d).astype(BF16)
        acc_scr[...] = jnp.zeros_like(acc_scr)

    h = h_scr[...]
    g = jnp.dot(h, wg_ref[...], preferred_element_type=F32)
    u = jnp.dot(h, wu_ref[...], preferred_element_type=F32)
    a = (_silu(g) * u).astype(BF16)
    acc_scr[...] += jnp.dot(a, wd_ref[...], preferred_element_type=F32)

    @pl.when(f == pl.num_programs(2) - 1)
    def _():
        y = acc_scr[...].reshape(bb, tb, d)
        base = xm_scr[...] if mixer_in else x_ref[...]
        out = base + 0.5 * (1.0 + _mod_rows(mod_ref, j0 + 2, ib, bb, tb)) * y
        if final_norm:
            out = _rms(out) * fgain_ref[...]
        o_ref[...] = out


def _ffn_call(x, mod, gain, wg, wu, wd, widx, fgain, j0, bb, tb, tf, final_norm, mixer=None):
    wl, wk = widx
    b, t, d = x.shape
    xspec = pl.BlockSpec((bb, tb, d), lambda i, k, f: (i, k, 0))
    rowspec = pl.BlockSpec((1, d), lambda i, k, f: (0, 0))
    in_specs = [
        xspec,
        _mod_table_spec(mod, bb, 3),
        rowspec,
        pl.BlockSpec((None, None, d, tf), lambda i, k, f: (wl, wk, 0, f)),
        pl.BlockSpec((None, None, d, tf), lambda i, k, f: (wl, wk, 0, f)),
        pl.BlockSpec((None, None, tf, d), lambda i, k, f: (wl, wk, f, 0)),
        rowspec,
    ]
    args = [x, mod, gain, wg, wu, wd, fgain]
    scratch = [pltpu.VMEM((bb * tb, d), BF16), pltpu.VMEM((bb * tb, d), F32)]
    if mixer is not None:
        oa, ob, wo = mixer
        hspec = pl.BlockSpec((bb, tb, GROUP_W), lambda i, k, f: (i, k, 0))
        in_specs += [hspec, hspec, pl.BlockSpec((d, d), lambda i, k, f: (0, 0))]
        args += [oa, ob, wo]
        scratch.append(pltpu.VMEM((bb, tb, d), F32))
    return pl.pallas_call(
        functools.partial(_ffn_kernel, j0=j0, final_norm=final_norm, mixer_in=mixer is not None),
        grid=(b // bb, t // tb, D_FF // tf),
        in_specs=in_specs,
        out_specs=xspec,
        out_shape=jax.ShapeDtypeStruct(x.shape, x.dtype),
        scratch_shapes=scratch,
        compiler_params=_cparams("parallel", "parallel", "arbitrary"),
        name="ffn",
    )(*args)


def _adaln_proj_kernel(x_ref, mod_ref, gain_ref, w_ref, *refs, tail):
    p_ref = refs[-1]
    ib = pl.program_id(0)
    bb, tb, d = x_ref.shape
    h = _adaln(x_ref[...], gain_ref[...], _mod_rows(mod_ref, 4, ib, bb, tb), _mod_rows(mod_ref, 3, ib, bb, tb))
    if tail:
        refs[0][...] = h[:, tb - SUBLANES:, :]
    hb = h.reshape(bb * tb, d).astype(BF16)
    for c0 in range(0, w_ref.shape[1], PROJ_COLS):
        p = jnp.dot(hb, w_ref[:, c0:c0 + PROJ_COLS], preferred_element_type=F32)
        p_ref[:, :, c0:c0 + PROJ_COLS] = p.reshape(bb, tb, PROJ_COLS)


def _adaln_proj_call(x, mod, gain, w, bb, tb, tail):
    b, t, d = x.shape
    n = w.shape[1]
    xspec = pl.BlockSpec((bb, tb, d), lambda i, k: (i, k, 0))
    out_specs = [pl.BlockSpec((bb, tb, n), lambda i, k: (i, k, 0))]
    out_shape = [jax.ShapeDtypeStruct((b, t, n), F32)]
    if tail:
        out_specs.insert(0, pl.BlockSpec((bb, SUBLANES, d), lambda i, k: (i, 0, 0)))
        out_shape.insert(0, jax.ShapeDtypeStruct((b, SUBLANES, d), F32))
    return pl.pallas_call(
        functools.partial(_adaln_proj_kernel, tail=tail),
        grid=(b // bb, t // tb),
        in_specs=[
            xspec, _mod_table_spec(mod, bb, 2),
            pl.BlockSpec((1, d), lambda i, k: (0, 0)),
            pl.BlockSpec((d, n), lambda i, k: (0, 0)),
        ],
        out_specs=out_specs,
        out_shape=out_shape,
        compiler_params=_cparams("parallel", "arbitrary"),
        name="adaln_proj",
    )(x, mod, gain, w)


def _rows_proj_kernel(a_ref, w_ref, o_ref):
    o_ref[...] = _mm(a_ref[...], w_ref[...])


def _rows_proj_call(a, w, tn):
    m, k = a.shape
    n = w.shape[1]
    return pl.pallas_call(
        _rows_proj_kernel,
        grid=(n // tn,),
        in_specs=[pl.BlockSpec((m, k), lambda j: (0, 0)), pl.BlockSpec((k, tn), lambda j: (0, j))],
        out_specs=pl.BlockSpec((m, tn), lambda j: (0, j)),
        out_shape=jax.ShapeDtypeStruct((m, n), F32),
        compiler_params=_cparams("parallel"),
        name="rows_proj",
    )(a, w)


def _stack_heads(x, nheads, head_w):
    lane = _iota2((1, x.shape[1]), 1)
    return jnp.concatenate(
        [jnp.where(jnp.logical_and(lane >= h * head_w, lane < (h + 1) * head_w), x, 0.0)
         for h in range(nheads)], axis=0)


def _delta_kernel(qkv_ref, z_ref, ab_ref, carry_ref, s0_ref, cw_ref, hp_ref, gain_ref,
                  o_ref, s_ref, ext_scr, s_scr, *, groups):
    t = pl.program_id(1)
    bb, n, _ = qkv_ref.shape
    rows = groups * n
    nprob = bb // groups
    hr = H_A * rows

    @pl.when(t == 0)
    def _():
        ext_scr[:, 0:SUBLANES, :] = carry_ref[...]
        s_scr[...] = s0_ref[...]

    @pl.when(t > 0)
    def _():
        ext_scr[:, 0:SUBLANES, :] = ext_scr[:, n:n + SUBLANES, :]

    ext_scr[:, SUBLANES:SUBLANES + n, :] = qkv_ref[...]
    conv = cw_ref[0:1, :] * ext_scr[:, 5:5 + n, :]
    for j in range(1, CONV_W):
        conv = conv + cw_ref[j:j + 1, :] * ext_scr[:, 5 + j:5 + j + n, :]
    x = _silu(conv).reshape(bb * n, 3 * GROUP_W)
    ab = ab_ref[...].reshape(bb * n, LANES)
    g = -jnp.exp(hp_ref[0:1, :]) * _softplus(ab + hp_ref[1:2, :])
    beta = _sigmoid(ab)
    z = z_ref[...].reshape(bb * n, GROUP_W)

    incl1, _ = _group_masks(rows, n)
    incl, strict = _group_masks(hr, n)
    lane = _iota2((1, LANES), 1)
    ones = jnp.ones((hr, LANES), F32)
    probs = range(nprob)

    def head_rows(a, h):
        return a[:, h * DK_A:(h + 1) * DK_A]

    gcol, bcol, qs, ks, vs, zs = [], [], [], [], [], []
    kst, lhs = [], []
    grow = []
    for p in probs:
        sl = slice(p * rows, (p + 1) * rows)
        gc = _mm01(incl1, g[sl], 3)
        gsel = jnp.concatenate([jnp.where(lane == h, gc, 0.0) for h in range(H_A)], axis=0)
        bsel = jnp.concatenate([jnp.where(lane == H_A + h, beta[sl], 0.0) for h in range(H_A)], axis=0)
        gcol.append(jnp.sum(gsel, axis=-1, keepdims=True))
        bcol.append(jnp.sum(bsel, axis=-1, keepdims=True))
        grow.append(_mm_nt01(ones, gsel, 3))
        xp = x[sl]
        qn = jnp.concatenate([_l2n(head_rows(xp[:, 0:GROUP_W], h)) * (DK_A ** -0.5)
                              for h in range(H_A)], axis=1)
        kn = jnp.concatenate([_l2n(head_rows(xp[:, GROUP_W:2 * GROUP_W], h)) for h in range(H_A)], axis=1)
        qs.append(jnp.concatenate([head_rows(qn, h) for h in range(H_A)], axis=0))
        ks.append(jnp.concatenate([head_rows(kn, h) for h in range(H_A)], axis=0))
        vs.append(jnp.concatenate([head_rows(xp[:, 2 * GROUP_W:], h) for h in range(H_A)], axis=0))
        zs.append(z[sl])
        k_st = _stack_heads(kn, H_A, DK_A)
        kst.append(k_st)
        lhs.append(jnp.concatenate([k_st * bcol[p], _stack_heads(qn, H_A, DK_A)], axis=0))

    kq = [_mm_nt(lhs[p], kst[p]) for p in probs]
    dec = [jnp.exp(jnp.where(incl, gcol[p] - grow[p], -jnp.inf)) for p in probs]
    a_low = [jnp.where(strict, kq[p][:hr] * dec[p], 0.0) for p in probs]
    tinv = _unit_lower_inv_many(a_low, n)
    eg = [jnp.exp(gcol[p]) for p in probs]
    kb = [ks[p] * bcol[p] for p in probs]
    sol = [_mm(tinv[p], jnp.concatenate([vs[p] * bcol[p], kb[p] * eg[p]], axis=1)) for p in probs]
    qg = [qs[p] * eg[p] for p in probs]

    us, oparts = [], []
    for p in probs:
        u_rows, o_rows = [], []
        for h in range(H_A):
            for gi in range(groups):
                r0 = h * rows + gi * n
                s = s_scr[p * groups + gi, h]
                ksq = _mm(jnp.concatenate([sol[p][r0:r0 + n, DK_A:], qg[p][r0:r0 + n]], axis=0), s)
                u_rows.append(sol[p][r0:r0 + n, :DK_A] - ksq[:n])
                o_rows.append(ksq[n:])
        us.append(jnp.concatenate(u_rows, axis=0))
        oparts.append(jnp.concatenate(o_rows, axis=0))
    outs = [oparts[p] + _mm(kq[p][hr:] * dec[p], us[p]) for p in probs]
    for p in probs:
        for h in range(H_A):
            for gi in range(groups):
                r0 = h * rows + gi * n
                glast = gcol[p][r0 + n - 1:r0 + n]
                kd = ks[p][r0:r0 + n] * jnp.exp(glast - gcol[p][r0:r0 + n])
                s = s_scr[p * groups + gi, h]
                s_scr[p * groups + gi, h] = s * jnp.exp(glast) + _mm_tn(kd, us[p][r0:r0 + n])
            o = outs[p][h * rows:(h + 1) * rows]
            val = _rms(o) * gain_ref[...] * _silu(head_rows(zs[p], h))
            o_ref[p * groups:(p + 1) * groups, :, h * DK_A:(h + 1) * DK_A] = val.reshape(groups, n, DK_A)

    @pl.when(t == pl.num_programs(1) - 1)
    def _():
        s_ref[...] = s_scr[...]


def _delta_call(proj, carry, s0, layer, cw, hp, gain, bb, n, groups):
    b, t, _ = proj.shape
    return pl.pallas_call(
        functools.partial(_delta_kernel, groups=groups),
        grid=(b // bb, t // n),
        in_specs=[
            pl.BlockSpec((bb, n, 3 * GROUP_W), lambda i, k: (i, k, EV_QKV // (3 * GROUP_W))),
            pl.BlockSpec((bb, n, GROUP_W), lambda i, k: (i, k, EV_Z // GROUP_W)),
            pl.BlockSpec((bb, n, LANES), lambda i, k: (i, k, EV_AB // LANES)),
            pl.BlockSpec((bb, SUBLANES, 3 * GROUP_W), lambda i, k: (i, 0, EV_QKV // (3 * GROUP_W))),
            pl.BlockSpec((None, bb, H_A, DK_A, DK_A), lambda i, k: (layer, i, 0, 0, 0)),
            pl.BlockSpec((SUBLANES, 3 * GROUP_W), lambda i, k: (0, 0)),
            pl.BlockSpec((SUBLANES, LANES), lambda i, k: (0, 0)),
            pl.BlockSpec((1, DK_A), lambda i, k: (0, 0)),
        ],
        out_specs=[
            pl.BlockSpec((bb, n, GROUP_W), lambda i, k: (i, k, 0)),
            pl.BlockSpec((bb, H_A, DK_A, DK_A), lambda i, k: (i, 0, 0, 0)),
        ],
        out_shape=[jax.ShapeDtypeStruct((b, t, GROUP_W), F32),
                   jax.ShapeDtypeStruct((b, H_A, DK_A, DK_A), F32)],
        scratch_shapes=[pltpu.VMEM((bb, n + SUBLANES, 3 * GROUP_W), F32),
                        pltpu.VMEM((bb, H_A, DK_A, DK_A), F32)],
        compiler_params=_cparams("parallel", "arbitrary"),
        name="delta",
    )(proj, proj, proj, carry, s0, cw, hp, gain)


def _rwkv_kernel(rkv_ref, wag_ref, crkv_ref, cwag_ref, s0_ref, mur_ref, muw_ref, wlr_ref, prm_ref,
                 y_ref, s_ref, ext1_scr, ext2_scr, sp_scr, *, groups):
    t = pl.program_id(1)
    bb, n, _ = rkv_ref.shape
    rows = groups * n
    nprob = bb // groups
    npair = H_B // 2
    pw = 2 * HD_B
    r2 = 2 * rows

    @pl.when(t == 0)
    def _():
        ext1_scr[:, 0:SUBLANES, :] = crkv_ref[...]
        ext2_scr[:, 0:SUBLANES, :] = cwag_ref[...]
        sp_scr[...] = s0_ref[...]

    @pl.when(t > 0)
    def _():
        ext1_scr[:, 0:SUBLANES, :] = ext1_scr[:, n:n + SUBLANES, :]
        ext2_scr[:, 0:SUBLANES, :] = ext2_scr[:, n:n + SUBLANES, :]

    cur1 = rkv_ref[...]
    cur2 = wag_ref[...]
    ext1_scr[:, SUBLANES:SUBLANES + n, :] = cur1
    ext2_scr[:, SUBLANES:SUBLANES + n, :] = cur2
    xm1 = cur1 + (ext1_scr[:, SUBLANES - 1:SUBLANES - 1 + n, :] - cur1) * mur_ref[...]
    xm2 = cur2 + (ext2_scr[:, SUBLANES - 1:SUBLANES - 1 + n, :] - cur2) * muw_ref[...]
    xm1 = xm1.reshape(bb * n, 3 * GROUP_W)
    xm2 = xm2.reshape(bb * n, 2 * LANES)
    r = xm1[:, 0:GROUP_W]
    kb = xm1[:, GROUP_W:2 * GROUP_W]
    vb = xm1[:, 2 * GROUP_W:3 * GROUP_W]
    lane2 = _iota2(xm2.shape, 1)
    feat = jnp.where(lane2 < 64, jnp.tanh(xm2), jnp.where(lane2 < 128, xm2, _sigmoid(xm2)))
    lr = _mm(feat, wlr_ref[...])
    w_raw = prm_ref[0:1, :] + lr[:, 0:GROUP_W]
    logw = -jnp.exp(-_softplus(-w_raw) - 0.5)
    a = _sigmoid(prm_ref[1:2, :] + lr[:, GROUP_W:2 * GROUP_W])
    gb = lr[:, 2 * GROUP_W:3 * GROUP_W]
    kkraw = kb * prm_ref[2:3, :]
    k = kb * (1.0 + (a - 1.0) * prm_ref[3:4, :])

    incl1, _ = _group_masks(rows, n)
    incl, strict = _group_masks(r2, n)
    blk = (_iota2((pw, pw), 0) & -HD_B) == (_iota2((pw, pw), 1) & -HD_B)
    blkf = blk.astype(F32)
    cums = [_mm01(incl1, logw[p * rows:(p + 1) * rows], 3) for p in range(nprob)]

    units = [(p, q) for p in range(nprob) for q in range(npair)]

    def st(x):
        return _stack_heads(x, 2, HD_B)

    pre = []
    for p, q in units:
        rs = slice(p * rows, (p + 1) * rows)
        sl = slice(q * pw, (q + 1) * pw)
        kkr = kkraw[rs, sl]
        rp, kp, vp, ap = r[rs, sl], k[rs, sl], vb[rs, sl], a[rs, sl]
        sums = _mm_x01(jnp.concatenate([kkr * kkr, rp * kp * prm_ref[4:5, sl]], axis=0), blkf, 1)
        kk = kkr * lax.rsqrt(sums[:rows] + 1e-6)
        cump = cums[p][:, sl]
        ginv = jnp.exp(-cump)
        rt = rp * jnp.exp(cump)
        at = -kk * jnp.exp(cump - logw[rs, sl])
        bt = kk * ap * ginv
        kt = kp * ginv
        pre.append(dict(rt=rt, at=at, bt=bt, kt=kt, vp=vp, cump=cump, bonus=sums[rows:] * vp,
                        vst=st(vp), gb=gb[rs, sl]))
    mats = [_mm_nt(jnp.concatenate([st(u["at"]), st(u["rt"])], axis=0),
                   jnp.concatenate([st(u["bt"]), st(u["kt"])], axis=0)) for u in pre]
    tinv = _unit_lower_inv_many([-jnp.where(strict, m[:r2, :r2], 0.0) for m in mats], n)

    u0s, y0s = [], []
    for (p, q), u in zip(units, pre):
        u0_rows, y0_rows = [], []
        for gi in range(groups):
            gs = slice(gi * n, (gi + 1) * n)
            uy = _mm_nt(jnp.concatenate([u["at"][gs], u["rt"][gs]], axis=0), sp_scr[p * groups + gi, q])
            u0_rows.append(uy[:n])
            y0_rows.append(uy[n:])
        u0s.append(jnp.concatenate(u0_rows, axis=0))
        y0s.append(jnp.concatenate(y0_rows, axis=0))
    x1 = [_mm(jnp.where(strict, m[:r2, r2:], 0.0), u["vst"]) for m, u in zip(mats, pre)]
    ust = [_mm(ti, st(u0) + x) for ti, u0, x in zip(tinv, u0s, x1)]
    yst = [_mm(jnp.concatenate([jnp.where(incl, m[r2:, :r2], 0.0), jnp.where(incl, m[r2:, r2:], 0.0)], axis=1),
               jnp.concatenate([us_, u["vst"]], axis=0)) for m, us_, u in zip(mats, ust, pre)]
    ys = [y0 + ys_[:rows] + ys_[rows:] for y0, ys_ in zip(y0s, yst)]
    means = [_mm_x01(y, blkf, 1) * (1.0 / HD_B) for y in ys]
    ycs = [y - m for y, m in zip(ys, means)]
    variances = [_mm_x01(yc * yc, blkf, 1) * (1.0 / HD_B) for yc in ycs]

    for (p, q), u, us_, yc, var in zip(units, pre, ust, ycs, variances):
        sl = slice(q * pw, (q + 1) * pw)
        uu = us_[:rows] + us_[rows:]
        for gi in range(groups):
            gs = slice(gi * n, (gi + 1) * n)
            upd = _mm_tn(jnp.concatenate([uu[gs], u["vp"][gs]], axis=0),
                         jnp.concatenate([u["bt"][gs], u["kt"][gs]], axis=0))
            glast = jnp.exp(u["cump"][gi * n + n - 1:gi * n + n, :])
            sp = sp_scr[p * groups + gi, q]
            sp_scr[p * groups + gi, q] = jnp.where(blk, sp + upd, 0.0) * glast
        yn = yc * lax.rsqrt(var + GN_EPS) * prm_ref[5:6, sl] + prm_ref[6:7, sl]
        val = (yn + u["bonus"]) * u["gb"]
        y_ref[p * groups:(p + 1) * groups, :, sl] = val.reshape(groups, n, pw)

    @pl.when(t == pl.num_programs(1) - 1)
    def _():
        s_ref[...] = sp_scr[...]


def _rwkv_call(proj, carry, s0, mur, muw, wlr, prm, bb, n, groups):
    b, t, _ = proj.shape
    wag_w = 2 * LANES
    npair, pw = H_B // 2, 2 * HD_B
    return pl.pallas_call(
        functools.partial(_rwkv_kernel, groups=groups),
        grid=(b // bb, t // n),
        in_specs=[
            pl.BlockSpec((bb, n, 3 * GROUP_W), lambda i, k: (i, k, EV_RKV // (3 * GROUP_W))),
            pl.BlockSpec((bb, n, wag_w), lambda i, k: (i, k, EV_WAG // wag_w)),
            pl.BlockSpec((bb, SUBLANES, 3 * GROUP_W), lambda i, k: (i, 0, EV_RKV // (3 * GROUP_W))),
            pl.BlockSpec((bb, SUBLANES, wag_w), lambda i, k: (i, 0, EV_WAG // wag_w)),
            pl.BlockSpec((bb, npair, pw, pw), lambda i, k: (i, 0, 0, 0)),
            pl.BlockSpec((1, 3 * GROUP_W), lambda i, k: (0, 0)),
            pl.BlockSpec((1, wag_w), lambda i, k: (0, 0)),
            pl.BlockSpec((wag_w, 3 * GROUP_W), lambda i, k: (0, 0)),
            pl.BlockSpec((SUBLANES, GROUP_W), lambda i, k: (0, 0)),
        ],
        out_specs=[
            pl.BlockSpec((bb, n, GROUP_W), lambda i, k: (i, k, 0)),
            pl.BlockSpec((bb, npair, pw, pw), lambda i, k: (i, 0, 0, 0)),
        ],
        out_shape=[jax.ShapeDtypeStruct((b, t, GROUP_W), F32),
                   jax.ShapeDtypeStruct((b, npair, pw, pw), F32)],
        scratch_shapes=[pltpu.VMEM((bb, n + SUBLANES, 3 * GROUP_W), F32),
                        pltpu.VMEM((bb, n + SUBLANES, wag_w), F32),
                        pltpu.VMEM((bb, npair, pw, pw), F32)],
        compiler_params=_cparams("parallel", "arbitrary"),
        name="rwkv7",
    )(proj, proj, carry, carry, s0, mur, muw, wlr, prm)


def _rwkv_state_to_pairs(s):
    b = s.shape[0]
    s = s.reshape(b, H_B // 2, 2, HD_B, HD_B)
    zero = jnp.zeros_like(s[:, :, 0])
    top = jnp.concatenate([s[:, :, 0], zero], axis=-1)
    bot = jnp.concatenate([zero, s[:, :, 1]], axis=-1)
    return jnp.concatenate([top, bot], axis=-2)


def _rwkv_state_from_pairs(sp):
    b = sp.shape[0]
    heads = jnp.stack([sp[:, :, :HD_B, :HD_B], sp[:, :, HD_B:, HD_B:]], axis=2)
    return heads.reshape(b, H_B, HD_B, HD_B)


def _stack_pair(x):
    first = (_iota2((1, x.shape[1]), 1) & 64) == 0
    return jnp.concatenate([jnp.where(first, x, 0.0), jnp.where(first, 0.0, x)], axis=0)


def _group_rows(a, rows, n, gi):
    return jnp.concatenate([a[gi * n:(gi + 1) * n], a[rows + gi * n:rows + (gi + 1) * n]], axis=0)


def _ungroup_rows(pieces, n):
    return jnp.concatenate([p[:n] for p in pieces] + [p[n:] for p in pieces], axis=0)


def _gla_kernel(qk_ref, v_ref, z_ref, gkin_ref, s0_ref, wgk_ref, bgk_ref, gain_ref,
                o_ref, s_ref, s_scr, *, groups):
    t = pl.program_id(1)
    bb, n, _ = qk_ref.shape
    rows = groups * n
    nprob = bb // groups
    npair = H_C // 2
    pw = 2 * DK_C
    qkw = H_C * DK_C
    r2 = 2 * rows
    sub = min(GLA_SUB, n)
    nslab = n // sub
    assert groups == 1 or nslab == 1

    @pl.when(t == 0)
    def _():
        s_scr[...] = s0_ref[...]

    qk = qk_ref[...].reshape(bb * n, 2 * qkw)
    q = qk[:, :qkw] * (DK_C ** -0.5)
    k = qk[:, qkw:]
    v = v_ref[...].reshape(bb * n, GROUP_W)
    z = z_ref[...].reshape(bb * n, GROUP_W)
    gk = _log_sigmoid(_mm(gkin_ref[...].reshape(bb * n, LANES), wgk_ref[...]) + bgk_ref[...]) * (1.0 / GLA_NORM)
    incl1, _ = _group_masks(rows, n)
    incl2, _ = _group_masks(r2, n)
    row_t = _iota2((rows, 1), 0) & (n - 1)
    eye_p = _eye(pw)
    cums = [_mm01(incl1, gk[p * rows:(p + 1) * rows], 3) for p in range(nprob)]
    units = [(p, u) for p in range(nprob) for u in range(npair)]

    pre = []
    for p, u in units:
        rs = slice(p * rows, (p + 1) * rows)
        sl = slice(u * pw, (u + 1) * pw)
        bcum, qp, kp = cums[p][:, sl], q[rs, sl], k[rs, sl]
        qparts, kparts = [], []
        for s in range(nslab):
            rho = bcum[s * sub - 1:s * sub, :] if s > 0 else jnp.zeros((1, pw), F32)
            in_slab = jnp.logical_and(row_t >= s * sub, row_t < (s + 1) * sub)
            qparts.append(qp * jnp.exp(jnp.where(in_slab, bcum - rho, -jnp.inf)))
            kparts.append(kp * jnp.exp(jnp.where(row_t < (s + 1) * sub, rho - bcum, -jnp.inf)))
        blast = jnp.concatenate(
            [jnp.broadcast_to(bcum[gi * n + n - 1:gi * n + n, :], (n, pw)) for gi in range(groups)], axis=0)
        vst = jnp.concatenate([v[rs, (2 * u + j) * DV_C:(2 * u + j + 1) * DV_C] for j in range(2)], axis=0)
        pre.append(dict(qcat=_stack_pair(jnp.concatenate(qparts, axis=1)),
                        kcat=_stack_pair(jnp.concatenate(kparts, axis=1)),
                        qdb=_stack_pair(qp * jnp.exp(bcum)), kd=_stack_pair(kp * jnp.exp(blast - bcum)),
                        vst=vst, bcum=bcum))
    att = [jnp.where(incl2, _mm_nt(u["qcat"], u["kcat"]), 0.0) for u in pre]
    intra = [_mm(a, u["vst"]) for a, u in zip(att, pre)]
    inter = []
    for (p, uidx), u in zip(units, pre):
        pieces = [_mm(_group_rows(u["qdb"], rows, n, gi), s_scr[p * groups + gi, uidx])
                  for gi in range(groups)]
        inter.append(_ungroup_rows(pieces, n))
    pick_last = (_iota2((BF16_ROWS, rows), 1)
                 == (_iota2((BF16_ROWS, rows), 0) & (groups - 1)) * n + n - 1).astype(F32)
    glast = [jnp.exp(_mm01(pick_last, u["bcum"], 3)) for u in pre]
    glcols = [_mm_nt01(eye_p, g, 3) for g in glast]
    for (p, uidx), u, o_in, o_x, glc in zip(units, pre, intra, inter, glcols):
        for gi in range(groups):
            upd = _mm_tn(_group_rows(u["kd"], rows, n, gi), _group_rows(u["vst"], rows, n, gi))
            s_scr[p * groups + gi, uidx] = s_scr[p * groups + gi, uidx] * glc[:, gi:gi + 1] + upd
        o = o_in + o_x
        for j in range(2):
            h = 2 * uidx + j
            zh = z[p * rows:(p + 1) * rows, h * DV_C:(h + 1) * DV_C]
            val = _rms(o[j * rows:(j + 1) * rows]) * gain_ref[...] * _silu(zh)
            o_ref[p * groups:(p + 1) * groups, :, h * DV_C:(h + 1) * DV_C] = val.reshape(groups, n, DV_C)

    @pl.when(t == pl.num_programs(1) - 1)
    def _():
        s_ref[...] = s_scr[...]


def _gla_call(proj, s0, layer, wgk, bgk, gain, bb, n, groups):
    b, t, _ = proj.shape
    npair, pw = H_C // 2, 2 * DK_C
    return pl.pallas_call(
        functools.partial(_gla_kernel, groups=groups),
        grid=(b // bb, t // n),
        in_specs=[
            pl.BlockSpec((bb, n, GROUP_W), lambda i, k: (i, k, OD_CQK // GROUP_W)),
            pl.BlockSpec((bb, n, GROUP_W), lambda i, k: (i, k, OD_CV // GROUP_W)),
            pl.BlockSpec((bb, n, GROUP_W), lambda i, k: (i, k, OD_CZ // GROUP_W)),
            pl.BlockSpec((bb, n, LANES), lambda i, k: (i, k, OD_CGK // LANES)),
            pl.BlockSpec((None, bb, npair, DV_C, pw), lambda i, k: (layer, i, 0, 0, 0)),
            pl.BlockSpec((LANES, H_C * DK_C), lambda i, k: (0, 0)),
            pl.BlockSpec((1, H_C * DK_C), lambda i, k: (0, 0)),
            pl.BlockSpec((1, DV_C), lambda i, k: (0, 0)),
        ],
        out_specs=[
            pl.BlockSpec((bb, n, GROUP_W), lambda i, k: (i, k, 0)),
            pl.BlockSpec((bb, npair, DV_C, pw), lambda i, k: (i, 0, 0, 0)),
        ],
        out_shape=[jax.ShapeDtypeStruct((b, t, GROUP_W), F32),
                   jax.ShapeDtypeStruct((b, npair, DV_C, pw), F32)],
        scratch_shapes=[pltpu.VMEM((bb, npair, DV_C, pw), F32)],
        compiler_params=_cparams("parallel", "arbitrary"),
        name="gla",
    )(proj, proj, proj, proj, s0, wgk, bgk, gain)


def _mlstm_kernel(qk_ref, v_ref, og_ref, if_ref, c0_ref, n0_ref, m0_ref, bif_ref, gain_ref,
                  h_ref, c_ref, nn_ref, m_ref, c_scr, n_scr, m_scr, *, groups):
    t = pl.program_id(1)
    bb, n, _ = qk_ref.shape
    rows = groups * n
    nprob = bb // groups
    npair = H_D // 2
    pw = 2 * DK_D
    qkw = H_D * DK_D
    r2 = 2 * rows

    @pl.when(t == 0)
    def _():
        c_scr[...] = c0_ref[...]
        n_scr[...] = n0_ref[...]
        m_scr[...] = m0_ref[...]

    x = if_ref[...].reshape(bb * n, LANES) + bif_ref[...]
    lf = _log_sigmoid(x)
    qk = qk_ref[...].reshape(bb * n, 2 * qkw)
    q = qk[:, :qkw]
    k = qk[:, qkw:] * (DK_D ** -0.5)
    v = v_ref[...].reshape(bb * n, GROUP_W)
    og = og_ref[...].reshape(bb * n, GROUP_W)
    incl1, _ = _group_masks(rows, n)
    incl2, _ = _group_masks(r2, n)
    r_i, c_i = _iota2((r2, r2), 0), _iota2((r2, r2), 1)
    same2 = (r_i & -n) == (c_i & -n)
    last2 = jnp.logical_and(same2, (c_i & (n - 1)) == n - 1)
    lane = _iota2((1, LANES), 1)
    ones = jnp.ones((r2, LANES), F32)
    ones_sq = jnp.ones((LANES, LANES), F32)
    fcums = [_mm01(incl1, lf[p * rows:(p + 1) * rows], 3) for p in range(nprob)]
    m_rows = [m_scr[b_] for b_ in range(bb)]
    m_old = list(m_rows)
    units = [(p, u) for p in range(nprob) for u in range(npair)]

    def per_block(fn):
        return jnp.concatenate([jnp.broadcast_to(fn(j, gi), (n, fn(j, gi).shape[1]))
                                for j in range(2) for gi in range(groups)], axis=0)

    pre = []
    for p, u in units:
        rs = slice(p * rows, (p + 1) * rows)
        sl = slice(u * pw, (u + 1) * pw)
        fsel = jnp.concatenate([jnp.where(lane == H_D + 2 * u + j, fcums[p], 0.0) for j in range(2)], axis=0)
        isel = jnp.concatenate([jnp.where(lane == 2 * u + j, x[rs], 0.0) for j in range(2)], axis=0)
        cols = _mm_x01(jnp.concatenate([fsel, isel], axis=0), ones_sq, 3)
        fcol, icol = cols[:r2], cols[r2:]
        drow = _mm_nt01(ones, isel - fsel, 3)
        flast = _mm01(last2, fcol, 3)
        mprev = per_block(lambda j, gi: m_old[p * groups + gi][:, 2 * u + j:2 * u + j + 1])
        nmat = per_block(lambda j, gi: n_scr[p * groups + gi][:, sl])
        log_d = jnp.where(incl2, fcol + drow, -jnp.inf)
        m_in = jnp.max(log_d, axis=-1, keepdims=True)
        m_e = jnp.max(jnp.where(same2, flast + drow, -jnp.inf), axis=-1, keepdims=True)
        m_t = jnp.maximum(fcol + mprev, m_in)
        w_in = jnp.exp(fcol + mprev - m_t)
        m_new = jnp.maximum(flast + mprev, m_e)
        qst, kst = _stack_pair(q[rs, sl]), _stack_pair(k[rs, sl])
        vst = jnp.concatenate([v[rs, (2 * u + j) * DV_D:(2 * u + j + 1) * DV_D] for j in range(2)], axis=0)
        pre.append(dict(qst=qst, kst=kst, vst=vst, log_d=log_d, m_t=m_t, w_in=w_in, m_new=m_new,
                        cd=jnp.exp(flast + mprev - m_new), nmat=nmat,
                        ke=kst * jnp.exp(flast - fcol + icol - m_new)))
    dms = [jnp.exp(u["log_d"] - u["m_t"]) * _mm_nt(u["qst"], u["kst"]) for u in pre]
    intra = [_mm(dm, u["vst"]) for dm, u in zip(dms, pre)]
    inter = []
    for (p, uidx), u in zip(units, pre):
        qw = u["w_in"] * u["qst"]
        pieces = [_mm(_group_rows(qw, rows, n, gi), c_scr[p * groups + gi, uidx]) for gi in range(groups)]
        inter.append(_ungroup_rows(pieces, n))
    for (p, uidx), u, dm, o_in, o_x in zip(units, pre, dms, intra, inter):
        sl = slice(uidx * pw, (uidx + 1) * pw)
        den = (u["w_in"] * _mm_x01(u["qst"] * u["nmat"], ones_sq, 1)
               + _mm_x01(dm, ones_sq, 2))
        hout = (o_in + o_x) / jnp.maximum(jnp.abs(den), jnp.exp(-u["m_t"]))
        for gi in range(groups):
            b_ = p * groups + gi
            r0, r1 = gi * n, rows + gi * n
            cdrow = jnp.where(lane < DK_D, u["cd"][r0:r0 + 1], u["cd"][r1:r1 + 1])
            ke_g = _group_rows(u["ke"], rows, n, gi)
            cdcol = jnp.where(_iota2((pw, 1), 0) < DK_D, u["cd"][r0:r0 + 1], u["cd"][r1:r1 + 1])
            c_scr[b_, uidx] = c_scr[b_, uidx] * cdcol + _mm_tn(ke_g, _group_rows(u["vst"], rows, n, gi))
            n_scr[b_, :, sl] = n_scr[b_][:, sl] * cdrow + jnp.sum(ke_g, axis=0, keepdims=True)
            for j, r in ((0, r0), (1, r1)):
                m_rows[b_] = jnp.where(lane == 2 * uidx + j, u["m_new"][r:r + 1], m_rows[b_])
        for j in range(2):
            h = 2 * uidx + j
            ogh = og[p * rows:(p + 1) * rows, h * DV_D:(h + 1) * DV_D]
            val = _sigmoid(ogh) * (_rms(hout[j * rows:(j + 1) * rows]) * gain_ref[...])
            h_ref[p * groups:(p + 1) * groups, :, h * DV_D:(h + 1) * DV_D] = val.reshape(groups, n, DV_D)
    for b_ in range(bb):
        m_scr[b_] = m_rows[b_]

    @pl.when(t == pl.num_programs(1) - 1)
    def _():
        c_ref[...] = c_scr[...]
        nn_ref[...] = n_scr[...]
        m_ref[...] = m_scr[...]


def _mlstm_call(proj, c0, n0, layer, m0, bif, gain, bb, n, groups):
    b, t, _ = proj.shape
    qkw = H_D * DK_D
    npair, pw = H_D // 2, 2 * DK_D
    return pl.pallas_call(
        functools.partial(_mlstm_kernel, groups=groups),
        grid=(b // bb, t // n),
        in_specs=[
            pl.BlockSpec((bb, n, GROUP_W), lambda i, k: (i, k, OD_DQK // GROUP_W)),
            pl.BlockSpec((bb, n, GROUP_W), lambda i, k: (i, k, OD_DV // GROUP_W)),
            pl.BlockSpec((bb, n, GROUP_W), lambda i, k: (i, k, OD_DO // GROUP_W)),
            pl.BlockSpec((bb, n, LANES), lambda i, k: (i, k, OD_DIF // LANES)),
            pl.BlockSpec((None, bb, npair, DV_D, pw), lambda i, k: (layer, i, 0, 0, 0)),
            pl.BlockSpec((None, bb, 1, qkw), lambda i, k: (layer, i, 0, 0)),
            pl.BlockSpec((bb, 1, LANES), lambda i, k: (i, 0, 0)),
            pl.BlockSpec((1, LANES), lambda i, k: (0, 0)),
            pl.BlockSpec((1, DV_D), lambda i, k: (0, 0)),
        ],
        out_specs=[
            pl.BlockSpec((bb, n, GROUP_W), lambda i, k: (i, k, 0)),
            pl.BlockSpec((bb, npair, DV_D, pw), lambda i, k: (i, 0, 0, 0)),
            pl.BlockSpec((bb, 1, qkw), lambda i, k: (i, 0, 0)),
            pl.BlockSpec((bb, 1, LANES), lambda i, k: (i, 0, 0)),
        ],
        out_shape=[jax.ShapeDtypeStruct((b, t, GROUP_W), F32),
                   jax.ShapeDtypeStruct((b, npair, DV_D, pw), F32),
                   jax.ShapeDtypeStruct((b, 1, qkw), F32),
                   jax.ShapeDtypeStruct((b, 1, LANES), F32)],
        scratch_shapes=[pltpu.VMEM((bb, npair, DV_D, pw), F32),
                        pltpu.VMEM((bb, 1, qkw), F32),
                        pltpu.VMEM((bb, 1, LANES), F32)],
        compiler_params=_cparams("parallel", "arbitrary"),
        name="mlstm",
    )(proj, proj, proj, proj, c0, n0, m0, bif, gain)


def _pad_cols(w, width):
    return jnp.pad(w, ((0, 0), (0, width - w.shape[1])))


def _pad_rows(w, rows, at=0):
    return jnp.pad(w, ((at, rows - at - w.shape[0]), (0, 0)))


def _even_in_weight(w):
    pa, pb = w[:, :2056], w[:, 2056:]
    cols = [
        pb[:, 0:1536],
        pa[:, 0:1536],
        pa[:, 1544:2056],
        pb[:, 1536:1792],
        _pad_cols(pa[:, 1536:1544], LANES),
    ]
    return _pad_cols(jnp.concatenate(cols, axis=1), EV_COLS).astype(BF16)


def _odd_in_weight(w):
    pc, pd = w[:, :1552], w[:, 1552:]
    cols = [
        pc[:, 0:512],
        pd[:, 0:512],
        pc[:, 512:1024],
        pc[:, 1040:1552],
        pd[:, 512:1024],
        pd[:, 1032:1544],
        _pad_cols(pc[:, 1024:1040], LANES),
        _pad_cols(pd[:, 1024:1032], LANES),
    ]
    return jnp.concatenate(cols, axis=1).astype(BF16)


def _row(v):
    return v.reshape(1, -1).astype(F32)


def _tiles(x, rows=1024):
    b, t, _ = x.shape
    tb = min(t, rows)
    bb = min(b, rows // tb)
    return bb, tb


def _trunk(x, mod, states, wts):
    xbuf, s_delta, s_rwkv, s_gla, s_mc, s_mn, s_mm = states
    b, t, _ = x.shape
    bb, tb = _tiles(x)
    fbb, ftb = _tiles(x, FFN_ROWS)
    pbb, ptb = _tiles(x, PROJ_ROWS)
    n = min(CHUNK, t)
    groups = CHUNK // n
    rbb = (PROBLEMS_LONG if t > CHUNK else PROBLEMS) * groups
    new_even = ([], [], [])
    new_odd = ([], [], [], [])
    for l in range(DEPTH):
        lw = wts["layers"][l]
        m_l = mod[l]
        x = _ffn_call(x, m_l, lw["gain0"], wts["wg"], wts["wu"], wts["wd"], (l, 0), wts["final_gain"], 0, fbb, ftb,
                      FFN_COLS, False)
        i = l // 2
        if l % 2 == 0:
            h_tail, proj = _adaln_proj_call(x, m_l, lw["gain1"], lw["w_in"], pbb, ptb, True)
            if xbuf is None:
                carry = jnp.zeros((b, SUBLANES, EV_COLS), F32)
            else:
                rows = _rows_proj_call(xbuf[i].reshape(b * (CONV_W - 1), D_MODEL), lw["w_in"], 1024)
                carry = jnp.pad(rows.reshape(b, CONV_W - 1, EV_COLS),
                                ((0, 0), (SUBLANES - CONV_W + 1, 0), (0, 0)))
            oa, sd = _delta_call(proj, carry, s_delta, i, lw["conv_w"], lw["delta_hp"], lw["gain_a"],
                                 rbb, n, groups)
            ob, sr = _rwkv_call(proj, carry, _rwkv_state_to_pairs(s_rwkv[i]), lw["mu_rkv"], lw["mu_wag"],
                                lw["w_lora"], lw["rwkv_prm"], rbb, n, groups)
            for lst, val in zip(new_even, (h_tail[:, SUBLANES - (CONV_W - 1):], sd, _rwkv_state_from_pairs(sr))):
                lst.append(val)
        else:
            (proj,) = _adaln_proj_call(x, m_l, lw["gain1"], lw["w_in"], pbb, ptb, False)
            pair_shape = (N_ODD, b, H_C // 2, 2 * DK_C, DV_C)
            oa, sg = _gla_call(proj, s_gla.reshape(pair_shape), i, lw["w_gk2"], lw["b_gk"], lw["gain_c"],
                               rbb, n, groups)
            ob, sc, sn, sm = _mlstm_call(proj, s_mc.reshape(pair_shape), s_mn.reshape(N_ODD, b, 1, H_D * DK_D), i,
                                         _pad_cols(s_mm[i], LANES).reshape(b, 1, LANES),
                                         lw["b_if"], lw["gain_d"], rbb, n, groups)
            for lst, val in zip(new_odd, (sg.reshape(b, H_C, DK_C, DV_C), sc.reshape(b, H_D, DK_D, DV_D),
                                          sn.reshape(b, H_D, DK_D), sm[:, 0, :H_D])):
                lst.append(val)
        x = _ffn_call(x, m_l, lw["gain2"], wts["wg"], wts["wu"], wts["wd"], (l, 1), wts["final_gain"], 6, fbb, ftb,
                      FFN_COLS, l == DEPTH - 1, mixer=(oa, ob, lw["w_out"]))
    stacked = [jnp.stack(lst) for lst in new_even + new_odd]
    return (x, *stacked)


def _prepare_weights(norm_gain, final_gain, w_ffn_gate, w_ffn_up, w_ffn_down, w_in_even, w_out_even,
                     conv_w, a_log, dt_bias, gain_a, mu_b, w0_b, w_w2, a0_b, w_a2, w_g2, k_k, k_a, r_k,
                     lnx_gain, lnx_bias, w_in_odd, w_out_odd, w_gk2, b_gk, gain_c, b_i, b_f, gain_d):
    layers = []
    for l in range(DEPTH):
        i = l // 2
        lw = {
            "gain0": _row(norm_gain[l, 0]), "gain1": _row(norm_gain[l, 1]), "gain2": _row(norm_gain[l, 2]),
        }
        if l % 2 == 0:
            lw["w_in"] = _even_in_weight(w_in_even[i])
            lw["w_out"] = w_out_even[i].astype(BF16)
            lw["conv_w"] = _pad_rows(conv_w[i].astype(F32), SUBLANES)
            lw["delta_hp"] = _pad_rows(jnp.stack([_pad_cols(_row(a_log[i]), LANES)[0],
                                                  _pad_cols(_row(dt_bias[i]), LANES)[0]]), SUBLANES)
            lw["gain_a"] = _row(gain_a[i])
            lw["mu_rkv"] = _row(mu_b[i, :1536])
            lw["mu_wag"] = _row(mu_b[i, 1536:])
            lora = jnp.zeros((2 * LANES, 3 * GROUP_W), F32)
            lora = lora.at[0:64, 0:GROUP_W].set(w_w2[i])
            lora = lora.at[64:128, GROUP_W:2 * GROUP_W].set(w_a2[i])
            lora = lora.at[128:256, 2 * GROUP_W:].set(w_g2[i])
            lw["w_lora"] = lora.astype(BF16)
            lw["rwkv_prm"] = jnp.stack([w0_b[i], a0_b[i], k_k[i], k_a[i], r_k[i].reshape(-1),
                                        lnx_gain[i], lnx_bias[i], jnp.zeros_like(w0_b[i])]).astype(F32)
        else:
            lw["w_in"] = _odd_in_weight(w_in_odd[i])
            lw["w_out"] = w_out_odd[i].astype(BF16)
            lw["w_gk2"] = _pad_rows(w_gk2[i], LANES).astype(BF16)
            lw["b_gk"] = _row(b_gk[i])
            lw["gain_c"] = _row(gain_c[i])
            lw["b_if"] = _pad_cols(_row(jnp.concatenate([b_i[i], b_f[i]])), LANES)
            lw["gain_d"] = _row(gain_d[i])
        layers.append(lw)
    return {"layers": layers, "final_gain": _row(final_gain), "wg": w_ffn_gate.astype(BF16),
            "wu": w_ffn_up.astype(BF16), "wd": w_ffn_down.astype(BF16)}


def kernel(x_prompt, x_sample, c_prompt, c_sample, state_xbuf_even, state_delta, state_rwkv, state_gla,
           state_mlstm_c, state_mlstm_n, state_mlstm_m, w_mod, b_mod, norm_gain, final_gain, w_ffn_gate,
           w_ffn_up, w_ffn_down, w_in_even, w_out_even, conv_w, a_log, dt_bias, gain_a, mu_b, w0_b, w_w2,
           a0_b, w_a2, w_g2, k_k, k_a, r_k, lnx_gain, lnx_bias, w_in_odd, w_out_odd, w_gk2, b_gk, gain_c,
           b_i, b_f, gain_d):
    wts = _prepare_weights(norm_gain, final_gain, w_ffn_gate, w_ffn_up, w_ffn_down, w_in_even, w_out_even,
                           conv_w, a_log, dt_bias, gain_a, mu_b, w0_b, w_w2, a0_b, w_a2, w_g2, k_k, k_a,
                           r_k, lnx_gain, lnx_bias, w_in_odd, w_out_odd, w_gk2, b_gk, gain_c, b_i, b_f,
                           gain_d)
    bp, bs = x_prompt.shape[0], x_sample.shape[0]
    c_all = jnp.concatenate([c_prompt, c_sample], axis=0).astype(F32)
    mod_p, mod_s = _mod_call(c_all, bp, w_mod, b_mod)

    def zeros(shape):
        return jnp.zeros(shape, F32)

    zero_states = (
        None,
        zeros((N_EVEN, bp, H_A, DK_A, DK_A)),
        zeros((N_EVEN, bp, H_B, HD_B, HD_B)),
        zeros((N_ODD, bp, H_C, DK_C, DV_C)),
        zeros((N_ODD, bp, H_D, DK_D, DV_D)),
        zeros((N_ODD, bp, H_D, DK_D)),
        zeros((N_ODD, bp, H_D)),
    )
    y_p, xb_p, dl_p, rw_p, gl_p, mc_p, mn_p, mm_p = _trunk(x_prompt, mod_p, zero_states, wts)
    sample_states = (state_xbuf_even, state_delta, state_rwkv, state_gla,
                     state_mlstm_c, state_mlstm_n, state_mlstm_m)
    y_s, xb_s, dl_s, rw_s, gl_s, mc_s, mn_s, mm_s = _trunk(x_sample, mod_s, sample_states, wts)
    return (y_p, y_s, xb_p, xb_s, dl_p, dl_s, rw_p, rw_s, gl_p, gl_s, mc_p, mc_s, mn_p, mn_s, mm_p, mm_s)
```

```python
import functools

import jax
import jax.numpy as jnp
from jax import lax
from jax.experimental import pallas as pl
from jax.experimental.pallas import tpu as pltpu

F32 = jnp.float32
BF16 = jnp.bfloat16

D_MODEL = 1024
DEPTH = 4
N_EVEN = 2
N_ODD = 2
D_FF = 2816
N_MOD = 9
EPS = 1e-6
GN_EPS = 64e-5
CONV_W = 4
H_A, DK_A = 4, 128
H_B, HD_B = 8, 64
H_C, DK_C, DV_C = 4, 64, 128
H_D, DK_D, DV_D = 4, 64, 128
GLA_NORM = 16.0
GROUP_W = 512

LANES = 128
SUBLANES = 8
BF16_ROWS = 16
VMEM_LIMIT_BYTES = 48 * 1024 * 1024

EV_RKV, EV_QKV, EV_Z, EV_WAG, EV_AB, EV_COLS = 0, 1536, 3072, 3584, 3840, 4096
OD_CQK, OD_DQK, OD_CV, OD_CZ, OD_DV, OD_DO, OD_CGK, OD_DIF, OD_COLS = (
    0, 512, 1024, 1536, 2048, 2560, 3072, 3200, 3328)

CHUNK = 64
GLA_SUB = 16
PROJ_ROWS = 512
PROJ_COLS = 256
FFN_ROWS = 1024
FFN_COLS = 256
PROBLEMS = 2
PROBLEMS_LONG = 8


def _bf(x):
    if x.dtype == BF16:
        return x
    if x.shape[-2] % BF16_ROWS == 0 and x.shape[-1] % BF16_ROWS == 0:
        return x.astype(BF16)
    return x


def _pair(a, b):
    a, b = _bf(a), _bf(b)
    if a.dtype != b.dtype:
        a, b = a.astype(F32), b.astype(F32)
    return a, b


def _mm(a, b):
    a, b = _pair(a, b)
    return jnp.dot(a, b, preferred_element_type=F32)


def _mm_nt(a, b):
    a, b = _pair(a, b)
    return lax.dot_general(a, b, (((1,), (1,)), ((), ())), preferred_element_type=F32)


def _mm_tn(a, b):
    a, b = _pair(a, b)
    return lax.dot_general(a, b, (((0,), (0,)), ((), ())), preferred_element_type=F32)


def _split_bf16(x, parts):
    out, r = [], x
    for i in range(parts):
        p = r.astype(BF16)
        out.append(p)
        if i + 1 < parts:
            r = r - p.astype(F32)
    return out


def _mm01(a01, x, parts):
    a = a01.astype(BF16)
    acc = None
    for p in _split_bf16(x, parts):
        d = jnp.dot(a, p, preferred_element_type=F32)
        acc = d if acc is None else acc + d
    return acc


def _mm_x01(x, b01, parts):
    b = b01.astype(BF16)
    acc = None
    for p in _split_bf16(x, parts):
        d = jnp.dot(p, b, preferred_element_type=F32)
        acc = d if acc is None else acc + d
    return acc


def _mm_nt01(a01, x, parts):
    a = a01.astype(BF16)
    acc = None
    for p in _split_bf16(x, parts):
        d = lax.dot_general(a, p, (((1,), (1,)), ((), ())), preferred_element_type=F32)
        acc = d if acc is None else acc + d
    return acc


def _sigmoid(x):
    return jax.nn.sigmoid(x)


def _silu(x):
    return x * jax.nn.sigmoid(x)


def _softplus(x):
    return jnp.maximum(x, 0.0) + jnp.log1p(jnp.exp(-jnp.abs(x)))


def _log_sigmoid(x):
    return -_softplus(-x)


def _rms(x, eps=EPS):
    return x * lax.rsqrt(jnp.mean(x * x, axis=-1, keepdims=True) + eps)


def _l2n(x):
    return x * lax.rsqrt(jnp.sum(x * x, axis=-1, keepdims=True) + 1e-6)


def _iota2(shape, dim):
    return lax.broadcasted_iota(jnp.int32, shape, dim)


def _group_masks(size, n):
    r, c = _iota2((size, size), 0), _iota2((size, size), 1)
    same = (r & -n) == (c & -n)
    return jnp.logical_and(same, r >= c), jnp.logical_and(same, r > c)


def _eye(n):
    return (_iota2((n, n), 0) == _iota2((n, n), 1)).astype(F32)


def _unit_lower_inv_many(mats, n):
    eye = _eye(mats[0].shape[0])
    ms = [-a for a in mats]
    ps = [eye + m for m in ms]
    covered = 2
    while covered < n:
        ms = [_mm(m, m) for m in ms]
        ps = [p + _mm(p, m) for p, m in zip(ps, ms)]
        covered *= 2
    return ps


def _cparams(*sem):
    return pltpu.CompilerParams(dimension_semantics=sem, vmem_limit_bytes=VMEM_LIMIT_BYTES)


def _mod_kernel(c_ref, w_ref, b_ref, op_ref, os_ref):
    cs = _silu(c_ref[...])
    m = _mm(cs, w_ref[...]) + b_ref[...]
    nprompt = op_ref.shape[0]
    op_ref[...] = m[:nprompt]
    os_ref[...] = m[nprompt:]


def _mod_call(c_all, nprompt, w_mod, b_mod):
    rows = c_all.shape[0]
    tn = 1024
    width = N_MOD * D_MODEL
    return pl.pallas_call(
        _mod_kernel,
        grid=(DEPTH, width // tn),
        in_specs=[
            pl.BlockSpec((rows, D_MODEL), lambda l, j: (0, 0)),
            pl.BlockSpec((None, D_MODEL, tn), lambda l, j: (l, 0, j)),
            pl.BlockSpec((None, 1, tn), lambda l, j: (l, 0, j)),
        ],
        out_specs=[pl.BlockSpec((None, nprompt, tn), lambda l, j: (l, 0, j)),
                   pl.BlockSpec((None, rows - nprompt, tn), lambda l, j: (l, 0, j))],
        out_shape=[jax.ShapeDtypeStruct((DEPTH, nprompt, width), F32),
                   jax.ShapeDtypeStruct((DEPTH, rows - nprompt, width), F32)],
        compiler_params=_cparams("parallel", "parallel"),
        name="mod",
    )(c_all, w_mod, b_mod.reshape(DEPTH, 1, width))


def _mod_rows(mod_ref, j, batch_block, bb, tb):
    cols = slice(j * D_MODEL, (j + 1) * D_MODEL)
    if bb == 1:
        return mod_ref[pl.ds(batch_block, 1), cols]
    rep = (_iota2((bb * tb, bb), 0) & -tb) == _iota2((bb * tb, bb), 1) * tb
    return _mm01(rep.astype(F32), mod_ref[:, cols], 3).reshape(bb, tb, D_MODEL)


def _mod_table_spec(mod, bb, ngrid):
    nb, width = mod.shape
    if bb == 1:
        return pl.BlockSpec((nb, width), (lambda i, k, f: (0, 0)) if ngrid == 3 else (lambda i, k: (0, 0)))
    return pl.BlockSpec((bb, width), (lambda i, k, f: (i, 0)) if ngrid == 3 else (lambda i, k: (i, 0)))


def _adaln(x, gain, scale, shift):
    return _rms(x) * gain * (1.0 + scale) + shift


def _ffn_kernel(x_ref, mod_ref, gain_ref, wg_ref, wu_ref, wd_ref, fgain_ref, *refs,
                j0, final_norm, mixer_in):
    if mixer_in:
        oa_ref, ob_ref, wo_ref, o_ref, h_scr, acc_scr, xm_scr = refs
    else:
        o_ref, h_scr, acc_scr = refs
    ib = pl.program_id(0)
    f = pl.program_id(2)
    bb, tb, d = x_ref.shape

    @pl.when(f == 0)
    def _():
        x = x_ref[...]
        if mixer_in:
            o = jnp.concatenate([oa_ref[...], ob_ref[...]], axis=-1).reshape(bb * tb, d)
            y = jnp.dot(o.astype(BF16), wo_ref[...], preferred_element_type=F32).reshape(bb, tb, d)
            x = x + (1.0 + _mod_rows(mod_ref, 5, ib, bb, tb)) * y
            xm_scr[...] = x
        h = _adaln(x, gain_ref[...], _mod_rows(mod_ref, j0 + 1, ib, bb, tb), _mod_rows(mod_ref, j0, ib, bb, tb))
        h_scr[...] = h.reshape(bb * tb, d).astype(BF16)
        acc_scr[...] = jnp.zeros_like(acc_scr)

    h = h_scr[...]
    g = jnp.dot(h, wg_ref[...], preferred_element_type=F32)
    u = jnp.dot(h, wu_ref[...], preferred_element_type=F32)
    a = (_silu(g) * u).astype(BF16)
    acc_scr[...] += jnp.dot(a, wd_ref[...], preferred_element_type=F32)

    @pl.when(f == pl.num_programs(2) - 1)
    def _():
        y = acc_scr[...].reshape(bb, tb, d)
        base = xm_scr[...] if mixer_in else x_ref[...]
        out = base + 0.5 * (1.0 + _mod_rows(mod_ref, j0 + 2, ib, bb, tb)) * y
        if final_norm:
            out = _rms(out) * fgain_ref[...]
        o_ref[...] = out


def _ffn_call(x, mod, gain, wg, wu, wd, widx, fgain, j0, bb, tb, tf, final_norm, mixer=None):
    wl, wk = widx
    b, t, d = x.shape
    xspec = pl.BlockSpec((bb, tb, d), lambda i, k, f: (i, k, 0))
    rowspec = pl.BlockSpec((1, d), lambda i, k, f: (0, 0))
    in_specs = [
        xspec,
        _mod_table_spec(mod, bb, 3),
        rowspec,
        pl.BlockSpec((None, None, d, tf), lambda i, k, f: (wl, wk, 0, f)),
        pl.BlockSpec((None, None, d, tf), lambda i, k, f: (wl, wk, 0, f)),
        pl.BlockSpec((None, None, tf, d), lambda i, k, f: (wl, wk, f, 0)),
        rowspec,
    ]
    args = [x, mod, gain, wg, wu, wd, fgain]
    scratch = [pltpu.VMEM((bb * tb, d), BF16), pltpu.VMEM((bb * tb, d), F32)]
    if mixer is not None:
        oa, ob, wo = mixer
        hspec = pl.BlockSpec((bb, tb, GROUP_W), lambda i, k, f: (i, k, 0))
        in_specs += [hspec, hspec, pl.BlockSpec((d, d), lambda i, k, f: (0, 0))]
        args += [oa, ob, wo]
        scratch.append(pltpu.VMEM((bb, tb, d), F32))
    return pl.pallas_call(
        functools.partial(_ffn_kernel, j0=j0, final_norm=final_norm, mixer_in=mixer is not None),
        grid=(b // bb, t // tb, D_FF // tf),
        in_specs=in_specs,
        out_specs=xspec,
        out_shape=jax.ShapeDtypeStruct(x.shape, x.dtype),
        scratch_shapes=scratch,
        compiler_params=_cparams("parallel", "parallel", "arbitrary"),
        name="ffn",
    )(*args)


def _adaln_proj_kernel(x_ref, mod_ref, gain_ref, w_ref, *refs, tail):
    p_ref = refs[-1]
    ib = pl.program_id(0)
    bb, tb, d = x_ref.shape
    h = _adaln(x_ref[...], gain_ref[...], _mod_rows(mod_ref, 4, ib, bb, tb), _mod_rows(mod_ref, 3, ib, bb, tb))
    if tail:
        refs[0][...] = h[:, tb - SUBLANES:, :]
    hb = h.reshape(bb * tb, d).astype(BF16)
    for c0 in range(0, w_ref.shape[1], PROJ_COLS):
        p = jnp.dot(hb, w_ref[:, c0:c0 + PROJ_COLS], preferred_element_type=F32)
        p_ref[:, :, c0:c0 + PROJ_COLS] = p.reshape(bb, tb, PROJ_COLS)


def _adaln_proj_call(x, mod, gain, w, bb, tb, tail):
    b, t, d = x.shape
    n = w.shape[1]
    xspec = pl.BlockSpec((bb, tb, d), lambda i, k: (i, k, 0))
    out_specs = [pl.BlockSpec((bb, tb, n), lambda i, k: (i, k, 0))]
    out_shape = [jax.ShapeDtypeStruct((b, t, n), F32)]
    if tail:
        out_specs.insert(0, pl.BlockSpec((bb, SUBLANES, d), lambda i, k: (i, 0, 0)))
        out_shape.insert(0, jax.ShapeDtypeStruct((b, SUBLANES, d), F32))
    return pl.pallas_call(
        functools.partial(_adaln_proj_kernel, tail=tail),
        grid=(b // bb, t // tb),
        in_specs=[
            xspec, _mod_table_spec(mod, bb, 2),
            pl.BlockSpec((1, d), lambda i, k: (0, 0)),
            pl.BlockSpec((d, n), lambda i, k: (0, 0)),
        ],
        out_specs=out_specs,
        out_shape=out_shape,
        compiler_params=_cparams("parallel", "arbitrary"),
        name="adaln_proj",
    )(x, mod, gain, w)


def _rows_proj_kernel(a_ref, w_ref, o_ref):
    o_ref[...] = _mm(a_ref[...], w_ref[...])


def _rows_proj_call(a, w, tn):
    m, k = a.shape
    n = w.shape[1]
    return pl.pallas_call(
        _rows_proj_kernel,
        grid=(n // tn,),
        in_specs=[pl.BlockSpec((m, k), lambda j: (0, 0)), pl.BlockSpec((k, tn), lambda j: (0, j))],
        out_specs=pl.BlockSpec((m, tn), lambda j: (0, j)),
        out_shape=jax.ShapeDtypeStruct((m, n), F32),
        compiler_params=_cparams("parallel"),
        name="rows_proj",
    )(a, w)


def _stack_heads(x, nheads, head_w):
    lane = _iota2((1, x.shape[1]), 1)
    return jnp.concatenate(
        [jnp.where(jnp.logical_and(lane >= h * head_w, lane < (h + 1) * head_w), x, 0.0)
         for h in range(nheads)], axis=0)


def _delta_kernel(qkv_ref, z_ref, ab_ref, carry_ref, s0_ref, cw_ref, hp_ref, gain_ref,
                  o_ref, s_ref, ext_scr, s_scr, *, groups):
    t = pl.program_id(1)
    bb, n, _ = qkv_ref.shape
    rows = groups * n
    nprob = bb // groups
    hr = H_A * rows

    @pl.when(t == 0)
    def _():
        ext_scr[:, 0:SUBLANES, :] = carry_ref[...]
        s_scr[...] = s0_ref[...]

    @pl.when(t > 0)
    def _():
        ext_scr[:, 0:SUBLANES, :] = ext_scr[:, n:n + SUBLANES, :]

    ext_scr[:, SUBLANES:SUBLANES + n, :] = qkv_ref[...]
    conv = cw_ref[0:1, :] * ext_scr[:, 5:5 + n, :]
    for j in range(1, CONV_W):
        conv = conv + cw_ref[j:j + 1, :] * ext_scr[:, 5 + j:5 + j + n, :]
    x = _silu(conv).reshape(bb * n, 3 * GROUP_W)
    ab = ab_ref[...].reshape(bb * n, LANES)
    g = -jnp.exp(hp_ref[0:1, :]) * _softplus(ab + hp_ref[1:2, :])
    beta = _sigmoid(ab)
    z = z_ref[...].reshape(bb * n, GROUP_W)

    incl1, _ = _group_masks(rows, n)
    incl, strict = _group_masks(hr, n)
    lane = _iota2((1, LANES), 1)
    ones = jnp.ones((hr, LANES), F32)
    probs = range(nprob)

    def head_rows(a, h):
        return a[:, h * DK_A:(h + 1) * DK_A]

    gcol, bcol, qs, ks, vs, zs = [], [], [], [], [], []
    kst, lhs = [], []
    grow = []
    for p in probs:
        sl = slice(p * rows, (p + 1) * rows)
        gc = _mm01(incl1, g[sl], 3)
        gsel = jnp.concatenate([jnp.where(lane == h, gc, 0.0) for h in range(H_A)], axis=0)
        bsel = jnp.concatenate([jnp.where(lane == H_A + h, beta[sl], 0.0) for h in range(H_A)], axis=0)
        gcol.append(jnp.sum(gsel, axis=-1, keepdims=True))
        bcol.append(jnp.sum(bsel, axis=-1, keepdims=True))
        grow.append(_mm_nt01(ones, gsel, 3))
        xp = x[sl]
        qn = jnp.concatenate([_l2n(head_rows(xp[:, 0:GROUP_W], h)) * (DK_A ** -0.5)
                              for h in range(H_A)], axis=1)
        kn = jnp.concatenate([_l2n(head_rows(xp[:, GROUP_W:2 * GROUP_W], h)) for h in range(H_A)], axis=1)
        qs.append(jnp.concatenate([head_rows(qn, h) for h in range(H_A)], axis=0))
        ks.append(jnp.concatenate([head_rows(kn, h) for h in range(H_A)], axis=0))
        vs.append(jnp.concatenate([head_rows(xp[:, 2 * GROUP_W:], h) for h in range(H_A)], axis=0))
        zs.append(z[sl])
        k_st = _stack_heads(kn, H_A, DK_A)
        kst.append(k_st)
        lhs.append(jnp.concatenate([k_st * bcol[p], _stack_heads(qn, H_A, DK_A)], axis=0))

    kq = [_mm_nt(lhs[p], kst[p]) for p in probs]
    dec = [jnp.exp(jnp.where(incl, gcol[p] - grow[p], -jnp.inf)) for p in probs]
    a_low = [jnp.where(strict, kq[p][:hr] * dec[p], 0.0) for p in probs]
    tinv = _unit_lower_inv_many(a_low, n)
    eg = [jnp.exp(gcol[p]) for p in probs]
    kb = [ks[p] * bcol[p] for p in probs]
    sol = [_mm(tinv[p], jnp.concatenate([vs[p] * bcol[p], kb[p] * eg[p]], axis=1)) for p in probs]
    qg = [qs[p] * eg[p] for p in probs]

    us, oparts = [], []
    for p in probs:
        u_rows, o_rows = [], []
        for h in range(H_A):
            for gi in range(groups):
                r0 = h * rows + gi * n
                s = s_scr[p * groups + gi, h]
                ksq = _mm(jnp.concatenate([sol[p][r0:r0 + n, DK_A:], qg[p][r0:r0 + n]], axis=0), s)
                u_rows.append(sol[p][r0:r0 + n, :DK_A] - ksq[:n])
                o_rows.append(ksq[n:])
        us.append(jnp.concatenate(u_rows, axis=0))
        oparts.append(jnp.concatenate(o_rows, axis=0))
    outs = [oparts[p] + _mm(kq[p][hr:] * dec[p], us[p]) for p in probs]
    for p in probs:
        for h in range(H_A):
            for gi in range(groups):
                r0 = h * rows + gi * n
                glast = gcol[p][r0 + n - 1:r0 + n]
                kd = ks[p][r0:r0 + n] * jnp.exp(glast - gcol[p][r0:r0 + n])
                s = s_scr[p * groups + gi, h]
                s_scr[p * groups + gi, h] = s * jnp.exp(glast) + _mm_tn(kd, us[p][r0:r0 + n])
            o = outs[p][h * rows:(h + 1) * rows]
            val = _rms(o) * gain_ref[...] * _silu(head_rows(zs[p], h))
            o_ref[p * groups:(p + 1) * groups, :, h * DK_A:(h + 1) * DK_A] = val.reshape(groups, n, DK_A)

    @pl.when(t == pl.num_programs(1) - 1)
    def _():
        s_ref[...] = s_scr[...]


def _delta_call(proj, carry, s0, layer, cw, hp, gain, bb, n, groups):
    b, t, _ = proj.shape
    return pl.pallas_call(
        functools.partial(_delta_kernel, groups=groups),
        grid=(b // bb, t // n),
        in_specs=[
            pl.BlockSpec((bb, n, 3 * GROUP_W), lambda i, k: (i, k, EV_QKV // (3 * GROUP_W))),
            pl.BlockSpec((bb, n, GROUP_W), lambda i, k: (i, k, EV_Z // GROUP_W)),
            pl.BlockSpec((bb, n, LANES), lambda i, k: (i, k, EV_AB // LANES)),
            pl.BlockSpec((bb, SUBLANES, 3 * GROUP_W), lambda i, k: (i, 0, EV_QKV // (3 * GROUP_W))),
            pl.BlockSpec((None, bb, H_A, DK_A, DK_A), lambda i, k: (layer, i, 0, 0, 0)),
            pl.BlockSpec((SUBLANES, 3 * GROUP_W), lambda i, k: (0, 0)),
            pl.BlockSpec((SUBLANES, LANES), lambda i, k: (0, 0)),
            pl.BlockSpec((1, DK_A), lambda i, k: (0, 0)),
        ],
        out_specs=[
            pl.BlockSpec((bb, n, GROUP_W), lambda i, k: (i, k, 0)),
            pl.BlockSpec((bb, H_A, DK_A, DK_A), lambda i, k: (i, 0, 0, 0)),
        ],
        out_shape=[jax.ShapeDtypeStruct((b, t, GROUP_W), F32),
                   jax.ShapeDtypeStruct((b, H_A, DK_A, DK_A), F32)],
        scratch_shapes=[pltpu.VMEM((bb, n + SUBLANES, 3 * GROUP_W), F32),
                        pltpu.VMEM((bb, H_A, DK_A, DK_A), F32)],
        compiler_params=_cparams("parallel", "arbitrary"),
        name="delta",
    )(proj, proj, proj, carry, s0, cw, hp, gain)


def _rwkv_kernel(rkv_ref, wag_ref, crkv_ref, cwag_ref, s0_ref, mur_ref, muw_ref, wlr_ref, prm_ref,
                 y_ref, s_ref, ext1_scr, ext2_scr, sp_scr, *, groups):
    t = pl.program_id(1)
    bb, n, _ = rkv_ref.shape
    rows = groups * n
    nprob = bb // groups
    npair = H_B // 2
    pw = 2 * HD_B
    r2 = 2 * rows

    @pl.when(t == 0)
    def _():
        ext1_scr[:, 0:SUBLANES, :] = crkv_ref[...]
        ext2_scr[:, 0:SUBLANES, :] = cwag_ref[...]
        sp_scr[...] = s0_ref[...]

    @pl.when(t > 0)
    def _():
        ext1_scr[:, 0:SUBLANES, :] = ext1_scr[:, n:n + SUBLANES, :]
        ext2_scr[:, 0:SUBLANES, :] = ext2_scr[:, n:n + SUBLANES, :]

    cur1 = rkv_ref[...]
    cur2 = wag_ref[...]
    ext1_scr[:, SUBLANES:SUBLANES + n, :] = cur1
    ext2_scr[:, SUBLANES:SUBLANES + n, :] = cur2
    xm1 = cur1 + (ext1_scr[:, SUBLANES - 1:SUBLANES - 1 + n, :] - cur1) * mur_ref[...]
    xm2 = cur2 + (ext2_scr[:, SUBLANES - 1:SUBLANES - 1 + n, :] - cur2) * muw_ref[...]
    xm1 = xm1.reshape(bb * n, 3 * GROUP_W)
    xm2 = xm2.reshape(bb * n, 2 * LANES)
    r = xm1[:, 0:GROUP_W]
    kb = xm1[:, GROUP_W:2 * GROUP_W]
    vb = xm1[:, 2 * GROUP_W:3 * GROUP_W]
    lane2 = _iota2(xm2.shape, 1)
    feat = jnp.where(lane2 < 64, jnp.tanh(xm2), jnp.where(lane2 < 128, xm2, _sigmoid(xm2)))
    lr = _mm(feat, wlr_ref[...])
    w_raw = prm_ref[0:1, :] + lr[:, 0:GROUP_W]
    logw = -jnp.exp(-_softplus(-w_raw) - 0.5)
    a = _sigmoid(prm_ref[1:2, :] + lr[:, GROUP_W:2 * GROUP_W])
    gb = lr[:, 2 * GROUP_W:3 * GROUP_W]
    kkraw = kb * prm_ref[2:3, :]
    k = kb * (1.0 + (a - 1.0) * prm_ref[3:4, :])

    incl1, _ = _group_masks(rows, n)
    incl, strict = _group_masks(r2, n)
    blk = (_iota2((pw, pw), 0) & -HD_B) == (_iota2((pw, pw), 1) & -HD_B)
    blkf = blk.astype(F32)
    cums = [_mm01(incl1, logw[p * rows:(p + 1) * rows], 3) for p in range(nprob)]

    units = [(p, q) for p in range(nprob) for q in range(npair)]

    def st(x):
        return _stack_heads(x, 2, HD_B)

    pre = []
    for p, q in units:
        rs = slice(p * rows, (p + 1) * rows)
        sl = slice(q * pw, (q + 1) * pw)
        kkr = kkraw[rs, sl]
        rp, kp, vp, ap = r[rs, sl], k[rs, sl], vb[rs, sl], a[rs, sl]
        sums = _mm_x01(jnp.concatenate([kkr * kkr, rp * kp * prm_ref[4:5, sl]], axis=0), blkf, 1)
        kk = kkr * lax.rsqrt(sums[:rows] + 1e-6)
        cump = cums[p][:, sl]
        ginv = jnp.exp(-cump)
        rt = rp * jnp.exp(cump)
        at = -kk * jnp.exp(cump - logw[rs, sl])
        bt = kk * ap * ginv
        kt = kp * ginv
        pre.append(dict(rt=rt, at=at, bt=bt, kt=kt, vp=vp, cump=cump, bonus=sums[rows:] * vp,
                        vst=st(vp), gb=gb[rs, sl]))
    mats = [_mm_nt(jnp.concatenate([st(u["at"]), st(u["rt"])], axis=0),
                   jnp.concatenate([st(u["bt"]), st(u["kt"])], axis=0)) for u in pre]
    tinv = _unit_lower_inv_many([-jnp.where(strict, m[:r2, :r2], 0.0) for m in mats], n)

    u0s, y0s = [], []
    for (p, q), u in zip(units, pre):
        u0_rows, y0_rows = [], []
        for gi in range(groups):
            gs = slice(gi * n, (gi + 1) * n)
            uy = _mm_nt(jnp.concatenate([u["at"][gs], u["rt"][gs]], axis=0), sp_scr[p * groups + gi, q])
            u0_rows.append(uy[:n])
            y0_rows.append(uy[n:])
        u0s.append(jnp.concatenate(u0_rows, axis=0))
        y0s.append(jnp.concatenate(y0_rows, axis=0))
    x1 = [_mm(jnp.where(strict, m[:r2, r2:], 0.0), u["vst"]) for m, u in zip(mats, pre)]
    ust = [_mm(ti, st(u0) + x) for ti, u0, x in zip(tinv, u0s, x1)]
    yst = [_mm(jnp.concatenate([jnp.where(incl, m[r2:, :r2], 0.0), jnp.where(incl, m[r2:, r2:], 0.0)], axis=1),
               jnp.concatenate([us_, u["vst"]], axis=0)) for m, us_, u in zip(mats, ust, pre)]
    ys = [y0 + ys_[:rows] + ys_[rows:] for y0, ys_ in zip(y0s, yst)]
    means = [_mm_x01(y, blkf, 1) * (1.0 / HD_B) for y in ys]
    ycs = [y - m for y, m in zip(ys, means)]
    variances = [_mm_x01(yc * yc, blkf, 1) * (1.0 / HD_B) for yc in ycs]

    for (p, q), u, us_, yc, var in zip(units, pre, ust, ycs, variances):
        sl = slice(q * pw, (q + 1) * pw)
        uu = us_[:rows] + us_[rows:]
        for gi in range(groups):
            gs = slice(gi * n, (gi + 1) * n)
            upd = _mm_tn(jnp.concatenate([uu[gs], u["vp"][gs]], axis=0),
                         jnp.concatenate([u["bt"][gs], u["kt"][gs]], axis=0))
            glast = jnp.exp(u["cump"][gi * n + n - 1:gi * n + n, :])
            sp = sp_scr[p * groups + gi, q]
            sp_scr[p * groups + gi, q] = jnp.where(blk, sp + upd, 0.0) * glast
        yn = yc * lax.rsqrt(var + GN_EPS) * prm_ref[5:6, sl] + prm_ref[6:7, sl]
        val = (yn + u["bonus"]) * u["gb"]
        y_ref[p * groups:(p + 1) * groups, :, sl] = val.reshape(groups, n, pw)

    @pl.when(t == pl.num_programs(1) - 1)
    def _():
        s_ref[...] = sp_scr[...]


def _rwkv_call(proj, carry, s0, mur, muw, wlr, prm, bb, n, groups):
    b, t, _ = proj.shape
    wag_w = 2 * LANES
    npair, pw = H_B // 2, 2 * HD_B
    return pl.pallas_call(
        functools.partial(_rwkv_kernel, groups=groups),
        grid=(b // bb, t // n),
        in_specs=[
            pl.BlockSpec((bb, n, 3 * GROUP_W), lambda i, k: (i, k, EV_RKV // (3 * GROUP_W))),
            pl.BlockSpec((bb, n, wag_w), lambda i, k: (i, k, EV_WAG // wag_w)),
            pl.BlockSpec((bb, SUBLANES, 3 * GROUP_W), lambda i, k: (i, 0, EV_RKV // (3 * GROUP_W))),
            pl.BlockSpec((bb, SUBLANES, wag_w), lambda i, k: (i, 0, EV_WAG // wag_w)),
            pl.BlockSpec((bb, npair, pw, pw), lambda i, k: (i, 0, 0, 0)),
            pl.BlockSpec((1, 3 * GROUP_W), lambda i, k: (0, 0)),
            pl.BlockSpec((1, wag_w), lambda i, k: (0, 0)),
            pl.BlockSpec((wag_w, 3 * GROUP_W), lambda i, k: (0, 0)),
            pl.BlockSpec((SUBLANES, GROUP_W), lambda i, k: (0, 0)),
        ],
        out_specs=[
            pl.BlockSpec((bb, n, GROUP_W), lambda i, k: (i, k, 0)),
            pl.BlockSpec((bb, npair, pw, pw), lambda i, k: (i, 0, 0, 0)),
        ],
        out_shape=[jax.ShapeDtypeStruct((b, t, GROUP_W), F32),
                   jax.ShapeDtypeStruct((b, npair, pw, pw), F32)],
        scratch_shapes=[pltpu.VMEM((bb, n + SUBLANES, 3 * GROUP_W), F32),
                        pltpu.VMEM((bb, n + SUBLANES, wag_w), F32),
                        pltpu.VMEM((bb, npair, pw, pw), F32)],
        compiler_params=_cparams("parallel", "arbitrary"),
        name="rwkv7",
    )(proj, proj, carry, carry, s0, mur, muw, wlr, prm)


def _rwkv_state_to_pairs(s):
    b = s.shape[0]
    s = s.reshape(b, H_B // 2, 2, HD_B, HD_B)
    zero = jnp.zeros_like(s[:, :, 0])
    top = jnp.concatenate([s[:, :, 0], zero], axis=-1)
    bot = jnp.concatenate([zero, s[:, :, 1]], axis=-1)
    return jnp.concatenate([top, bot], axis=-2)


def _rwkv_state_from_pairs(sp):
    b = sp.shape[0]
    heads = jnp.stack([sp[:, :, :HD_B, :HD_B], sp[:, :, HD_B:, HD_B:]], axis=2)
    return heads.reshape(b, H_B, HD_B, HD_B)


def _stack_pair(x):
    first = (_iota2((1, x.shape[1]), 1) & 64) == 0
    return jnp.concatenate([jnp.where(first, x, 0.0), jnp.where(first, 0.0, x)], axis=0)


def _group_rows(a, rows, n, gi):
    return jnp.concatenate([a[gi * n:(gi + 1) * n], a[rows + gi * n:rows + (gi + 1) * n]], axis=0)


def _ungroup_rows(pieces, n):
    return jnp.concatenate([p[:n] for p in pieces] + [p[n:] for p in pieces], axis=0)


def _gla_kernel(qk_ref, v_ref, z_ref, gkin_ref, s0_ref, wgk_ref, bgk_ref, gain_ref,
                o_ref, s_ref, s_scr, *, groups):
    t = pl.program_id(1)
    bb, n, _ = qk_ref.shape
    rows = groups * n
    nprob = bb // groups
    npair = H_C // 2
    pw = 2 * DK_C
    qkw = H_C * DK_C
    r2 = 2 * rows
    sub = min(GLA_SUB, n)
    nslab = n // sub
    assert groups == 1 or nslab == 1

    @pl.when(t == 0)
    def _():
        s_scr[...] = s0_ref[...]

    qk = qk_ref[...].reshape(bb * n, 2 * qkw)
    q = qk[:, :qkw] * (DK_C ** -0.5)
    k = qk[:, qkw:]
    v = v_ref[...].reshape(bb * n, GROUP_W)
    z = z_ref[...].reshape(bb * n, GROUP_W)
    gk = _log_sigmoid(_mm(gkin_ref[...].reshape(bb * n, LANES), wgk_ref[...]) + bgk_ref[...]) * (1.0 / GLA_NORM)
    incl1, _ = _group_masks(rows, n)
    incl2, _ = _group_masks(r2, n)
    row_t = _iota2((rows, 1), 0) & (n - 1)
    eye_p = _eye(pw)
    cums = [_mm01(incl1, gk[p * rows:(p + 1) * rows], 3) for p in range(nprob)]
    units = [(p, u) for p in range(nprob) for u in range(npair)]

    pre = []
    for p, u in units:
        rs = slice(p * rows, (p + 1) * rows)
        sl = slice(u * pw, (u + 1) * pw)
        bcum, qp, kp = cums[p][:, sl], q[rs, sl], k[rs, sl]
        qparts, kparts = [], []
        for s in range(nslab):
            rho = bcum[s * sub - 1:s * sub, :] if s > 0 else jnp.zeros((1, pw), F32)
            in_slab = jnp.logical_and(row_t >= s * sub, row_t < (s + 1) * sub)
            qparts.append(qp * jnp.exp(jnp.where(in_slab, bcum - rho, -jnp.inf)))
            kparts.append(kp * jnp.exp(jnp.where(row_t < (s + 1) * sub, rho - bcum, -jnp.inf)))
        blast = jnp.concatenate(
            [jnp.broadcast_to(bcum[gi * n + n - 1:gi * n + n, :], (n, pw)) for gi in range(groups)], axis=0)
        vst = jnp.concatenate([v[rs, (2 * u + j) * DV_C:(2 * u + j + 1) * DV_C] for j in range(2)], axis=0)
        pre.append(dict(qcat=_stack_pair(jnp.concatenate(qparts, axis=1)),
                        kcat=_stack_pair(jnp.concatenate(kparts, axis=1)),
                        qdb=_stack_pair(qp * jnp.exp(bcum)), kd=_stack_pair(kp * jnp.exp(blast - bcum)),
                        vst=vst, bcum=bcum))
    att = [jnp.where(incl2, _mm_nt(u["qcat"], u["kcat"]), 0.0) for u in pre]
    intra = [_mm(a, u["vst"]) for a, u in zip(att, pre)]
    inter = []
    for (p, uidx), u in zip(units, pre):
        pieces = [_mm(_group_rows(u["qdb"], rows, n, gi), s_scr[p * groups + gi, uidx])
                  for gi in range(groups)]
        inter.append(_ungroup_rows(pieces, n))
    pick_last = (_iota2((BF16_ROWS, rows), 1)
                 == (_iota2((BF16_ROWS, rows), 0) & (groups - 1)) * n + n - 1).astype(F32)
    glast = [jnp.exp(_mm01(pick_last, u["bcum"], 3)) for u in pre]
    glcols = [_mm_nt01(eye_p, g, 3) for g in glast]
    for (p, uidx), u, o_in, o_x, glc in zip(units, pre, intra, inter, glcols):
        for gi in range(groups):
            upd = _mm_tn(_group_rows(u["kd"], rows, n, gi), _group_rows(u["vst"], rows, n, gi))
            s_scr[p * groups + gi, uidx] = s_scr[p * groups + gi, uidx] * glc[:, gi:gi + 1] + upd
        o = o_in + o_x
        for j in range(2):
            h = 2 * uidx + j
            zh = z[p * rows:(p + 1) * rows, h * DV_C:(h + 1) * DV_C]
            val = _rms(o[j * rows:(j + 1) * rows]) * gain_ref[...] * _silu(zh)
            o_ref[p * groups:(p + 1) * groups, :, h * DV_C:(h + 1) * DV_C] = val.reshape(groups, n, DV_C)

    @pl.when(t == pl.num_programs(1) - 1)
    def _():
        s_ref[...] = s_scr[...]


def _gla_call(proj, s0, layer, wgk, bgk, gain, bb, n, groups):
    b, t, _ = proj.shape
    npair, pw = H_C // 2, 2 * DK_C
    return pl.pallas_call(
        functools.partial(_gla_kernel, groups=groups),
        grid=(b // bb, t // n),
        in_specs=[
            pl.BlockSpec((bb, n, GROUP_W), lambda i, k: (i, k, OD_CQK // GROUP_W)),
            pl.BlockSpec((bb, n, GROUP_W), lambda i, k: (i, k, OD_CV // GROUP_W)),
            pl.BlockSpec((bb, n, GROUP_W), lambda i, k: (i, k, OD_CZ // GROUP_W)),
            pl.BlockSpec((bb, n, LANES), lambda i, k: (i, k, OD_CGK // LANES)),
            pl.BlockSpec((None, bb, npair, DV_C, pw), lambda i, k: (layer, i, 0, 0, 0)),
            pl.BlockSpec((LANES, H_C * DK_C), lambda i, k: (0, 0)),
            pl.BlockSpec((1, H_C * DK_C), lambda i, k: (0, 0)),
            pl.BlockSpec((1, DV_C), lambda i, k: (0, 0)),
        ],
        out_specs=[
            pl.BlockSpec((bb, n, GROUP_W), lambda i, k: (i, k, 0)),
            pl.BlockSpec((bb, npair, DV_C, pw), lambda i, k: (i, 0, 0, 0)),
        ],
        out_shape=[jax.ShapeDtypeStruct((b, t, GROUP_W), F32),
                   jax.ShapeDtypeStruct((b, npair, DV_C, pw), F32)],
        scratch_shapes=[pltpu.VMEM((bb, npair, DV_C, pw), F32)],
        compiler_params=_cparams("parallel", "arbitrary"),
        name="gla",
    )(proj, proj, proj, proj, s0, wgk, bgk, gain)


def _mlstm_kernel(qk_ref, v_ref, og_ref, if_ref, c0_ref, n0_ref, m0_ref, bif_ref, gain_ref,
                  h_ref, c_ref, nn_ref, m_ref, c_scr, n_scr, m_scr, *, groups):
    t = pl.program_id(1)
    bb, n, _ = qk_ref.shape
    rows = groups * n
    nprob = bb // groups
    npair = H_D // 2
    pw = 2 * DK_D
    qkw = H_D * DK_D
    r2 = 2 * rows

    @pl.when(t == 0)
    def _():
        c_scr[...] = c0_ref[...]
        n_scr[...] = n0_ref[...]
        m_scr[...] = m0_ref[...]

    x = if_ref[...].reshape(bb * n, LANES) + bif_ref[...]
    lf = _log_sigmoid(x)
    qk = qk_ref[...].reshape(bb * n, 2 * qkw)
    q = qk[:, :qkw]
    k = qk[:, qkw:] * (DK_D ** -0.5)
    v = v_ref[...].reshape(bb * n, GROUP_W)
    og = og_ref[...].reshape(bb * n, GROUP_W)
    incl1, _ = _group_masks(rows, n)
    incl2, _ = _group_masks(r2, n)
    r_i, c_i = _iota2((r2, r2), 0), _iota2((r2, r2), 1)
    same2 = (r_i & -n) == (c_i & -n)
    last2 = jnp.logical_and(same2, (c_i & (n - 1)) == n - 1)
    lane = _iota2((1, LANES), 1)
    ones = jnp.ones((r2, LANES), F32)
    ones_sq = jnp.ones((LANES, LANES), F32)
    fcums = [_mm01(incl1, lf[p * rows:(p + 1) * rows], 3) for p in range(nprob)]
    m_rows = [m_scr[b_] for b_ in range(bb)]
    m_old = list(m_rows)
    units = [(p, u) for p in range(nprob) for u in range(npair)]

    def per_block(fn):
        return jnp.concatenate([jnp.broadcast_to(fn(j, gi), (n, fn(j, gi).shape[1]))
                                for j in range(2) for gi in range(groups)], axis=0)

    pre = []
    for p, u in units:
        rs = slice(p * rows, (p + 1) * rows)
        sl = slice(u * pw, (u + 1) * pw)
        fsel = jnp.concatenate([jnp.where(lane == H_D + 2 * u + j, fcums[p], 0.0) for j in range(2)], axis=0)
        isel = jnp.concatenate([jnp.where(lane == 2 * u + j, x[rs], 0.0) for j in range(2)], axis=0)
        cols = _mm_x01(jnp.concatenate([fsel, isel], axis=0), ones_sq, 3)
        fcol, icol = cols[:r2], cols[r2:]
        drow = _mm_nt01(ones, isel - fsel, 3)
        flast = _mm01(last2, fcol, 3)
        mprev = per_block(lambda j, gi: m_old[p * groups + gi][:, 2 * u + j:2 * u + j + 1])
        nmat = per_block(lambda j, gi: n_scr[p * groups + gi][:, sl])
        log_d = jnp.where(incl2, fcol + drow, -jnp.inf)
        m_in = jnp.max(log_d, axis=-1, keepdims=True)
        m_e = jnp.max(jnp.where(same2, flast + drow, -jnp.inf), axis=-1, keepdims=True)
        m_t = jnp.maximum(fcol + mprev, m_in)
        w_in = jnp.exp(fcol + mprev - m_t)
        m_new = jnp.maximum(flast + mprev, m_e)
        qst, kst = _stack_pair(q[rs, sl]), _stack_pair(k[rs, sl])
        vst = jnp.concatenate([v[rs, (2 * u + j) * DV_D:(2 * u + j + 1) * DV_D] for j in range(2)], axis=0)
        pre.append(dict(qst=qst, kst=kst, vst=vst, log_d=log_d, m_t=m_t, w_in=w_in, m_new=m_new,
                        cd=jnp.exp(flast + mprev - m_new), nmat=nmat,
                        ke=kst * jnp.exp(flast - fcol + icol - m_new)))
    dms = [jnp.exp(u["log_d"] - u["m_t"]) * _mm_nt(u["qst"], u["kst"]) for u in pre]
    intra = [_mm(dm, u["vst"]) for dm, u in zip(dms, pre)]
    inter = []
    for (p, uidx), u in zip(units, pre):
        qw = u["w_in"] * u["qst"]
        pieces = [_mm(_group_rows(qw, rows, n, gi), c_scr[p * groups + gi, uidx]) for gi in range(groups)]
        inter.append(_ungroup_rows(pieces, n))
    for (p, uidx), u, dm, o_in, o_x in zip(units, pre, dms, intra, inter):
        sl = slice(uidx * pw, (uidx + 1) * pw)
        den = (u["w_in"] * _mm_x01(u["qst"] * u["nmat"], ones_sq, 1)
               + _mm_x01(dm, ones_sq, 2))
        hout = (o_in + o_x) / jnp.maximum(jnp.abs(den), jnp.exp(-u["m_t"]))
        for gi in range(groups):
            b_ = p * groups + gi
            r0, r1 = gi * n, rows + gi * n
            cdrow = jnp.where(lane < DK_D, u["cd"][r0:r0 + 1], u["cd"][r1:r1 + 1])
            ke_g = _group_rows(u["ke"], rows, n, gi)
            cdcol = jnp.where(_iota2((pw, 1), 0) < DK_D, u["cd"][r0:r0 + 1], u["cd"][r1:r1 + 1])
            c_scr[b_, uidx] = c_scr[b_, uidx] * cdcol + _mm_tn(ke_g, _group_rows(u["vst"], rows, n, gi))
            n_scr[b_, :, sl] = n_scr[b_][:, sl] * cdrow + jnp.sum(ke_g, axis=0, keepdims=True)
            for j, r in ((0, r0), (1, r1)):
                m_rows[b_] = jnp.where(lane == 2 * uidx + j, u["m_new"][r:r + 1], m_rows[b_])
        for j in range(2):
            h = 2 * uidx + j
            ogh = og[p * rows:(p + 1) * rows, h * DV_D:(h + 1) * DV_D]
            val = _sigmoid(ogh) * (_rms(hout[j * rows:(j + 1) * rows]) * gain_ref[...])
            h_ref[p * groups:(p + 1) * groups, :, h * DV_D:(h + 1) * DV_D] = val.reshape(groups, n, DV_D)
    for b_ in range(bb):
        m_scr[b_] = m_rows[b_]

    @pl.when(t == pl.num_programs(1) - 1)
    def _():
        c_ref[...] = c_scr[...]
        nn_ref[...] = n_scr[...]
        m_ref[...] = m_scr[...]


def _mlstm_call(proj, c0, n0, layer, m0, bif, gain, bb, n, groups):
    b, t, _ = proj.shape
    qkw = H_D * DK_D
    npair, pw = H_D // 2, 2 * DK_D
    return pl.pallas_call(
        functools.partial(_mlstm_kernel, groups=groups),
        grid=(b // bb, t // n),
        in_specs=[
            pl.BlockSpec((bb, n, GROUP_W), lambda i, k: (i, k, OD_DQK // GROUP_W)),
            pl.BlockSpec((bb, n, GROUP_W), lambda i, k: (i, k, OD_DV // GROUP_W)),
            pl.BlockSpec((bb, n, GROUP_W), lambda i, k: (i, k, OD_DO // GROUP_W)),
            pl.BlockSpec((bb, n, LANES), lambda i, k: (i, k, OD_DIF // LANES)),
            pl.BlockSpec((None, bb, npair, DV_D, pw), lambda i, k: (layer, i, 0, 0, 0)),
            pl.BlockSpec((None, bb, 1, qkw), lambda i, k: (layer, i, 0, 0)),
            pl.BlockSpec((bb, 1, LANES), lambda i, k: (i, 0, 0)),
            pl.BlockSpec((1, LANES), lambda i, k: (0, 0)),
            pl.BlockSpec((1, DV_D), lambda i, k: (0, 0)),
        ],
        out_specs=[
            pl.BlockSpec((bb, n, GROUP_W), lambda i, k: (i, k, 0)),
            pl.BlockSpec((bb, npair, DV_D, pw), lambda i, k: (i, 0, 0, 0)),
            pl.BlockSpec((bb, 1, qkw), lambda i, k: (i, 0, 0)),
            pl.BlockSpec((bb, 1, LANES), lambda i, k: (i, 0, 0)),
        ],
        out_shape=[jax.ShapeDtypeStruct((b, t, GROUP_W), F32),
                   jax.ShapeDtypeStruct((b, npair, DV_D, pw), F32),
                   jax.ShapeDtypeStruct((b, 1, qkw), F32),
                   jax.ShapeDtypeStruct((b, 1, LANES), F32)],
        scratch_shapes=[pltpu.VMEM((bb, npair, DV_D, pw), F32),
                        pltpu.VMEM((bb, 1, qkw), F32),
                        pltpu.VMEM((bb, 1, LANES), F32)],
        compiler_params=_cparams("parallel", "arbitrary"),
        name="mlstm",
    )(proj, proj, proj, proj, c0, n0, m0, bif, gain)


def _pad_cols(w, width):
    return jnp.pad(w, ((0, 0), (0, width - w.shape[1])))


def _pad_rows(w, rows, at=0):
    return jnp.pad(w, ((at, rows - at - w.shape[0]), (0, 0)))


def _even_in_weight(w):
    pa, pb = w[:, :2056], w[:, 2056:]
    cols = [
        pb[:, 0:1536],
        pa[:, 0:1536],
        pa[:, 1544:2056],
        pb[:, 1536:1792],
        _pad_cols(pa[:, 1536:1544], LANES),
    ]
    return _pad_cols(jnp.concatenate(cols, axis=1), EV_COLS).astype(BF16)


def _odd_in_weight(w):
    pc, pd = w[:, :1552], w[:, 1552:]
    cols = [
        pc[:, 0:512],
        pd[:, 0:512],
        pc[:, 512:1024],
        pc[:, 1040:1552],
        pd[:, 512:1024],
        pd[:, 1032:1544],
        _pad_cols(pc[:, 1024:1040], LANES),
        _pad_cols(pd[:, 1024:1032], LANES),
    ]
    return jnp.concatenate(cols, axis=1).astype(BF16)


def _row(v):
    return v.reshape(1, -1).astype(F32)


def _tiles(x, rows=1024):
    b, t, _ = x.shape
    tb = min(t, rows)
    bb = min(b, rows // tb)
    return bb, tb


def _trunk(x, mod, states, wts):
    xbuf, s_delta, s_rwkv, s_gla, s_mc, s_mn, s_mm = states
    b, t, _ = x.shape
    bb, tb = _tiles(x)
    fbb, ftb = _tiles(x, FFN_ROWS)
    pbb, ptb = _tiles(x, PROJ_ROWS)
    n = min(CHUNK, t)
    groups = CHUNK // n
    rbb = (PROBLEMS_LONG if t > CHUNK else PROBLEMS) * groups
    new_even = ([], [], [])
    new_odd = ([], [], [], [])
    for l in range(DEPTH):
        lw = wts["layers"][l]
        m_l = mod[l]
        x = _ffn_call(x, m_l, lw["gain0"], wts["wg"], wts["wu"], wts["wd"], (l, 0), wts["final_gain"], 0, fbb, ftb,
                      FFN_COLS, False)
        i = l // 2
        if l % 2 == 0:
            h_tail, proj = _adaln_proj_call(x, m_l, lw["gain1"], lw["w_in"], pbb, ptb, True)
            if xbuf is None:
                carry = jnp.zeros((b, SUBLANES, EV_COLS), F32)
            else:
                rows = _rows_proj_call(xbuf[i].reshape(b * (CONV_W - 1), D_MODEL), lw["w_in"], 1024)
                carry = jnp.pad(rows.reshape(b, CONV_W - 1, EV_COLS),
                                ((0, 0), (SUBLANES - CONV_W + 1, 0), (0, 0)))
            oa, sd = _delta_call(proj, carry, s_delta, i, lw["conv_w"], lw["delta_hp"], lw["gain_a"],
                                 rbb, n, groups)
            ob, sr = _rwkv_call(proj, carry, _rwkv_state_to_pairs(s_rwkv[i]), lw["mu_rkv"], lw["mu_wag"],
                                lw["w_lora"], lw["rwkv_prm"], rbb, n, groups)
            for lst, val in zip(new_even, (h_tail[:, SUBLANES - (CONV_W - 1):], sd, _rwkv_state_from_pairs(sr))):
                lst.append(val)
        else:
            (proj,) = _adaln_proj_call(x, m_l, lw["gain1"], lw["w_in"], pbb, ptb, False)
            pair_shape = (N_ODD, b, H_C // 2, 2 * DK_C, DV_C)
            oa, sg = _gla_call(proj, s_gla.reshape(pair_shape), i, lw["w_gk2"], lw["b_gk"], lw["gain_c"],
                               rbb, n, groups)
            ob, sc, sn, sm = _mlstm_call(proj, s_mc.reshape(pair_shape), s_mn.reshape(N_ODD, b, 1, H_D * DK_D), i,
                                         _pad_cols(s_mm[i], LANES).reshape(b, 1, LANES),
                                         lw["b_if"], lw["gain_d"], rbb, n, groups)
            for lst, val in zip(new_odd, (sg.reshape(b, H_C, DK_C, DV_C), sc.reshape(b, H_D, DK_D, DV_D),
                                          sn.reshape(b, H_D, DK_D), sm[:, 0, :H_D])):
                lst.append(val)
        x = _ffn_call(x, m_l, lw["gain2"], wts["wg"], wts["wu"], wts["wd"], (l, 1), wts["final_gain"], 6, fbb, ftb,
                      FFN_COLS, l == DEPTH - 1, mixer=(oa, ob, lw["w_out"]))
    stacked = [jnp.stack(lst) for lst in new_even + new_odd]
    return (x, *stacked)


def _prepare_weights(norm_gain, final_gain, w_ffn_gate, w_ffn_up, w_ffn_down, w_in_even, w_out_even,
                     conv_w, a_log, dt_bias, gain_a, mu_b, w0_b, w_w2, a0_b, w_a2, w_g2, k_k, k_a, r_k,
                     lnx_gain, lnx_bias, w_in_odd, w_out_odd, w_gk2, b_gk, gain_c, b_i, b_f, gain_d):
    layers = []
    for l in range(DEPTH):
        i = l // 2
        lw = {
            "gain0": _row(norm_gain[l, 0]), "gain1": _row(norm_gain[l, 1]), "gain2": _row(norm_gain[l, 2]),
        }
        if l % 2 == 0:
            lw["w_in"] = _even_in_weight(w_in_even[i])
            lw["w_out"] = w_out_even[i].astype(BF16)
            lw["conv_w"] = _pad_rows(conv_w[i].astype(F32), SUBLANES)
            lw["delta_hp"] = _pad_rows(jnp.stack([_pad_cols(_row(a_log[i]), LANES)[0],
                                                  _pad_cols(_row(dt_bias[i]), LANES)[0]]), SUBLANES)
            lw["gain_a"] = _row(gain_a[i])
            lw["mu_rkv"] = _row(mu_b[i, :1536])
            lw["mu_wag"] = _row(mu_b[i, 1536:])
            lora = jnp.zeros((2 * LANES, 3 * GROUP_W), F32)
            lora = lora.at[0:64, 0:GROUP_W].set(w_w2[i])
            lora = lora.at[64:128, GROUP_W:2 * GROUP_W].set(w_a2[i])
            lora = lora.at[128:256, 2 * GROUP_W:].set(w_g2[i])
            lw["w_lora"] = lora.astype(BF16)
            lw["rwkv_prm"] = jnp.stack([w0_b[i], a0_b[i], k_k[i], k_a[i], r_k[i].reshape(-1),
                                        lnx_gain[i], lnx_bias[i], jnp.zeros_like(w0_b[i])]).astype(F32)
        else:
            lw["w_in"] = _odd_in_weight(w_in_odd[i])
            lw["w_out"] = w_out_odd[i].astype(BF16)
            lw["w_gk2"] = _pad_rows(w_gk2[i], LANES).astype(BF16)
            lw["b_gk"] = _row(b_gk[i])
            lw["gain_c"] = _row(gain_c[i])
            lw["b_if"] = _pad_cols(_row(jnp.concatenate([b_i[i], b_f[i]])), LANES)
            lw["gain_d"] = _row(gain_d[i])
        layers.append(lw)
    return {"layers": layers, "final_gain": _row(final_gain), "wg": w_ffn_gate.astype(BF16),
            "wu": w_ffn_up.astype(BF16), "wd": w_ffn_down.astype(BF16)}


def kernel(x_prompt, x_sample, c_prompt, c_sample, state_xbuf_even, state_delta, state_rwkv, state_gla,
           state_mlstm_c, state_mlstm_n, state_mlstm_m, w_mod, b_mod, norm_gain, final_gain, w_ffn_gate,
           w_ffn_up, w_ffn_down, w_in_even, w_out_even, conv_w, a_log, dt_bias, gain_a, mu_b, w0_b, w_w2,
           a0_b, w_a2, w_g2, k_k, k_a, r_k, lnx_gain, lnx_bias, w_in_odd, w_out_odd, w_gk2, b_gk, gain_c,
           b_i, b_f, gain_d):
    wts = _prepare_weights(norm_gain, final_gain, w_ffn_gate, w_ffn_up, w_ffn_down, w_in_even, w_out_even,
                           conv_w, a_log, dt_bias, gain_a, mu_b, w0_b, w_w2, a0_b, w_a2, w_g2, k_k, k_a,
                           r_k, lnx_gain, lnx_bias, w_in_odd, w_out_odd, w_gk2, b_gk, gain_c, b_i, b_f,
                           gain_d)
    bp, bs = x_prompt.shape[0], x_sample.shape[0]
    c_all = jnp.concatenate([c_prompt, c_sample], axis=0).astype(F32)
    mod_p, mod_s = _mod_call(c_all, bp, w_mod, b_mod)

    def zeros(shape):
        return jnp.zeros(shape, F32)

    zero_states = (
        None,
        zeros((N_EVEN, bp, H_A, DK_A, DK_A)),
        zeros((N_EVEN, bp, H_B, HD_B, HD_B)),
        zeros((N_ODD, bp, H_C, DK_C, DV_C)),
        zeros((N_ODD, bp, H_D, DK_D, DV_D)),
        zeros((N_ODD, bp, H_D, DK_D)),
        zeros((N_ODD, bp, H_D)),
    )
    y_p, xb_p, dl_p, rw_p, gl_p, mc_p, mn_p, mm_p = _trunk(x_prompt, mod_p, zero_states, wts)
    sample_states = (state_xbuf_even, state_delta, state_rwkv, state_gla,
                     state_mlstm_c, state_mlstm_n, state_mlstm_m)
    y_s, xb_s, dl_s, rw_s, gl_s, mc_s, mn_s, mm_s = _trunk(x_sample, mod_s, sample_states, wts)
    return (y_p, y_s, xb_p, xb_s, dl_p, dl_s, rw_p, rw_s, gl_p, gl_s, mc_p, mc_s, mn_p, mn_s, mm_p, mm_s)
```
